```python
import math
import jax
import jax.numpy as jnp
from jax import lax
import numpy as np

D_MODEL = 1024
BATCH = 16
SEQ = 4096
DEPTH = 4

GRID_W = 64
CTX_LEN = 256
N_MIXERS = 3
N_MOD = 9
EPS = 1e-6
ROPE_BASE = 10000.0
D_FF = 128 * ((8 * D_MODEL + 3 * 128 - 1) // (3 * 128))

MLA_HEADS = D_MODEL // 128
MLA_NOPE = 128
MLA_ROPE = 64
MLA_V = 128
MLA_Q_RANK = 3 * D_MODEL // 8
MLA_KV_RANK = D_MODEL // 4
Q_BLOCK = 128

RET_HEADS = D_MODEL // 256
RET_DK = 256
RET_DV = 512
RET_CHUNK = 128

S5_GROUP = 16
S5_GROUPS = D_MODEL // S5_GROUP
S5_STATE = 64
S5_CHUNK = 128
DT_MIN = 1e-3
DT_MAX = 1e-1

N_MLA = (DEPTH + N_MIXERS - 1) // N_MIXERS
N_RET = (DEPTH + N_MIXERS - 2) // N_MIXERS
N_S5 = (DEPTH + N_MIXERS - 3) // N_MIXERS

kernel_name = 'hybrid_mla_retention_s5_dit'


def rms_norm(x, gain):
    xf = x.astype(jnp.float32)
    y = xf * lax.rsqrt(jnp.mean(xf * xf, axis=-1, keepdims=True) + EPS)
    return (y * gain.astype(jnp.float32)).astype(x.dtype)


def pre_norm(z, gain, mod, k):
    return rms_norm(z, gain) * (1 + mod[:, 3 * k + 1]) + mod[:, 3 * k]


def swiglu(x, w_in, w_out):
    a, b = jnp.split(x @ w_in, 2, axis=-1)
    return (jax.nn.silu(a) * b) @ w_out


def flip(a):
    return jnp.flip(a, axis=1)


def rope_angles(pos, dim):
    inv_freq = ROPE_BASE ** (-jnp.arange(0, dim, 2, dtype=jnp.float32) / dim)
    return pos.astype(jnp.float32)[:, None] * inv_freq[None, :]


def apply_rotary(x, ang):
    cos = jnp.cos(ang)[:, None, :]
    sin = jnp.sin(ang)[:, None, :]
    x1, x2 = jnp.split(x.astype(jnp.float32), 2, axis=-1)
    return jnp.concatenate([x1 * cos - x2 * sin, x2 * cos + x1 * sin], axis=-1).astype(x.dtype)


def grid_angles(n_tok, dim):
    rows = n_tok // GRID_W
    row = jnp.repeat(jnp.arange(rows), GRID_W)
    col = jnp.tile(jnp.arange(GRID_W), rows)
    return rope_angles(row, dim // 2), rope_angles(col, dim // 2)


def axial_rotary(x, angs):
    ang_row, ang_col = angs
    half = x.shape[-1] // 2
    return jnp.concatenate([apply_rotary(x[..., :half], ang_row), apply_rotary(x[..., half:], ang_col)], axis=-1)


def mla_qk_norm(x, gain):
    return jnp.concatenate([rms_norm(x[..., :MLA_NOPE], gain[:MLA_NOPE]),
                            rms_norm(x[..., MLA_NOPE:], gain[MLA_NOPE:])], axis=-1)


def mla_queries(cq, q_norm, w_uq, q_gain, angs):
    b, t, _ = cq.shape
    q = (rms_norm(cq, q_norm) @ w_uq).reshape(b, t, MLA_HEADS, MLA_NOPE + MLA_ROPE)
    q = mla_qk_norm(q, q_gain)
    if angs is None:
        return q
    return jnp.concatenate([q[..., :MLA_NOPE], axial_rotary(q[..., MLA_NOPE:], angs)], axis=-1)


def mla_keys_values(ckv, k_rope, kv_norm, w_ukv, k_gain, angs):
    b, t, _ = ckv.shape
    kv = (rms_norm(ckv, kv_norm) @ w_ukv).reshape(b, t, MLA_HEADS, MLA_NOPE + MLA_V)
    k_nope = rms_norm(kv[..., :MLA_NOPE], k_gain[:MLA_NOPE])
    k_rope = rms_norm(k_rope, k_gain[MLA_NOPE:])[:, :, None, :]
    if angs is not None:
        k_rope = axial_rotary(k_rope, angs)
    k = jnp.concatenate([k_nope, jnp.broadcast_to(k_rope, (b, t, MLA_HEADS, MLA_ROPE))], axis=-1)
    return k, kv[..., MLA_NOPE:]


def attend(q, k, v):
    s = jnp.einsum('bqhd,bkhd->bhqk', q, k, preferred_element_type=jnp.float32) * (q.shape[-1] ** -0.5)
    p = jax.nn.softmax(s, axis=-1).astype(v.dtype)
    return jnp.einsum('bhqk,bkhd->bqhd', p, v)


def blocked_attend(q, k, v):
    b, t, h, dk = q.shape
    nb = t // Q_BLOCK
    qb = q.reshape(b, nb, Q_BLOCK, h, dk).swapaxes(0, 1)
    o = lax.map(lambda qi: attend(qi, k, v), qb)
    return o.swapaxes(0, 1).reshape(b, t, h, v.shape[-1])


def mla_mixer(hn, gn, need_ctx, w_in, q_norm, kv_norm, w_uq, w_ukv, q_gain, k_gain, w_o):
    b, s, _ = hn.shape
    l = gn.shape[1]
    cuts = [MLA_Q_RANK, MLA_Q_RANK + MLA_KV_RANK]
    angs = grid_angles(s, MLA_ROPE)
    cq_l, ckv_l, kr_l = jnp.split(hn @ w_in, cuts, axis=-1)
    q_l = mla_queries(cq_l, q_norm, w_uq, q_gain, angs)
    k_l, v_l = mla_keys_values(ckv_l, kr_l, kv_norm, w_ukv, k_gain, angs)
    if need_ctx:
        cq_c, ckv_c, kr_c = jnp.split(gn @ w_in, cuts, axis=-1)
    else:
        ckv_c, kr_c = jnp.split(gn @ w_in[:, MLA_Q_RANK:], [MLA_KV_RANK], axis=-1)
    k_c, v_c = mla_keys_values(ckv_c, kr_c, kv_norm, w_ukv, k_gain, None)
    k_all = jnp.concatenate([k_c, k_l], axis=1)
    v_all = jnp.concatenate([v_c, v_l], axis=1)
    y_l = blocked_attend(q_l, k_all, v_all).reshape(b, s, MLA_HEADS * MLA_V) @ w_o
    y_c = None
    if need_ctx:
        q_c = mla_queries(cq_c, q_norm, w_uq, q_gain, None)
        y_c = attend(q_c, k_c, v_c).reshape(b, l, MLA_HEADS * MLA_V) @ w_o
    return y_l, y_c


def ret_heads(a, dim):
    return a.reshape(a.shape[0], a.shape[1], RET_HEADS, dim)


def retention_scan(q, k, v, log_g, state0, strict):
    b, t, h, _ = q.shape
    dv = v.shape[-1]
    c = RET_CHUNK
    n = t // c
    idx = jnp.arange(c, dtype=jnp.float32)
    diff = idx[:, None] - idx[None, :]
    lower = diff > 0 if strict else diff >= 0
    intra = jnp.where(lower[None], jnp.exp(log_g[:, None, None] * jnp.maximum(diff, 0.0)[None]), 0.0)
    q_dec = jnp.exp(log_g[None, :] * (idx[:, None] + 1.0))[None, :, :, None]
    k_dec = jnp.exp(log_g[None, :] * (c - 1.0 - idx[:, None]))[None, :, :, None]
    chunk_dec = jnp.exp(log_g * c)[None, :, None, None]

    def chunks(a):
        return a.astype(jnp.float32).reshape(b, n, c, h, a.shape[-1]).swapaxes(0, 1)

    def step(state, inp):
        qc, kc, vc = inp
        scores = jnp.einsum('bihd,bjhd->bhij', qc, kc) * intra[None]
        o = jnp.einsum('bhij,bjhe->bihe', scores, vc) + jnp.einsum('bihd,bhde->bihe', qc * q_dec, state)
        state = state * chunk_dec + jnp.einsum('bjhd,bjhe->bhde', kc * k_dec, vc)
        return state, o

    _, o = lax.scan(step, state0, (chunks(q), chunks(k), chunks(v)))
    return o.swapaxes(0, 1).reshape(b, t, h, dv)


def retention_state(k, v, log_g, expo):
    w = jnp.exp(expo[:, None] * log_g[None, :])
    return jnp.einsum('lh,blhd,blhe->bhde', w, k.astype(jnp.float32), v.astype(jnp.float32))


def retention_out(o, gate, gn_gain, w_o):
    b, t = o.shape[:2]
    mu = jnp.mean(o, axis=-1, keepdims=True)
    var = jnp.mean(jnp.square(o - mu), axis=-1, keepdims=True)
    on = ((o - mu) * lax.rsqrt(var + EPS)).reshape(b, t, RET_HEADS * RET_DV)
    y = on * gn_gain.astype(jnp.float32) * jax.nn.silu(gate.astype(jnp.float32))
    return y.astype(w_o.dtype) @ w_o


def retention_mixer(hn, gn, need_ctx, w_in, log1m_gamma, gn_gain, w_o):
    b, s, _ = hn.shape
    l = gn.shape[1]
    hdk, hdv = RET_HEADS * RET_DK, RET_HEADS * RET_DV
    cuts = [hdk, 2 * hdk, 2 * hdk + hdv]
    log_g = jnp.log1p(-jnp.exp(log1m_gamma.astype(jnp.float32)))
    scale = RET_DK ** -0.5
    q_l, k_l, v_l, gate_l = jnp.split(hn @ w_in, cuts, axis=-1)
    ang = rope_angles(jnp.arange(s), RET_DK)
    q_l = apply_rotary(ret_heads(q_l, RET_DK), ang)
    k_l = apply_rotary(ret_heads(k_l, RET_DK), ang) * scale
    v_l = ret_heads(v_l, RET_DV)
    if need_ctx:
        q_c, k_c, v_c, gate_c = jnp.split(gn @ w_in, cuts, axis=-1)
    else:
        k_c, v_c = jnp.split(gn @ w_in[:, hdk:2 * hdk + hdv], [hdk], axis=-1)
    k_c = ret_heads(k_c, RET_DK) * scale
    v_c = ret_heads(v_c, RET_DV)
    pos_c = jnp.arange(l, dtype=jnp.float32)
    s_fwd = retention_state(k_c, v_c, log_g[0], (l - 1) - pos_c)
    s_bwd = retention_state(k_c, v_c, log_g[1], pos_c)
    o_l = (retention_scan(q_l, k_l, v_l, log_g[0], s_fwd, False)
           + flip(retention_scan(flip(q_l), flip(k_l), flip(v_l), log_g[1], s_bwd, True)))
    y_l = retention_out(o_l, gate_l, gn_gain, w_o)
    y_c = None
    if need_ctx:
        q_c = ret_heads(q_c, RET_DK)
        zero = jnp.zeros_like(s_fwd)
        o_c = (retention_scan(q_c, k_c, v_c, log_g[0], zero, False)
               + flip(retention_scan(flip(q_c), flip(k_c), flip(v_c), log_g[1], zero, True)))
        y_c = retention_out(o_c, gate_c, gn_gain, w_o)
    return y_l, y_c


def s5_discretise(lam_re, lam_im, log_dt, b_re, b_im, c_re, c_im):
    f32 = jnp.float32
    lam = lax.complex(lam_re.astype(f32), lam_im.astype(f32))
    a_log = lam * jnp.exp(log_dt.astype(f32))[:, None]
    b_bar = ((jnp.exp(a_log) - 1.0) / lam)[..., None] * lax.complex(b_re.astype(f32), b_im.astype(f32))
    c_mat = lax.complex(c_re.astype(f32), c_im.astype(f32))
    return a_log, b_bar, c_mat


def ssm_combine(e1, e2):
    a1, b1 = e1
    a2, b2 = e2
    return a1 * a2, a2 * b1 + b2


def s5_state(u, a_log, b_bar, expo):
    pw = jnp.exp(a_log[None] * expo[:, None, None])
    return jnp.einsum('lgp,gpm,blgm->bgp', pw, b_bar, u.astype(jnp.complex64))


def s5_scan(u, a_log, b_bar, c_mat, h0):
    b, t, g, m = u.shape
    n = t // S5_CHUNK
    a_bar = jnp.exp(a_log)

    def step(h_prev, uc):
        bu = jnp.einsum('gpm,bcgm->bcgp', b_bar, uc.astype(jnp.complex64))
        a_cum, h_loc = lax.associative_scan(ssm_combine, (jnp.broadcast_to(a_bar, bu.shape), bu), axis=1)
        hs = h_loc + a_cum * h_prev[:, None]
        y = jnp.einsum('gmp,bcgp->bcgm', c_mat, hs).real
        return hs[:, -1], y

    _, y = lax.scan(step, h0, u.reshape(b, n, S5_CHUNK, g, m).swapaxes(0, 1))
    return y.swapaxes(0, 1).reshape(b, t, g, m)


def s5_out(y, glu_w, glu_b):
    z = jax.nn.gelu(y).astype(glu_w.dtype)
    a, gate = jnp.split(z @ glu_w + glu_b, 2, axis=-1)
    return a * jax.nn.sigmoid(gate)


def s5_mixer(hn, gn, need_ctx, lam_re, lam_im, log_dt, b_re, b_im, c_re, c_im, d_skip, glu_w, glu_b):
    b, s, d = hn.shape
    l = gn.shape[1]
    fwd = s5_discretise(lam_re[0], lam_im[0], log_dt[0], b_re[0], b_im[0], c_re[0], c_im[0])
    bwd = s5_discretise(lam_re[1], lam_im[1], log_dt[1], b_re[1], b_im[1], c_re[1], c_im[1])
    u_l = hn.astype(jnp.float32).reshape(b, s, S5_GROUPS, S5_GROUP)
    u_c = gn.astype(jnp.float32).reshape(b, l, S5_GROUPS, S5_GROUP)
    d_g = d_skip.astype(jnp.float32).reshape(S5_GROUPS, S5_GROUP)
    pos_c = jnp.arange(l, dtype=jnp.float32)
    h_fwd = s5_state(u_c, fwd[0], fwd[1], (l - 1) - pos_c)
    h_bwd = s5_state(u_c, bwd[0], bwd[1], pos_c)
    y_l = s5_scan(u_l, *fwd, h_fwd) + flip(s5_scan(flip(u_l), *bwd, h_bwd)) + d_g * u_l
    out_l = s5_out(y_l.reshape(b, s, d), glu_w, glu_b)
    out_c = None
    if need_ctx:
        zero = jnp.zeros_like(h_fwd)
        y_c = s5_scan(u_c, *fwd, zero) + flip(s5_scan(flip(u_c), *bwd, zero)) + d_g * u_c
        out_c = s5_out(y_c.reshape(b, l, d), glu_w, glu_b)
    return out_l, out_c


def setup_inputs(seed: int = 0) -> dict:
    key = jax.random.key(seed)
    keys = iter(jax.random.split(key, 40))
    f32 = jnp.float32

    def nrm(shape, std):
        return jax.random.normal(next(keys), shape, f32) * std

    def gain(shape):
        return 1.0 + nrm(shape, 0.01)

    d = D_MODEL
    hq = MLA_HEADS * (MLA_NOPE + MLA_ROPE)
    hkv = MLA_HEADS * (MLA_NOPE + MLA_V)
    hdk, hdv = RET_HEADS * RET_DK, RET_HEADS * RET_DV
    gp = (N_S5, 2, S5_GROUPS, S5_STATE)
    decay0 = -(5.0 + jnp.arange(RET_HEADS, dtype=f32)) * math.log(2.0)
    return {
        'x': nrm((BATCH, SEQ, d), 1.0),
        'c': nrm((BATCH, d), 1.0),
        'ctx': nrm((BATCH, CTX_LEN, d), 1.0),
        'c_ctx': nrm((d,), 1.0),
        'ada_w': nrm((DEPTH, d, N_MOD * d), 0.5 * d ** -0.5),
        'ada_b': nrm((DEPTH, N_MOD * d), 0.01),
        'norm_g': gain((DEPTH, 3, d)),
        'ffn_w_in': nrm((DEPTH, 2, d, 2 * D_FF), d ** -0.5),
        'ffn_w_out': nrm((DEPTH, 2, D_FF, d), D_FF ** -0.5),
        'mla_w_in': nrm((N_MLA, d, MLA_Q_RANK + MLA_KV_RANK + MLA_ROPE), d ** -0.5),
        'mla_q_norm': gain((N_MLA, MLA_Q_RANK)),
        'mla_kv_norm': gain((N_MLA, MLA_KV_RANK)),
        'mla_w_uq': nrm((N_MLA, MLA_Q_RANK, hq), MLA_Q_RANK ** -0.5),
        'mla_w_ukv': nrm((N_MLA, MLA_KV_RANK, hkv), MLA_KV_RANK ** -0.5),
        'mla_q_gain': gain((N_MLA, MLA_NOPE + MLA_ROPE)),
        'mla_k_gain': gain((N_MLA, MLA_NOPE + MLA_ROPE)),
        'mla_w_o': nrm((N_MLA, MLA_HEADS * MLA_V, d), (MLA_HEADS * MLA_V) ** -0.5),
        'ret_w_in': nrm((N_RET, d, 2 * hdk + 2 * hdv), d ** -0.5),
        'ret_log1m_gamma': jnp.broadcast_to(decay0, (N_RET, 2, RET_HEADS)) + nrm((N_RET, 2, RET_HEADS), 0.05),
        'ret_gn_gain': gain((N_RET, hdv)),
        'ret_w_o': nrm((N_RET, hdv, d), hdv ** -0.5),
        's5_lam_re': -0.5 + nrm(gp, 0.01),
        's5_lam_im': jnp.pi * jnp.arange(S5_STATE, dtype=f32) + nrm(gp, 0.01),
        's5_log_dt': jax.random.uniform(next(keys), (N_S5, 2, S5_GROUPS), f32, math.log(DT_MIN), math.log(DT_MAX)),
        's5_b_re': nrm(gp + (S5_GROUP,), (2 * S5_GROUP) ** -0.5),
        's5_b_im': nrm(gp + (S5_GROUP,), (2 * S5_GROUP) ** -0.5),
        's5_c_re': nrm((N_S5, 2, S5_GROUPS, S5_GROUP, S5_STATE), S5_STATE ** -0.5),
        's5_c_im': nrm((N_S5, 2, S5_GROUPS, S5_GROUP, S5_STATE), S5_STATE ** -0.5),
        's5_d': nrm((N_S5, d), 1.0),
        's5_glu_w': nrm((N_S5, d, 2 * d), d ** -0.5),
        's5_glu_b': nrm((N_S5, 2 * d), 0.01),
    }


def reference(x, c, ctx, c_ctx, ada_w, ada_b, norm_g, ffn_w_in, ffn_w_out,
              mla_w_in, mla_q_norm, mla_kv_norm, mla_w_uq, mla_w_ukv, mla_q_gain, mla_k_gain, mla_w_o,
              ret_w_in, ret_log1m_gamma, ret_gn_gain, ret_w_o,
              s5_lam_re, s5_lam_im, s5_log_dt, s5_b_re, s5_b_im, s5_c_re, s5_c_im, s5_d, s5_glu_w, s5_glu_b):
    b, s, d = x.shape
    silu_c = jax.nn.silu(c)
    silu_cc = jax.nn.silu(c_ctx)[None, :]
    h, g = x, ctx
    for i in range(DEPTH):
        last = i == DEPTH - 1
        kind, j = i % N_MIXERS, i // N_MIXERS
        mod_l = (silu_c @ ada_w[i] + ada_b[i]).reshape(b, N_MOD, 1, d)
        mod_c = (silu_cc @ ada_w[i] + ada_b[i]).reshape(1, N_MOD, 1, d)
        h = h + 0.5 * mod_l[:, 2] * swiglu(pre_norm(h, norm_g[i, 0], mod_l, 0), ffn_w_in[i, 0], ffn_w_out[i, 0])
        g = g + 0.5 * mod_c[:, 2] * swiglu(pre_norm(g, norm_g[i, 0], mod_c, 0), ffn_w_in[i, 0], ffn_w_out[i, 0])
        hn = pre_norm(h, norm_g[i, 1], mod_l, 1)
        gn = pre_norm(g, norm_g[i, 1], mod_c, 1)
        if kind == 0:
            y_l, y_c = mla_mixer(hn, gn, not last, mla_w_in[j], mla_q_norm[j], mla_kv_norm[j], mla_w_uq[j],
                                 mla_w_ukv[j], mla_q_gain[j], mla_k_gain[j], mla_w_o[j])
        elif kind == 1:
            y_l, y_c = retention_mixer(hn, gn, not last, ret_w_in[j], ret_log1m_gamma[j], ret_gn_gain[j], ret_w_o[j])
        else:
            y_l, y_c = s5_mixer(hn, gn, not last, s5_lam_re[j], s5_lam_im[j], s5_log_dt[j], s5_b_re[j], s5_b_im[j],
                                s5_c_re[j], s5_c_im[j], s5_d[j], s5_glu_w[j], s5_glu_b[j])
        h = h + mod_l[:, 5] * y_l
        h = h + 0.5 * mod_l[:, 8] * swiglu(pre_norm(h, norm_g[i, 2], mod_l, 2), ffn_w_in[i, 1], ffn_w_out[i, 1])
        if not last:
            g = g + mod_c[:, 5] * y_c
            g = g + 0.5 * mod_c[:, 8] * swiglu(pre_norm(g, norm_g[i, 2], mod_c, 2), ffn_w_in[i, 1], ffn_w_out[i, 1])
    return h
```

```python
import functools
import math

import jax
import jax.numpy as jnp
import numpy as np
from jax import lax
from jax.experimental import pallas as pl
from jax.experimental.pallas import tpu as pltpu

F32 = jnp.float32
BF16 = jnp.bfloat16

EPS = 1e-6
ROPE_BASE = 10000.0
GRID_W = 64
N_MIXERS = 3
N_MOD = 9

MLA_HEADS = 8
MLA_NOPE = 128
MLA_ROPE = 64
MLA_V = 128
MLA_Q_RANK = 384
MLA_KV_RANK = 256
MLA_HEAD_PAD = 256

RET_HEADS = 4
RET_DK = 256
RET_DV = 512
SCAN_CHUNK = 128

S5_GROUP = 16
S5_STATE = 64

LANES = 128
TOKEN_TILE = 256
ATTN_KV_CHUNK = 256
VMEM_LIMIT_BYTES = 56 * 1024 * 1024


def _cparams(*sem):
    return pltpu.CompilerParams(dimension_semantics=sem, vmem_limit_bytes=VMEM_LIMIT_BYTES)


def _resident(shape):
    zeros = (0,) * len(shape)
    return pl.BlockSpec(shape, lambda *_: zeros)


def _silu(x):
    return x * jax.nn.sigmoid(x)


def _rms(x, n):
    return x * lax.rsqrt(jnp.sum(x * x, axis=-1, keepdims=True) * (1.0 / n) + EPS)


def _pre_norm(x, gain, mod_ref, k):
    y = _rms(x, x.shape[-1]) * gain
    return y * (1.0 + mod_ref[3 * k + 1:3 * k + 2, :]) + mod_ref[3 * k:3 * k + 1, :]


def _dot(a, b):
    return jnp.dot(a, b, preferred_element_type=F32)


def _dot_nt(a, b):
    return lax.dot_general(a, b, (((1,), (1,)), ((), ())), preferred_element_type=F32)


def _split3(x):
    hi = x.astype(BF16)
    r1 = x - hi.astype(F32)
    mid = r1.astype(BF16)
    lo = (r1 - mid.astype(F32)).astype(BF16)
    return hi, mid, lo


def _dot_select(x, sel):
    hi, mid, lo = _split3(x)
    return _dot(hi, sel) + _dot(mid, sel) + _dot(lo, sel)


def _mod_kernel(cv_ref, w_ref, b_ref, o_ref):
    s = _silu(cv_ref[...]).astype(BF16)
    o_ref[...] = _dot(s, w_ref[...].astype(BF16)) + b_ref[...]


def _modulation(cvec, ada_w, ada_b):
    depth, d, n = ada_w.shape
    rows = cvec.shape[0]
    tn = n // 4
    return pl.pallas_call(
        _mod_kernel,
        grid=(depth, n // tn),
        in_specs=[
            _resident((rows, d)),
            pl.BlockSpec((None, d, tn), lambda i, j: (i, 0, j)),
            pl.BlockSpec((None, 1, tn), lambda i, j: (i, 0, j)),
        ],
        out_specs=pl.BlockSpec((None, rows, tn), lambda i, j: (i, 0, j)),
        out_shape=jax.ShapeDtypeStruct((depth, rows, n), F32),
        compiler_params=_cparams("arbitrary", "arbitrary"),
        name="adaln_modulation",
    )(cvec, ada_w, ada_b.reshape(depth, 1, n))


def _row_call(kernel, name, r_like, row_inputs, mod, consts, outs, n_ctx_tiles):
    b, t, _ = r_like.shape
    tt = TOKEN_TILE
    in_specs = [pl.BlockSpec((None, tt, a.shape[-1]), lambda i, j: (i, j, 0)) for a in row_inputs]
    in_specs.append(pl.BlockSpec((None, None, N_MOD, mod.shape[-1]),
                                 lambda i, j: (i, (j >= n_ctx_tiles).astype(jnp.int32), 0, 0)))
    in_specs += [_resident(a.shape) for a in consts]
    out_specs = [pl.BlockSpec((None, tt, w), lambda i, j: (i, j, 0)) for w, _ in outs]
    out_shape = [jax.ShapeDtypeStruct((b, t, w), dt) for w, dt in outs]
    res = pl.pallas_call(
        kernel,
        grid=(b, t // tt),
        in_specs=in_specs,
        out_specs=out_specs,
        out_shape=out_shape,
        compiler_params=_cparams("arbitrary", "arbitrary"),
        name=name,
    )(*row_inputs, mod, *consts)
    return res


def _ffn_kernel(r_ref, mod_ref, g_ref, win_ref, wout_ref, o_ref, *, k):
    x = r_ref[...]
    xn = _pre_norm(x, g_ref[...], mod_ref, k).astype(BF16)
    f = wout_ref.shape[0]
    a = _dot(xn, win_ref[:, :f])
    b = _dot(xn, win_ref[:, f:])
    hm = (_silu(a) * b).astype(BF16)
    y = _dot(hm, wout_ref[...])
    o_ref[...] = x + (0.5 * mod_ref[3 * k + 2:3 * k + 3, :]) * y


def _ffn(r, mod, gain, w_in, w_out, k, n_ctx_tiles):
    d = r.shape[-1]
    (out,) = _row_call(functools.partial(_ffn_kernel, k=k), "swiglu_half_step", r, [r], mod,
                       [gain.reshape(1, d), w_in, w_out], [(d, F32)], n_ctx_tiles)
    return out


def _proj_res_kernel(r_ref, y_ref, mod_ref, w_ref, o_ref):
    y = _dot(y_ref[...], w_ref[...])
    o_ref[...] = r_ref[...] + mod_ref[5:6, :] * y


def _proj_res(r, y, mod, w, n_ctx_tiles):
    (out,) = _row_call(_proj_res_kernel, "mixer_out_proj", r, [r, y], mod, [w],
                       [(r.shape[-1], F32)], n_ctx_tiles)
    return out


def _gelu_tanh(x):
    cdf = 0.5 * (1.0 + jnp.tanh(math.sqrt(2.0 / math.pi) * (x + 0.044715 * (x * x * x))))
    return x * cdf


def _glu_res_kernel(r_ref, y_ref, mod_ref, w_ref, b_ref, o_ref):
    d = r_ref.shape[-1]
    z = _gelu_tanh(y_ref[...]).astype(BF16)
    ag = _dot(z, w_ref[...]) + b_ref[...]
    o_ref[...] = r_ref[...] + mod_ref[5:6, :] * (ag[:, :d] * jax.nn.sigmoid(ag[:, d:]))


def _glu_res(r, y, mod, w, bias, n_ctx_tiles):
    (out,) = _row_call(_glu_res_kernel, "s5_glu_out", r, [r, y], mod, [w, bias.reshape(1, -1)],
                       [(r.shape[-1], F32)], n_ctx_tiles)
    return out


def _norm_kernel(r_ref, mod_ref, g_ref, o_ref):
    o_ref[...] = _pre_norm(r_ref[...], g_ref[...], mod_ref, 1)


def _norm_only(r, mod, gain, n_ctx_tiles):
    d = r.shape[-1]
    (out,) = _row_call(_norm_kernel, "s5_pre_norm", r, [r], mod, [gain.reshape(1, d)], [(d, F32)],
                       n_ctx_tiles)
    return out


def _rope_tile_perm():
    src = np.full((LANES,), -1, np.int64)
    src[0:16] = np.arange(0, 16)
    src[16:32] = np.arange(32, 48)
    src[64:80] = np.arange(16, 32)
    src[80:96] = np.arange(48, 64)
    return src


def _pad_rope_cols(w):
    src = _rope_tile_perm()
    cols = jnp.take(w, jnp.asarray(np.maximum(src, 0)), axis=-1)
    return jnp.where(jnp.asarray(src >= 0), cols, 0.0)


def _mla_tables(n_ctx, n_lat):
    half = MLA_ROPE // 2
    inv = ROPE_BASE ** (-jnp.arange(0, half, 2, dtype=F32) / half)
    pos = jnp.arange(n_lat)
    ang_r = (pos // GRID_W).astype(F32)[:, None] * inv[None, :]
    ang_c = (pos % GRID_W).astype(F32)[:, None] * inv[None, :]
    z = jnp.zeros((n_lat, 32), F32)
    cos = jnp.concatenate([jnp.cos(ang_r), jnp.cos(ang_c), z, jnp.cos(ang_r), jnp.cos(ang_c), z], axis=-1)
    sin = jnp.concatenate([-jnp.sin(ang_r), -jnp.sin(ang_c), z, jnp.sin(ang_r), jnp.sin(ang_c), z], axis=-1)
    cos = jnp.concatenate([jnp.ones((n_ctx, LANES), F32), cos], axis=0)
    sin = jnp.concatenate([jnp.zeros((n_ctx, LANES), F32), sin], axis=0)
    return cos, sin


def _mla_proj_kernel(r_ref, cos_ref, sin_ref, mod_ref, g_ref, win_ref, qn_ref, kvn_ref, wuq_ref, wukv_ref,
                     qg_ref, kg_ref, q_ref, k_ref, v_ref):
    xn = _pre_norm(r_ref[...], g_ref[...], mod_ref, 1).astype(BF16)
    p = _dot(xn, win_ref[...])
    cos = cos_ref[...]
    sin = sin_ref[...]

    def rotate(x):
        return x * cos + pltpu.roll(x, LANES // 2, 1) * sin

    cq = (_rms(p[:, :MLA_Q_RANK], MLA_Q_RANK) * qn_ref[...]).astype(BF16)
    ckv = (_rms(p[:, MLA_Q_RANK:MLA_Q_RANK + MLA_KV_RANK], MLA_KV_RANK) * kvn_ref[...]).astype(BF16)
    kr = rotate(_rms(p[:, MLA_Q_RANK + MLA_KV_RANK:], MLA_ROPE) * kg_ref[:, LANES:]).astype(BF16)
    q = _dot(cq, wuq_ref[...])
    kv = _dot(ckv, wukv_ref[...])
    scale = (MLA_NOPE + MLA_ROPE) ** -0.5
    for h in range(MLA_HEADS):
        a0 = h * MLA_HEAD_PAD
        qa = _rms(q[:, a0:a0 + LANES], MLA_NOPE) * qg_ref[:, :LANES]
        qb = rotate(_rms(q[:, a0 + LANES:a0 + 2 * LANES], MLA_ROPE) * qg_ref[:, LANES:])
        q_ref[:, a0:a0 + LANES] = (qa * scale).astype(BF16)
        q_ref[:, a0 + LANES:a0 + 2 * LANES] = (qb * scale).astype(BF16)
        kn = _rms(kv[:, h * LANES:(h + 1) * LANES], MLA_NOPE) * kg_ref[:, :LANES]
        k_ref[:, a0:a0 + LANES] = kn.astype(BF16)
        k_ref[:, a0 + LANES:a0 + 2 * LANES] = kr
    v_ref[...] = kv[:, MLA_HEADS * LANES:].astype(BF16)


def _mla_weights(w_in, q_norm, kv_norm, w_uq, w_ukv, q_gain, k_gain):
    lat = MLA_Q_RANK + MLA_KV_RANK
    w_in_p = jnp.concatenate([w_in[:, :lat], _pad_rope_cols(w_in[:, lat:])], axis=-1).astype(BF16)
    uq = w_uq.reshape(MLA_Q_RANK, MLA_HEADS, MLA_NOPE + MLA_ROPE)
    uq = jnp.concatenate([uq[..., :MLA_NOPE], _pad_rope_cols(uq[..., MLA_NOPE:])], axis=-1)
    uq = uq.reshape(MLA_Q_RANK, MLA_HEADS * MLA_HEAD_PAD).astype(BF16)
    ukv = w_ukv.reshape(MLA_KV_RANK, MLA_HEADS, MLA_NOPE + MLA_V)
    ukv = jnp.concatenate([ukv[..., :MLA_NOPE].reshape(MLA_KV_RANK, -1),
                           ukv[..., MLA_NOPE:].reshape(MLA_KV_RANK, -1)], axis=-1).astype(BF16)
    qg = jnp.concatenate([q_gain[:MLA_NOPE], _pad_rope_cols(q_gain[MLA_NOPE:])]).reshape(1, -1)
    kg = jnp.concatenate([k_gain[:MLA_NOPE], _pad_rope_cols(k_gain[MLA_NOPE:])]).reshape(1, -1)
    return [w_in_p, q_norm.reshape(1, -1), kv_norm.reshape(1, -1), uq, ukv, qg, kg]


def _mla_proj(r, cos, sin, mod, gain, weights, n_ctx_tiles):
    b, t, d = r.shape
    cos_b = jnp.broadcast_to(cos[None], (1,) + cos.shape)
    sin_b = jnp.broadcast_to(sin[None], (1,) + sin.shape)
    tt = TOKEN_TILE
    hw = MLA_HEADS * MLA_HEAD_PAD
    consts = [gain.reshape(1, d)] + weights
    in_specs = [pl.BlockSpec((None, tt, d), lambda i, j: (i, j, 0)),
                pl.BlockSpec((None, tt, LANES), lambda i, j: (0, j, 0)),
                pl.BlockSpec((None, tt, LANES), lambda i, j: (0, j, 0)),
                pl.BlockSpec((None, None, N_MOD, d),
                             lambda i, j: (i, (j >= n_ctx_tiles).astype(jnp.int32), 0, 0))]
    in_specs += [_resident(a.shape) for a in consts]
    widths = [(hw, BF16), (hw, BF16), (MLA_HEADS * MLA_V, BF16)]
    return pl.pallas_call(
        _mla_proj_kernel,
        grid=(b, t // tt),
        in_specs=in_specs,
        out_specs=[pl.BlockSpec((None, tt, w), lambda i, j: (i, j, 0)) for w, _ in widths],
        out_shape=[jax.ShapeDtypeStruct((b, t, w), dt) for w, dt in widths],
        compiler_params=_cparams("arbitrary", "arbitrary"),
        name="mla_projections",
    )(r, cos_b, sin_b, mod, *consts)


def _attn_kernel(q_ref, k_ref, v_ref, o_ref, *, n_ctx_tiles, n_ctx_chunks, n_chunks):
    kc = ATTN_KV_CHUNK
    q = q_ref[...]
    tq = q.shape[0]
    n_k = jnp.where(pl.program_id(2) < n_ctx_tiles, n_ctx_chunks, n_chunks)

    def body(j, carry):
        m, l, acc = carry
        off = pl.multiple_of(j * kc, kc)
        s = _dot_nt(q, k_ref[pl.ds(off, kc), :])
        m_new = jnp.maximum(m, jnp.max(s, axis=-1, keepdims=True))
        alpha = jnp.exp(m - m_new)
        p = jnp.exp(s - m_new)
        l = alpha * l + jnp.sum(p, axis=-1, keepdims=True)
        acc = alpha * acc + _dot(p.astype(BF16), v_ref[pl.ds(off, kc), :])
        return m_new, l, acc

    init = (jnp.full((tq, 1), -1e30, F32), jnp.zeros((tq, 1), F32), jnp.zeros((tq, MLA_V), F32))
    _, l, acc = lax.fori_loop(0, n_k, body, init)
    o_ref[...] = (acc / l).astype(o_ref.dtype)


def _attention(q, k, v, n_ctx):
    b, t, _ = q.shape
    tq = TOKEN_TILE
    kern = functools.partial(_attn_kernel, n_ctx_tiles=n_ctx // tq, n_ctx_chunks=n_ctx // ATTN_KV_CHUNK,
                             n_chunks=t // ATTN_KV_CHUNK)
    return pl.pallas_call(
        kern,
        grid=(b, MLA_HEADS, t // tq),
        in_specs=[pl.BlockSpec((None, tq, MLA_HEAD_PAD), lambda i, h, j: (i, j, h)),
                  pl.BlockSpec((None, t, MLA_HEAD_PAD), lambda i, h, j: (i, 0, h)),
                  pl.BlockSpec((None, t, MLA_V), lambda i, h, j: (i, 0, h))],
        out_specs=pl.BlockSpec((None, tq, MLA_V), lambda i, h, j: (i, j, h)),
        out_shape=jax.ShapeDtypeStruct((b, t, MLA_HEADS * MLA_V), BF16),
        compiler_params=_cparams("arbitrary", "arbitrary", "arbitrary"),
        name="mla_attention",
    )(q, k, v)


def _ret_tables(n_ctx, n_lat):
    inv = ROPE_BASE ** (-jnp.arange(0, RET_DK, 2, dtype=F32) / RET_DK)
    ang = jnp.arange(n_lat, dtype=F32)[:, None] * inv[None, :]
    cos = jnp.concatenate([jnp.ones((n_ctx, LANES), F32), jnp.cos(ang)], axis=0)
    sin = jnp.concatenate([jnp.zeros((n_ctx, LANES), F32), jnp.sin(ang)], axis=0)
    return cos, sin


def _ret_proj_kernel(r_ref, cos_ref, sin_ref, mod_ref, g_ref, win_ref, q_ref, k_ref, v_ref, sg_ref):
    xn = _pre_norm(r_ref[...], g_ref[...], mod_ref, 1).astype(BF16)
    cos = cos_ref[...]
    sin = sin_ref[...]
    hdk = RET_HEADS * RET_DK
    hdv = RET_HEADS * RET_DV
    scale = RET_DK ** -0.5
    for out_ref, base, sc in ((q_ref, 0, None), (k_ref, hdk, scale)):
        for h in range(RET_HEADS):
            c0 = h * RET_DK
            x1 = _dot(xn, win_ref[:, base + c0:base + c0 + LANES])
            x2 = _dot(xn, win_ref[:, base + c0 + LANES:base + c0 + 2 * LANES])
            o1 = x1 * cos - x2 * sin
            o2 = x2 * cos + x1 * sin
            if sc is not None:
                o1 = o1 * sc
                o2 = o2 * sc
            out_ref[:, c0:c0 + LANES] = o1.astype(BF16)
            out_ref[:, c0 + LANES:c0 + 2 * LANES] = o2.astype(BF16)
    v_ref[...] = _dot(xn, win_ref[:, 2 * hdk:2 * hdk + hdv]).astype(BF16)
    sg_ref[...] = _silu(_dot(xn, win_ref[:, 2 * hdk + hdv:])).astype(BF16)


def _ret_proj(r, cos, sin, mod, gain, w_in, n_ctx_tiles):
    b, t, d = r.shape
    tt = TOKEN_TILE
    consts = [gain.reshape(1, d), w_in]
    in_specs = [pl.BlockSpec((None, tt, d), lambda i, j: (i, j, 0)),
                pl.BlockSpec((tt, LANES), lambda i, j: (j, 0)),
                pl.BlockSpec((tt, LANES), lambda i, j: (j, 0)),
                pl.BlockSpec((None, None, N_MOD, d),
                             lambda i, j: (i, (j >= n_ctx_tiles).astype(jnp.int32), 0, 0))]
    in_specs += [_resident(a.shape) for a in consts]
    widths = [(RET_HEADS * RET_DK, BF16), (RET_HEADS * RET_DK, BF16),
              (RET_HEADS * RET_DV, BF16), (RET_HEADS * RET_DV, BF16)]
    return pl.pallas_call(
        _ret_proj_kernel,
        grid=(b, t // tt),
        in_specs=in_specs,
        out_specs=[pl.BlockSpec((None, tt, w), lambda i, j: (i, j, 0)) for w, _ in widths],
        out_shape=[jax.ShapeDtypeStruct((b, t, w), dt) for w, dt in widths],
        compiler_params=_cparams("arbitrary", "arbitrary"),
        name="retention_projections",
    )(r, cos, sin, mod, *consts)


def _ret_scan_kernel(l1g_ref, q_ref, k_ref, v_ref, sg_ref, gain_ref, y_ref, ob_scr, st_scr, *, n_ctx, n_all):
    c = SCAN_CHUNK
    log_g = jnp.log1p(-jnp.exp(l1g_ref[...]))
    lgf = log_g[0:1, 0:1]
    lgb = log_g[1:2, 0:1]
    ii = lax.broadcasted_iota(jnp.int32, (c, c), 0)
    jj = lax.broadcasted_iota(jnp.int32, (c, c), 1)
    diff = (ii - jj).astype(F32)
    mask = jnp.where(diff >= 0.0, jnp.exp(lgf * jnp.maximum(diff, 0.0)), jnp.exp(lgb * jnp.maximum(-diff, 0.0)))
    ic = lax.broadcasted_iota(jnp.int32, (c, 1), 0).astype(F32)
    qdec_f = jnp.exp(lgf * (ic + 1.0))
    qdec_b = jnp.exp(lgb * (c - ic))
    kdec_f = jnp.exp(lgf * (c - 1.0 - ic))
    kdec_b = jnp.exp(lgb * ic)
    cdec_f = jnp.exp(lgf * c)
    cdec_b = jnp.exp(lgb * c)

    def kv_outer(kc, dec, vc):
        return _dot((kc.astype(F32) * dec).T.astype(BF16), vc)

    st_scr[...] = jnp.zeros_like(st_scr)

    def sweep_b(idx, carry):
        n = jnp.where(idx < n_ctx, n_ctx - 1 - idx, n_all - 1 - (idx - n_ctx))
        off = pl.multiple_of(n * c, c)
        qc = q_ref[pl.ds(off, c), :]
        state = st_scr[...]
        ob_scr[pl.ds(off, c), :] = _dot((qc.astype(F32) * qdec_b).astype(BF16), state.astype(BF16))
        st_scr[...] = state * cdec_b + kv_outer(k_ref[pl.ds(off, c), :], kdec_b, v_ref[pl.ds(off, c), :])
        return carry

    lax.fori_loop(0, n_all, sweep_b, 0)

    st_scr[...] = jnp.zeros_like(st_scr)
    gain = gain_ref[...]

    def sweep_f(n, carry):
        off = pl.multiple_of(n * c, c)
        qc = q_ref[pl.ds(off, c), :]
        kc = k_ref[pl.ds(off, c), :]
        vc = v_ref[pl.ds(off, c), :]
        state = st_scr[...]
        scores = _dot_nt(qc, kc) * mask
        o = (_dot(scores.astype(BF16), vc)
             + _dot((qc.astype(F32) * qdec_f).astype(BF16), state.astype(BF16))
             + ob_scr[pl.ds(off, c), :])
        mu = jnp.mean(o, axis=-1, keepdims=True)
        dev = o - mu
        on = dev * lax.rsqrt(jnp.mean(dev * dev, axis=-1, keepdims=True) + EPS)
        y_ref[pl.ds(off, c), :] = (on * gain * sg_ref[pl.ds(off, c), :].astype(F32)).astype(y_ref.dtype)
        st_scr[...] = state * cdec_f + kv_outer(kc, kdec_f, vc)
        return carry

    lax.fori_loop(0, n_all, sweep_f, 0)


def _ret_scan(q, k, v, sg, log1m_gamma, gn_gain, n_ctx):
    b, t, _ = q.shape
    l1g = jnp.broadcast_to(log1m_gamma.T[:, :, None], (RET_HEADS, 2, LANES))
    l1g = jnp.concatenate([l1g, jnp.zeros((RET_HEADS, 6, LANES), F32) - 1.0], axis=1)
    kern = functools.partial(_ret_scan_kernel, n_ctx=n_ctx // SCAN_CHUNK, n_all=t // SCAN_CHUNK)
    return pl.pallas_call(
        kern,
        grid=(b, RET_HEADS),
        in_specs=[pl.BlockSpec((None, 8, LANES), lambda i, h: (h, 0, 0)),
                  pl.BlockSpec((None, t, RET_DK), lambda i, h: (i, 0, h)),
                  pl.BlockSpec((None, t, RET_DK), lambda i, h: (i, 0, h)),
                  pl.BlockSpec((None, t, RET_DV), lambda i, h: (i, 0, h)),
                  pl.BlockSpec((None, t, RET_DV), lambda i, h: (i, 0, h)),
                  pl.BlockSpec((1, RET_DV), lambda i, h: (0, h))],
        out_specs=pl.BlockSpec((None, t, RET_DV), lambda i, h: (i, 0, h)),
        out_shape=jax.ShapeDtypeStruct((b, t, RET_HEADS * RET_DV), BF16),
        scratch_shapes=[pltpu.VMEM((t, RET_DV), F32), pltpu.VMEM((RET_DK, RET_DV), F32)],
        compiler_params=_cparams("arbitrary", "arbitrary"),
        name="retention_scan",
    )(l1g, q, k, v, sg, gn_gain.reshape(1, -1))


def _s5_constants():
    c, m = SCAN_CHUNK, S5_GROUP
    lane = np.arange(c * m)
    k = np.arange(LANES)[:, None]
    rep = (lane[None, :] // m == k)
    rep_rev = ((c - 1) - lane[None, :] // m == k)
    rep_all = np.concatenate([rep, rep_rev], axis=1).astype(np.float32)
    tile = ((lane[None, :] % m == k) & (k < m)).astype(np.float32)
    return jnp.asarray(rep_all, BF16), jnp.asarray(tile, BF16)


def _s5_disc(lam_re, lam_im, log_dt):
    dt = jnp.exp(log_dt)
    a_re = lam_re * dt
    a_im = lam_im * dt
    mag = jnp.exp(a_re)
    e_re = mag * jnp.cos(a_im) - 1.0
    e_im = mag * jnp.sin(a_im)
    den = lam_re * lam_re + lam_im * lam_im
    return a_re, a_im, (e_re * lam_re + e_im * lam_im) / den, (e_im * lam_re - e_re * lam_im) / den


def _cmul(ar, ai, br, bi):
    return ar * br - ai * bi, ar * bi + ai * br


def _s5_kernel(u_ref, lamc_ref, lamr_ref, lamr2_ref, bc_ref, br_ref, ct_ref, dt_ref, rep_ref, tile_ref,
               y_ref, t_scr, x_scr, w_scr, *, batch, n_ctx, n_all):
    c, m, p = SCAN_CHUNK, S5_GROUP, S5_STATE
    cm = c * m
    u = u_ref[...]
    x_scr[:, :cm] = u.astype(BF16)

    kk = lax.broadcasted_iota(jnp.int32, (1, LANES), 1).astype(F32)
    pw, bb, a1 = [], [], []
    for d in range(2):
        lam = lamc_ref[d]
        a_re, a_im, cf_re, cf_im = _s5_disc(lam[:, 0:1], lam[:, 1:2], lam[:, 2:3])
        mag = jnp.exp(a_re * kk)
        pw += [mag * jnp.cos(a_im * kk), mag * jnp.sin(a_im * kk)]
        b_re, b_im = _cmul(cf_re, cf_im, bc_ref[d, 0], bc_ref[d, 1])
        bb += [b_re, b_im]
        a1.append((jnp.exp(a_re) * jnp.cos(a_im), jnp.exp(a_re) * jnp.sin(a_im)))
    pr = _dot_select(jnp.concatenate(pw, axis=0), rep_ref[...])
    small = jnp.concatenate(bb + [ct_ref[0, 0], ct_ref[0, 1], ct_ref[1, 0], ct_ref[1, 1]], axis=0)
    st = _dot_select(small, tile_ref[...])

    def rows(a, i):
        return a[i * p:(i + 1) * p]

    pf_asc = (rows(pr, 0)[:, :cm], rows(pr, 1)[:, :cm])
    pf_desc = (rows(pr, 0)[:, cm:], rows(pr, 1)[:, cm:])
    pb_asc = (rows(pr, 2)[:, :cm], rows(pr, 3)[:, :cm])
    pb_desc = (rows(pr, 2)[:, cm:], rows(pr, 3)[:, cm:])
    bt_f = (rows(st, 0), rows(st, 1))
    bt_b = (rows(st, 2), rows(st, 3))
    ct_f = (rows(st, 4), rows(st, 5))
    ct_b = (rows(st, 6), rows(st, 7))

    e_f = _cmul(*pf_asc, *ct_f)
    e_b = _cmul(*pb_desc, *ct_b)

    hp = lax.Precision.HIGHEST
    btr = []
    for d in range(2):
        lam = lamr_ref[d]
        _, _, cf_re, cf_im = _s5_disc(lam[0:1, :], lam[1:2, :], lam[2:3, :])
        btr.append(_cmul(cf_re, cf_im, br_ref[d, 0], br_ref[d, 1]))

    def gen(bt, e):
        return (jnp.dot(bt[0], e[0], precision=hp, preferred_element_type=F32)
                - jnp.dot(bt[1], e[1], precision=hp, preferred_element_type=F32))

    w_f = gen(btr[0], e_f)
    w_b = gen(btr[1], e_b)
    zeros = jnp.zeros_like(w_f)
    w_scr[...] = (jnp.concatenate([w_b, zeros], axis=1)
                  + pltpu.roll(jnp.concatenate([zeros, w_f], axis=1), 2 * cm - m, 1))

    def fill(s, carry):
        shift = (2 * cm - (c - 1 - s) * m) % (2 * cm)
        rolled = pltpu.roll(w_scr[...], shift, 1)
        t_scr[pl.ds(pl.multiple_of(s * m, m), m), :] = rolled[:, :cm].astype(BF16)
        return carry

    lax.fori_loop(0, c, fill, 0)

    for d, e in ((0, e_f), (1, e_b)):
        ca_re, ca_im = _cmul(a1[d][0], a1[d][1], e[0], e[1])
        base = cm + 2 * p * d
        t_scr[base:base + p, :] = ca_re.astype(BF16)
        t_scr[base + p:base + 2 * p, :] = (-ca_im).astype(BF16)

    wb = jnp.concatenate(list(_cmul(*pf_desc, *bt_f)) + list(_cmul(*pb_asc, *bt_b)), axis=0)
    hloc = _dot_nt(x_scr[:, :cm], wb.astype(BF16))

    sign = jnp.where(lax.broadcasted_iota(jnp.int32, (1, 2 * p), 1) < p, -1.0, 1.0)
    order_f = list(range(n_all))
    order_b = list(range(n_ctx - 1, -1, -1)) + list(range(n_all - 1, n_ctx - 1, -1))
    for d, order in ((0, order_f), (1, order_b)):
        lam = lamr2_ref[d]
        a_re, a_im, _, _ = _s5_disc(lam[0:1, :], lam[1:2, :], lam[2:3, :])
        mag = jnp.exp(a_re * c)
        ac_r = mag * jnp.cos(a_im * c)
        ac_i = mag * jnp.sin(a_im * c) * sign
        state = jnp.zeros((batch, 2 * p), F32)
        col = cm + 2 * p * d
        for n in order:
            x_scr[n * batch:(n + 1) * batch, col:col + 2 * p] = state.astype(BF16)
            local = hloc[n * batch:(n + 1) * batch, 2 * p * d:2 * p * (d + 1)]
            state = state * ac_r + pltpu.roll(state, p, 1) * ac_i + local

    y_ref[...] = _dot(x_scr[...], t_scr[...]) + dt_ref[...] * u


def _s5_scan(u, lam_re, lam_im, log_dt, b_re, b_im, c_re, c_im, d_skip, batch, n_ctx):
    g, rows, cm = u.shape
    p, m = S5_STATE, S5_GROUP
    n_all = rows // batch
    rep_all, tile = _s5_constants()
    dtb = jnp.broadcast_to(log_dt[:, :, None], lam_re.shape)
    lam3 = jnp.stack([lam_re, lam_im, dtb], axis=-1)
    lamc = jnp.pad(lam3, ((0, 0), (0, 0), (0, 0), (0, LANES - 3))).transpose(1, 0, 2, 3)
    lam3r = jnp.stack([lam_re, lam_im, dtb], axis=2)
    lamr = jnp.pad(lam3r, ((0, 0), (0, 0), (0, 5), (0, 0))).transpose(1, 0, 2, 3)
    lamr2 = jnp.concatenate([lamr, lamr], axis=-1)
    bcol = jnp.stack([b_re, b_im], axis=1)
    bc = jnp.pad(bcol, ((0, 0),) * 4 + ((0, LANES - m),)).transpose(2, 0, 1, 3, 4)
    brow = jnp.swapaxes(bcol, -1, -2).transpose(2, 0, 1, 3, 4)
    ctc = jnp.swapaxes(jnp.stack([c_re, c_im], axis=1), -1, -2)
    ctc = jnp.pad(ctc, ((0, 0),) * 4 + ((0, LANES - m),)).transpose(2, 0, 1, 3, 4)
    dtile = jnp.tile(d_skip.reshape(g, 1, m), (1, 1, SCAN_CHUNK))
    kern = functools.partial(_s5_kernel, batch=batch, n_ctx=n_ctx // SCAN_CHUNK, n_all=n_all)
    per_g = lambda *tail: pl.BlockSpec((None,) + tail, lambda i: (i,) + (0,) * len(tail))
    return pl.pallas_call(
        kern,
        grid=(g,),
        in_specs=[per_g(rows, cm), per_g(2, p, LANES), per_g(2, 8, p), per_g(2, 8, 2 * p),
                  per_g(2, 2, p, LANES), per_g(2, 2, m, p), per_g(2, 2, p, LANES), per_g(1, cm),
                  _resident(rep_all.shape), _resident(tile.shape)],
        out_specs=per_g(rows, cm),
        out_shape=jax.ShapeDtypeStruct((g, rows, cm), F32),
        scratch_shapes=[pltpu.VMEM((cm + 4 * p, cm), BF16), pltpu.VMEM((rows, cm + 4 * p), BF16),
                        pltpu.VMEM((m, 2 * cm), F32)],
        compiler_params=_cparams("arbitrary"),
        name="s5_scan",
    )(u, lamc, lamr, lamr2, bc, brow, ctc, dtile, rep_all, tile)


def _s5_mixer(r, mod, gain, lam_re, lam_im, log_dt, b_re, b_im, c_re, c_im, d_skip, glu_w, glu_b,
              n_ctx, n_ctx_tiles):
    b, t, d = r.shape
    g, m, c = d // S5_GROUP, S5_GROUP, SCAN_CHUNK
    hn = _norm_only(r, mod, gain, n_ctx_tiles)
    u = hn.reshape(b, t // c, c, g, m).transpose(3, 1, 0, 2, 4).reshape(g, (t // c) * b, c * m)
    y = _s5_scan(u, lam_re, lam_im, log_dt, b_re, b_im, c_re, c_im, d_skip, b, n_ctx)
    y = y.reshape(g, t // c, b, c, m).transpose(2, 1, 3, 0, 4).reshape(b, t, d)
    return _glu_res(r, y, mod, glu_w.astype(BF16), glu_b, n_ctx_tiles)


def kernel(x, c, ctx, c_ctx, ada_w, ada_b, norm_g, ffn_w_in, ffn_w_out, mla_w_in, mla_q_norm, mla_kv_norm,
           mla_w_uq, mla_w_ukv, mla_q_gain, mla_k_gain, mla_w_o, ret_w_in, ret_log1m_gamma, ret_gn_gain, ret_w_o,
           s5_lam_re, s5_lam_im, s5_log_dt, s5_b_re, s5_b_im, s5_c_re, s5_c_im, s5_d, s5_glu_w, s5_glu_b):
    b, s, d = x.shape
    n_ctx = ctx.shape[1]
    depth = ada_w.shape[0]
    assert n_ctx % TOKEN_TILE == 0 and s % TOKEN_TILE == 0 and s % GRID_W == 0
    n_ctx_tiles = n_ctx // TOKEN_TILE

    pad = (-(b + 1)) % 8
    cvec = jnp.concatenate([c, c_ctx[None, :], jnp.zeros((pad, d), F32)], axis=0)
    mod_all = _modulation(cvec, ada_w, ada_b)
    mod_lat = mod_all[:, :b].reshape(depth, b, 1, N_MOD, d)
    mod_ctx = jnp.broadcast_to(mod_all[:, b].reshape(depth, 1, 1, N_MOD, d), (depth, b, 1, N_MOD, d))
    mods = jnp.concatenate([mod_ctx, mod_lat], axis=2)

    r = jnp.concatenate([ctx, x], axis=1)
    mla_tabs = _mla_tables(n_ctx, s)
    ret_tabs = _ret_tables(n_ctx, s)

    for i in range(depth):
        kind, j = i % N_MIXERS, i // N_MIXERS
        mod = mods[i]
        r = _ffn(r, mod, norm_g[i, 0], ffn_w_in[i, 0].astype(BF16), ffn_w_out[i, 0].astype(BF16), 0, n_ctx_tiles)
        if kind == 0:
            weights = _mla_weights(mla_w_in[j], mla_q_norm[j], mla_kv_norm[j], mla_w_uq[j], mla_w_ukv[j],
                                   mla_q_gain[j], mla_k_gain[j])
            q, k, v = _mla_proj(r, *mla_tabs, mod, norm_g[i, 1], weights, n_ctx_tiles)
            o = _attention(q, k, v, n_ctx)
            r = _proj_res(r, o, mod, mla_w_o[j].astype(BF16), n_ctx_tiles)
        elif kind == 1:
            q, k, v, sg = _ret_proj(r, *ret_tabs, mod, norm_g[i, 1], ret_w_in[j].astype(BF16), n_ctx_tiles)
            y = _ret_scan(q, k, v, sg, ret_log1m_gamma[j], ret_gn_gain[j], n_ctx)
            r = _proj_res(r, y, mod, ret_w_o[j].astype(BF16), n_ctx_tiles)
        else:
            r = _s5_mixer(r, mod, norm_g[i, 1], s5_lam_re[j], s5_lam_im[j], s5_log_dt[j], s5_b_re[j], s5_b_im[j],
                          s5_c_re[j], s5_c_im[j], s5_d[j], s5_glu_w[j], s5_glu_b[j], n_ctx, n_ctx_tiles)
        r = _ffn(r, mod, norm_g[i, 2], ffn_w_in[i, 1].astype(BF16), ffn_w_out[i, 1].astype(BF16), 2, n_ctx_tiles)
    return r[:, n_ctx:, :]
```

```python
import functools
import math

import jax
import jax.numpy as jnp
import numpy as np
from jax import lax
from jax.experimental import pallas as pl
from jax.experimental.pallas import tpu as pltpu

F32 = jnp.float32
BF16 = jnp.bfloat16

EPS = 1e-6
ROPE_BASE = 10000.0
GRID_W = 64
N_MIXERS = 3
N_MOD = 9

MLA_HEADS = 8
MLA_NOPE = 128
MLA_ROPE = 64
MLA_V = 128
MLA_Q_RANK = 384
MLA_KV_RANK = 256
MLA_HEAD_PAD = 256

RET_HEADS = 4
RET_DK = 256
RET_DV = 512
SCAN_CHUNK = 128

S5_GROUP = 16
S5_STATE = 64

LANES = 128
TOKEN_TILE = 256
VMEM_LIMIT_BYTES = 56 * 1024 * 1024


def _cparams(*sem):
    return pltpu.CompilerParams(dimension_semantics=sem, vmem_limit_bytes=VMEM_LIMIT_BYTES)


def _resident(shape):
    zeros = (0,) * len(shape)
    return pl.BlockSpec(shape, lambda *_: zeros)


def _silu(x):
    return x * jax.nn.sigmoid(x)


def _rms(x, n):
    return x * lax.rsqrt(jnp.sum(x * x, axis=-1, keepdims=True) * (1.0 / n) + EPS)


def _pre_norm(x, gain, mod_ref, k):
    y = _rms(x, x.shape[-1]) * gain
    return y * (1.0 + mod_ref[3 * k + 1:3 * k + 2, :]) + mod_ref[3 * k:3 * k + 1, :]


def _dot(a, b):
    return jnp.dot(a, b, preferred_element_type=F32)


def _dot_nt(a, b):
    return lax.dot_general(a, b, (((1,), (1,)), ((), ())), preferred_element_type=F32)


def _split3(x):
    hi = x.astype(BF16)
    r1 = x - hi.astype(F32)
    mid = r1.astype(BF16)
    lo = (r1 - mid.astype(F32)).astype(BF16)
    return hi, mid, lo


def _dot_select(x, sel):
    hi, mid, lo = _split3(x)
    return _dot(hi, sel) + _dot(mid, sel) + _dot(lo, sel)


def _mod_kernel(cv_ref, w_ref, b_ref, o_ref):
    s = _silu(cv_ref[...]).astype(BF16)
    o_ref[...] = _dot(s, w_ref[...].astype(BF16)) + b_ref[...]


def _modulation(cvec, ada_w, ada_b):
    depth, d, n = ada_w.shape
    rows = cvec.shape[0]
    tn = n // 4
    return pl.pallas_call(
        _mod_kernel,
        grid=(depth, n // tn),
        in_specs=[
            _resident((rows, d)),
            pl.BlockSpec((None, d, tn), lambda i, j: (i, 0, j)),
            pl.BlockSpec((None, 1, tn), lambda i, j: (i, 0, j)),
        ],
        out_specs=pl.BlockSpec((None, rows, tn), lambda i, j: (i, 0, j)),
        out_shape=jax.ShapeDtypeStruct((depth, rows, n), F32),
        compiler_params=_cparams("arbitrary", "arbitrary"),
        name="adaln_modulation",
    )(cvec, ada_w, ada_b.reshape(depth, 1, n))


def _row_call(kernel, name, r_like, row_inputs, mod, consts, outs, n_ctx_tiles):
    b, t, _ = r_like.shape
    tt = TOKEN_TILE
    in_specs = [pl.BlockSpec((None, tt, a.shape[-1]), lambda i, j: (i, j, 0)) for a in row_inputs]
    in_specs.append(pl.BlockSpec((None, None, N_MOD, mod.shape[-1]),
                                 lambda i, j: (i, (j >= n_ctx_tiles).astype(jnp.int32), 0, 0)))
    in_specs += [_resident(a.shape) for a in consts]
    out_specs = [pl.BlockSpec((None, tt, w), lambda i, j: (i, j, 0)) for w, _ in outs]
    out_shape = [jax.ShapeDtypeStruct((b, t, w), dt) for w, dt in outs]
    res = pl.pallas_call(
        kernel,
        grid=(b, t // tt),
        in_specs=in_specs,
        out_specs=out_specs,
        out_shape=out_shape,
        compiler_params=_cparams("arbitrary", "arbitrary"),
        name=name,
    )(*row_inputs, mod, *consts)
    return res


def _ffn_kernel(r_ref, mod_ref, g_ref, win_ref, wout_ref, o_ref, *, k):
    x = r_ref[...]
    xn = _pre_norm(x, g_ref[...], mod_ref, k).astype(BF16)
    f = wout_ref.shape[0]
    a = _dot(xn, win_ref[:, :f])
    b = _dot(xn, win_ref[:, f:])
    hm = (_silu(a) * b).astype(BF16)
    y = _dot(hm, wout_ref[...])
    o_ref[...] = x + (0.5 * mod_ref[3 * k + 2:3 * k + 3, :]) * y


def _ffn(r, mod, gain, w_in, w_out, k, n_ctx_tiles):
    d = r.shape[-1]
    (out,) = _row_call(functools.partial(_ffn_kernel, k=k), "swiglu_half_step", r, [r], mod,
                       [gain.reshape(1, d), w_in, w_out], [(d, F32)], n_ctx_tiles)
    return out


def _proj_res_kernel(r_ref, y_ref, mod_ref, w_ref, o_ref):
    y = _dot(y_ref[...], w_ref[...])
    o_ref[...] = r_ref[...] + mod_ref[5:6, :] * y


def _proj_res(r, y, mod, w, n_ctx_tiles):
    (out,) = _row_call(_proj_res_kernel, "mixer_out_proj", r, [r, y], mod, [w],
                       [(r.shape[-1], F32)], n_ctx_tiles)
    return out


def _gelu_tanh(x):
    cdf = 0.5 * (1.0 + jnp.tanh(math.sqrt(2.0 / math.pi) * (x + 0.044715 * (x * x * x))))
    return x * cdf


def _glu_res_kernel(r_ref, y_ref, mod_ref, w_ref, b_ref, o_ref):
    d = r_ref.shape[-1]
    z = _gelu_tanh(y_ref[...]).astype(BF16)
    ag = _dot(z, w_ref[...]) + b_ref[...]
    o_ref[...] = r_ref[...] + mod_ref[5:6, :] * (ag[:, :d] * jax.nn.sigmoid(ag[:, d:]))


def _glu_res(r, y, mod, w, bias, n_ctx_tiles):
    (out,) = _row_call(_glu_res_kernel, "s5_glu_out", r, [r, y], mod, [w, bias.reshape(1, -1)],
                       [(r.shape[-1], F32)], n_ctx_tiles)
    return out


def _norm_kernel(r_ref, mod_ref, g_ref, o_ref):
    o_ref[...] = _pre_norm(r_ref[...], g_ref[...], mod_ref, 1)


def _norm_only(r, mod, gain, n_ctx_tiles):
    d = r.shape[-1]
    (out,) = _row_call(_norm_kernel, "s5_pre_norm", r, [r], mod, [gain.reshape(1, d)], [(d, F32)],
                       n_ctx_tiles)
    return out


def _rope_tile_perm():
    src = np.full((LANES,), -1, np.int64)
    src[0:16] = np.arange(0, 16)
    src[16:32] = np.arange(32, 48)
    src[64:80] = np.arange(16, 32)
    src[80:96] = np.arange(48, 64)
    return src


def _pad_rope_cols(w):
    src = _rope_tile_perm()
    cols = jnp.take(w, jnp.asarray(np.maximum(src, 0)), axis=-1)
    return jnp.where(jnp.asarray(src >= 0), cols, 0.0)


def _mla_tables(n_ctx, n_lat):
    half = MLA_ROPE // 2
    inv = ROPE_BASE ** (-jnp.arange(0, half, 2, dtype=F32) / half)
    pos = jnp.arange(n_lat)
    ang_r = (pos // GRID_W).astype(F32)[:, None] * inv[None, :]
    ang_c = (pos % GRID_W).astype(F32)[:, None] * inv[None, :]
    z = jnp.zeros((n_lat, 32), F32)
    cos = jnp.concatenate([jnp.cos(ang_r), jnp.cos(ang_c), z, jnp.cos(ang_r), jnp.cos(ang_c), z], axis=-1)
    sin = jnp.concatenate([-jnp.sin(ang_r), -jnp.sin(ang_c), z, jnp.sin(ang_r), jnp.sin(ang_c), z], axis=-1)
    cos = jnp.concatenate([jnp.ones((n_ctx, LANES), F32), cos], axis=0)
    sin = jnp.concatenate([jnp.zeros((n_ctx, LANES), F32), sin], axis=0)
    return cos, sin


def _mla_proj_kernel(r_ref, cos_ref, sin_ref, mod_ref, g_ref, win_ref, qn_ref, kvn_ref, wuq_ref, wukv_ref,
                     qg_ref, kg_ref, q_ref, k_ref, v_ref):
    xn = _pre_norm(r_ref[...], g_ref[...], mod_ref, 1).astype(BF16)
    p = _dot(xn, win_ref[...])
    cos = cos_ref[...]
    sin = sin_ref[...]

    def rotate(x):
        return x * cos + pltpu.roll(x, LANES // 2, 1) * sin

    cq = (_rms(p[:, :MLA_Q_RANK], MLA_Q_RANK) * qn_ref[...]).astype(BF16)
    ckv = (_rms(p[:, MLA_Q_RANK:MLA_Q_RANK + MLA_KV_RANK], MLA_KV_RANK) * kvn_ref[...]).astype(BF16)
    kr = rotate(_rms(p[:, MLA_Q_RANK + MLA_KV_RANK:], MLA_ROPE) * kg_ref[:, LANES:]).astype(BF16)
    q = _dot(cq, wuq_ref[...])
    kv = _dot(ckv, wukv_ref[...])
    scale = (MLA_NOPE + MLA_ROPE) ** -0.5 * math.log2(math.e)
    for h in range(MLA_HEADS):
        a0 = h * MLA_HEAD_PAD
        qa = _rms(q[:, a0:a0 + LANES], MLA_NOPE) * qg_ref[:, :LANES]
        qb = rotate(_rms(q[:, a0 + LANES:a0 + 2 * LANES], MLA_ROPE) * qg_ref[:, LANES:])
        q_ref[:, a0:a0 + LANES] = (qa * scale).astype(BF16)
        q_ref[:, a0 + LANES:a0 + 2 * LANES] = (qb * scale).astype(BF16)
        kn = _rms(kv[:, h * LANES:(h + 1) * LANES], MLA_NOPE) * kg_ref[:, :LANES]
        k_ref[:, a0:a0 + LANES] = kn.astype(BF16)
        k_ref[:, a0 + LANES:a0 + 2 * LANES] = kr
    v_ref[...] = kv[:, MLA_HEADS * LANES:].astype(BF16)


def _mla_weights(w_in, q_norm, kv_norm, w_uq, w_ukv, q_gain, k_gain):
    lat = MLA_Q_RANK + MLA_KV_RANK
    w_in_p = jnp.concatenate([w_in[:, :lat], _pad_rope_cols(w_in[:, lat:])], axis=-1).astype(BF16)
    uq = w_uq.reshape(MLA_Q_RANK, MLA_HEADS, MLA_NOPE + MLA_ROPE)
    uq = jnp.concatenate([uq[..., :MLA_NOPE], _pad_rope_cols(uq[..., MLA_NOPE:])], axis=-1)
    uq = uq.reshape(MLA_Q_RANK, MLA_HEADS * MLA_HEAD_PAD).astype(BF16)
    ukv = w_ukv.reshape(MLA_KV_RANK, MLA_HEADS, MLA_NOPE + MLA_V)
    ukv = jnp.concatenate([ukv[..., :MLA_NOPE].reshape(MLA_KV_RANK, -1),
                           ukv[..., MLA_NOPE:].reshape(MLA_KV_RANK, -1)], axis=-1).astype(BF16)
    qg = jnp.concatenate([q_gain[:MLA_NOPE], _pad_rope_cols(q_gain[MLA_NOPE:])]).reshape(1, -1)
    kg = jnp.concatenate([k_gain[:MLA_NOPE], _pad_rope_cols(k_gain[MLA_NOPE:])]).reshape(1, -1)
    return [w_in_p, q_norm.reshape(1, -1), kv_norm.reshape(1, -1), uq, ukv, qg, kg]


def _mla_proj(r, cos, sin, mod, gain, weights, n_ctx_tiles):
    b, t, d = r.shape
    cos_b = jnp.broadcast_to(cos[None], (1,) + cos.shape)
    sin_b = jnp.broadcast_to(sin[None], (1,) + sin.shape)
    tt = TOKEN_TILE
    hw = MLA_HEADS * MLA_HEAD_PAD
    consts = [gain.reshape(1, d)] + weights
    in_specs = [pl.BlockSpec((None, tt, d), lambda i, j: (i, j, 0)),
                pl.BlockSpec((None, tt, LANES), lambda i, j: (0, j, 0)),
                pl.BlockSpec((None, tt, LANES), lambda i, j: (0, j, 0)),
                pl.BlockSpec((None, None, N_MOD, d),
                             lambda i, j: (i, (j >= n_ctx_tiles).astype(jnp.int32), 0, 0))]
    in_specs += [_resident(a.shape) for a in consts]
    widths = [(hw, BF16), (hw, BF16), (MLA_HEADS * MLA_V, BF16)]
    return pl.pallas_call(
        _mla_proj_kernel,
        grid=(b, t // tt),
        in_specs=in_specs,
        out_specs=[pl.BlockSpec((None, tt, w), lambda i, j: (i, j, 0)) for w, _ in widths],
        out_shape=[jax.ShapeDtypeStruct((b, t, w), dt) for w, dt in widths],
        compiler_params=_cparams("arbitrary", "arbitrary"),
        name="mla_projections",
    )(r, cos_b, sin_b, mod, *consts)


def _attn_kernel(q_ref, k_ref, v_ref, o_ref, *, n_ctx_tiles, n_ctx):
    def attend(n_keys):
        s = _dot_nt(q_ref[...], k_ref[:n_keys, :])
        p = jnp.exp2(s - jnp.max(s, axis=-1, keepdims=True))
        l = jnp.sum(p, axis=-1, keepdims=True)
        o = _dot(p.astype(BF16), v_ref[:n_keys, :])
        o_ref[...] = (o / l).astype(o_ref.dtype)

    is_ctx = pl.program_id(2) < n_ctx_tiles
    pl.when(is_ctx)(lambda: attend(n_ctx))
    pl.when(jnp.logical_not(is_ctx))(lambda: attend(k_ref.shape[0]))


def _attention(q, k, v, n_ctx):
    b, t, _ = q.shape
    tq = TOKEN_TILE
    kern = functools.partial(_attn_kernel, n_ctx_tiles=n_ctx // tq, n_ctx=n_ctx)
    return pl.pallas_call(
        kern,
        grid=(b, MLA_HEADS, t // tq),
        in_specs=[pl.BlockSpec((None, tq, MLA_HEAD_PAD), lambda i, h, j: (i, j, h)),
                  pl.BlockSpec((None, t, MLA_HEAD_PAD), lambda i, h, j: (i, 0, h)),
                  pl.BlockSpec((None, t, MLA_V), lambda i, h, j: (i, 0, h))],
        out_specs=pl.BlockSpec((None, tq, MLA_V), lambda i, h, j: (i, j, h)),
        out_shape=jax.ShapeDtypeStruct((b, t, MLA_HEADS * MLA_V), BF16),
        compiler_params=_cparams("arbitrary", "arbitrary", "arbitrary"),
        name="mla_attention",
    )(q, k, v)


def _ret_tables(n_ctx, n_lat):
    inv = ROPE_BASE ** (-jnp.arange(0, RET_DK, 2, dtype=F32) / RET_DK)
    ang = jnp.arange(n_lat, dtype=F32)[:, None] * inv[None, :]
    cos = jnp.concatenate([jnp.ones((n_ctx, LANES), F32), jnp.cos(ang)], axis=0)
    sin = jnp.concatenate([jnp.zeros((n_ctx, LANES), F32), jnp.sin(ang)], axis=0)
    return cos, sin


def _ret_proj_kernel(r_ref, cos_ref, sin_ref, mod_ref, g_ref, win_ref, q_ref, k_ref, v_ref, sg_ref):
    xn = _pre_norm(r_ref[...], g_ref[...], mod_ref, 1).astype(BF16)
    cos = cos_ref[...]
    sin = sin_ref[...]
    hdk = RET_HEADS * RET_DK
    hdv = RET_HEADS * RET_DV
    scale = RET_DK ** -0.5
    for out_ref, base, sc in ((q_ref, 0, None), (k_ref, hdk, scale)):
        for h in range(RET_HEADS):
            c0 = h * RET_DK
            x1 = _dot(xn, win_ref[:, base + c0:base + c0 + LANES])
            x2 = _dot(xn, win_ref[:, base + c0 + LANES:base + c0 + 2 * LANES])
            o1 = x1 * cos - x2 * sin
            o2 = x2 * cos + x1 * sin
            if sc is not None:
                o1 = o1 * sc
                o2 = o2 * sc
            out_ref[:, c0:c0 + LANES] = o1.astype(BF16)
            out_ref[:, c0 + LANES:c0 + 2 * LANES] = o2.astype(BF16)
    v_ref[...] = _dot(xn, win_ref[:, 2 * hdk:2 * hdk + hdv]).astype(BF16)
    sg_ref[...] = _silu(_dot(xn, win_ref[:, 2 * hdk + hdv:])).astype(BF16)


def _ret_proj(r, cos, sin, mod, gain, w_in, n_ctx_tiles):
    b, t, d = r.shape
    tt = TOKEN_TILE
    consts = [gain.reshape(1, d), w_in]
    in_specs = [pl.BlockSpec((None, tt, d), lambda i, j: (i, j, 0)),
                pl.BlockSpec((tt, LANES), lambda i, j: (j, 0)),
                pl.BlockSpec((tt, LANES), lambda i, j: (j, 0)),
                pl.BlockSpec((None, None, N_MOD, d),
                             lambda i, j: (i, (j >= n_ctx_tiles).astype(jnp.int32), 0, 0))]
    in_specs += [_resident(a.shape) for a in consts]
    widths = [(RET_HEADS * RET_DK, BF16), (RET_HEADS * RET_DK, BF16),
              (RET_HEADS * RET_DV, BF16), (RET_HEADS * RET_DV, BF16)]
    return pl.pallas_call(
        _ret_proj_kernel,
        grid=(b, t // tt),
        in_specs=in_specs,
        out_specs=[pl.BlockSpec((None, tt, w), lambda i, j: (i, j, 0)) for w, _ in widths],
        out_shape=[jax.ShapeDtypeStruct((b, t, w), dt) for w, dt in widths],
        compiler_params=_cparams("arbitrary", "arbitrary"),
        name="retention_projections",
    )(r, cos, sin, mod, *consts)


def _ret_scan_kernel(l1g_ref, q_ref, k_ref, v_ref, sg_ref, gain_ref, y_ref, ob_scr, st_scr, *, n_ctx, n_all):
    c = SCAN_CHUNK
    log_g = jnp.log1p(-jnp.exp(l1g_ref[...]))
    lgf = log_g[0:1, 0:1]
    lgb = log_g[1:2, 0:1]
    ii = lax.broadcasted_iota(jnp.int32, (c, c), 0)
    jj = lax.broadcasted_iota(jnp.int32, (c, c), 1)
    diff = (ii - jj).astype(F32)
    mask = jnp.where(diff >= 0.0, jnp.exp(lgf * jnp.maximum(diff, 0.0)), jnp.exp(lgb * jnp.maximum(-diff, 0.0)))
    ic = lax.broadcasted_iota(jnp.int32, (c, 1), 0).astype(F32)
    qdec_f = jnp.exp(lgf * (ic + 1.0))
    qdec_b = jnp.exp(lgb * (c - ic))
    kdec_f = jnp.exp(lgf * (c - 1.0 - ic))
    kdec_b = jnp.exp(lgb * ic)
    cdec_f = jnp.exp(lgf * c)
    cdec_b = jnp.exp(lgb * c)

    def kv_outer(kc, dec, vc):
        return _dot((kc.astype(F32) * dec).T.astype(BF16), vc)

    st_scr[...] = jnp.zeros_like(st_scr)

    def sweep_b(idx, carry):
        n = jnp.where(idx < n_ctx, n_ctx - 1 - idx, n_all - 1 - (idx - n_ctx))
        off = pl.multiple_of(n * c, c)
        qc = q_ref[pl.ds(off, c), :]
        state = st_scr[...]
        ob_scr[pl.ds(off, c), :] = _dot((qc.astype(F32) * qdec_b).astype(BF16), state.astype(BF16))
        st_scr[...] = state * cdec_b + kv_outer(k_ref[pl.ds(off, c), :], kdec_b, v_ref[pl.ds(off, c), :])
        return carry

    lax.fori_loop(0, n_all, sweep_b, 0)

    st_scr[...] = jnp.zeros_like(st_scr)
    gain = gain_ref[...]

    def sweep_f(n, carry):
        off = pl.multiple_of(n * c, c)
        qc = q_ref[pl.ds(off, c), :]
        kc = k_ref[pl.ds(off, c), :]
        vc = v_ref[pl.ds(off, c), :]
        state = st_scr[...]
        scores = _dot_nt(qc, kc) * mask
        o = (_dot(scores.astype(BF16), vc)
             + _dot((qc.astype(F32) * qdec_f).astype(BF16), state.astype(BF16))
             + ob_scr[pl.ds(off, c), :])
        mu = jnp.mean(o, axis=-1, keepdims=True)
        dev = o - mu
        on = dev * lax.rsqrt(jnp.mean(dev * dev, axis=-1, keepdims=True) + EPS)
        y_ref[pl.ds(off, c), :] = (on * gain * sg_ref[pl.ds(off, c), :].astype(F32)).astype(y_ref.dtype)
        st_scr[...] = state * cdec_f + kv_outer(kc, kdec_f, vc)
        return carry

    lax.fori_loop(0, n_all, sweep_f, 0)


def _ret_scan(q, k, v, sg, log1m_gamma, gn_gain, n_ctx):
    b, t, _ = q.shape
    l1g = jnp.broadcast_to(log1m_gamma.T[:, :, None], (RET_HEADS, 2, LANES))
    l1g = jnp.concatenate([l1g, jnp.zeros((RET_HEADS, 6, LANES), F32) - 1.0], axis=1)
    kern = functools.partial(_ret_scan_kernel, n_ctx=n_ctx // SCAN_CHUNK, n_all=t // SCAN_CHUNK)
    return pl.pallas_call(
        kern,
        grid=(b, RET_HEADS),
        in_specs=[pl.BlockSpec((None, 8, LANES), lambda i, h: (h, 0, 0)),
                  pl.BlockSpec((None, t, RET_DK), lambda i, h: (i, 0, h)),
                  pl.BlockSpec((None, t, RET_DK), lambda i, h: (i, 0, h)),
                  pl.BlockSpec((None, t, RET_DV), lambda i, h: (i, 0, h)),
                  pl.BlockSpec((None, t, RET_DV), lambda i, h: (i, 0, h)),
                  pl.BlockSpec((1, RET_DV), lambda i, h: (0, h))],
        out_specs=pl.BlockSpec((None, t, RET_DV), lambda i, h: (i, 0, h)),
        out_shape=jax.ShapeDtypeStruct((b, t, RET_HEADS * RET_DV), BF16),
        scratch_shapes=[pltpu.VMEM((t, RET_DV), F32), pltpu.VMEM((RET_DK, RET_DV), F32)],
        compiler_params=_cparams("arbitrary", "arbitrary"),
        name="retention_scan",
    )(l1g, q, k, v, sg, gn_gain.reshape(1, -1))


def _s5_constants():
    c, m = SCAN_CHUNK, S5_GROUP
    lane = np.arange(c * m)
    k = np.arange(LANES)[:, None]
    rep = (lane[None, :] // m == k)
    rep_rev = ((c - 1) - lane[None, :] // m == k)
    rep_all = np.concatenate([rep, rep_rev], axis=1).astype(np.float32)
    tile = ((lane[None, :] % m == k) & (k < m)).astype(np.float32)
    return jnp.asarray(rep_all, BF16), jnp.asarray(tile, BF16)


def _s5_disc(lam_re, lam_im, log_dt):
    dt = jnp.exp(log_dt)
    a_re = lam_re * dt
    a_im = lam_im * dt
    mag = jnp.exp(a_re)
    e_re = mag * jnp.cos(a_im) - 1.0
    e_im = mag * jnp.sin(a_im)
    den = lam_re * lam_re + lam_im * lam_im
    return a_re, a_im, (e_re * lam_re + e_im * lam_im) / den, (e_im * lam_re - e_re * lam_im) / den


def _cmul(ar, ai, br, bi):
    return ar * br - ai * bi, ar * bi + ai * br


def _s5_kernel(u_ref, lamc_ref, lamr_ref, lamr2_ref, bc_ref, br_ref, ct_ref, dt_ref, rep_ref, tile_ref,
               y_ref, t_scr, x_scr, w_scr, *, batch, n_ctx, n_all):
    c, m, p = SCAN_CHUNK, S5_GROUP, S5_STATE
    cm = c * m
    u = u_ref[...]
    x_scr[:, :cm] = u.astype(BF16)

    kk = lax.broadcasted_iota(jnp.int32, (1, LANES), 1).astype(F32)
    pw, bb, a1 = [], [], []
    for d in range(2):
        lam = lamc_ref[d]
        a_re, a_im, cf_re, cf_im = _s5_disc(lam[:, 0:1], lam[:, 1:2], lam[:, 2:3])
        mag = jnp.exp(a_re * kk)
        pw += [mag * jnp.cos(a_im * kk), mag * jnp.sin(a_im * kk)]
        b_re, b_im = _cmul(cf_re, cf_im, bc_ref[d, 0], bc_ref[d, 1])
        bb += [b_re, b_im]
        a1.append((jnp.exp(a_re) * jnp.cos(a_im), jnp.exp(a_re) * jnp.sin(a_im)))
    pr = _dot_select(jnp.concatenate(pw, axis=0), rep_ref[...])
    small = jnp.concatenate(bb + [ct_ref[0, 0], ct_ref[0, 1], ct_ref[1, 0], ct_ref[1, 1]], axis=0)
    st = _dot_select(small, tile_ref[...])

    def rows(a, i):
        return a[i * p:(i + 1) * p]

    pf_asc = (rows(pr, 0)[:, :cm], rows(pr, 1)[:, :cm])
    pf_desc = (rows(pr, 0)[:, cm:], rows(pr, 1)[:, cm:])
    pb_asc = (rows(pr, 2)[:, :cm], rows(pr, 3)[:, :cm])
    pb_desc = (rows(pr, 2)[:, cm:], rows(pr, 3)[:, cm:])
    bt_f = (rows(st, 0), rows(st, 1))
    bt_b = (rows(st, 2), rows(st, 3))
    ct_f = (rows(st, 4), rows(st, 5))
    ct_b = (rows(st, 6), rows(st, 7))

    e_f = _cmul(*pf_asc, *ct_f)
    e_b = _cmul(*pb_desc, *ct_b)

    hp = lax.Precision.HIGHEST
    btr = []
    for d in range(2):
        lam = lamr_ref[d]
        _, _, cf_re, cf_im = _s5_disc(lam[0:1, :], lam[1:2, :], lam[2:3, :])
        btr.append(_cmul(cf_re, cf_im, br_ref[d, 0], br_ref[d, 1]))

    def gen(bt, e):
        return (jnp.dot(bt[0], e[0], precision=hp, preferred_element_type=F32)
                - jnp.dot(bt[1], e[1], precision=hp, preferred_element_type=F32))

    w_f = gen(btr[0], e_f)
    w_b = gen(btr[1], e_b)
    zeros = jnp.zeros_like(w_f)
    w_scr[...] = (jnp.concatenate([w_b, zeros], axis=1)
                  + pltpu.roll(jnp.concatenate([zeros, w_f], axis=1), 2 * cm - m, 1))

    def fill(s, carry):
        shift = (2 * cm - (c - 1 - s) * m) % (2 * cm)
        rolled = pltpu.roll(w_scr[...], shift, 1)
        t_scr[pl.ds(pl.multiple_of(s * m, m), m), :] = rolled[:, :cm].astype(BF16)
        return carry

    lax.fori_loop(0, c, fill, 0)

    for d, e in ((0, e_f), (1, e_b)):
        ca_re, ca_im = _cmul(a1[d][0], a1[d][1], e[0], e[1])
        base = cm + 2 * p * d
        t_scr[base:base + p, :] = ca_re.astype(BF16)
        t_scr[base + p:base + 2 * p, :] = (-ca_im).astype(BF16)

    wb = jnp.concatenate(list(_cmul(*pf_desc, *bt_f)) + list(_cmul(*pb_asc, *bt_b)), axis=0)
    hloc = _dot_nt(x_scr[:, :cm], wb.astype(BF16))

    sign = jnp.where(lax.broadcasted_iota(jnp.int32, (1, 2 * p), 1) < p, -1.0, 1.0)
    order_f = list(range(n_all))
    order_b = list(range(n_ctx - 1, -1, -1)) + list(range(n_all - 1, n_ctx - 1, -1))
    for d, order in ((0, order_f), (1, order_b)):
        lam = lamr2_ref[d]
        a_re, a_im, _, _ = _s5_disc(lam[0:1, :], lam[1:2, :], lam[2:3, :])
        mag = jnp.exp(a_re * c)
        ac_r = mag * jnp.cos(a_im * c)
        ac_i = mag * jnp.sin(a_im * c) * sign
        state = jnp.zeros((batch, 2 * p), F32)
        col = cm + 2 * p * d
        for n in order:
            x_scr[n * batch:(n + 1) * batch, col:col + 2 * p] = state.astype(BF16)
            local = hloc[n * batch:(n + 1) * batch, 2 * p * d:2 * p * (d + 1)]
            state = state * ac_r + pltpu.roll(state, p, 1) * ac_i + local

    y_ref[...] = _dot(x_scr[...], t_scr[...]) + dt_ref[...] * u


def _s5_scan(u, lam_re, lam_im, log_dt, b_re, b_im, c_re, c_im, d_skip, batch, n_ctx):
    g, rows, cm = u.shape
    p, m = S5_STATE, S5_GROUP
    n_all = rows // batch
    rep_all, tile = _s5_constants()
    dtb = jnp.broadcast_to(log_dt[:, :, None], lam_re.shape)
    lam3 = jnp.stack([lam_re, lam_im, dtb], axis=-1)
    lamc = jnp.pad(lam3, ((0, 0), (0, 0), (0, 0), (0, LANES - 3))).transpose(1, 0, 2, 3)
    lam3r = jnp.stack([lam_re, lam_im, dtb], axis=2)
    lamr = jnp.pad(lam3r, ((0, 0), (0, 0), (0, 5), (0, 0))).transpose(1, 0, 2, 3)
    lamr2 = jnp.concatenate([lamr, lamr], axis=-1)
    bcol = jnp.stack([b_re, b_im], axis=1)
    bc = jnp.pad(bcol, ((0, 0),) * 4 + ((0, LANES - m),)).transpose(2, 0, 1, 3, 4)
    brow = jnp.swapaxes(bcol, -1, -2).transpose(2, 0, 1, 3, 4)
    ctc = jnp.swapaxes(jnp.stack([c_re, c_im], axis=1), -1, -2)
    ctc = jnp.pad(ctc, ((0, 0),) * 4 + ((0, LANES - m),)).transpose(2, 0, 1, 3, 4)
    dtile = jnp.tile(d_skip.reshape(g, 1, m), (1, 1, SCAN_CHUNK))
    kern = functools.partial(_s5_kernel, batch=batch, n_ctx=n_ctx // SCAN_CHUNK, n_all=n_all)
    per_g = lambda *tail: pl.BlockSpec((None,) + tail, lambda i: (i,) + (0,) * len(tail))
    return pl.pallas_call(
        kern,
        grid=(g,),
        in_specs=[per_g(rows, cm), per_g(2, p, LANES), per_g(2, 8, p), per_g(2, 8, 2 * p),
                  per_g(2, 2, p, LANES), per_g(2, 2, m, p), per_g(2, 2, p, LANES), per_g(1, cm),
                  _resident(rep_all.shape), _resident(tile.shape)],
        out_specs=per_g(rows, cm),
        out_shape=jax.ShapeDtypeStruct((g, rows, cm), F32),
        scratch_shapes=[pltpu.VMEM((cm + 4 * p, cm), BF16), pltpu.VMEM((rows, cm + 4 * p), BF16),
                        pltpu.VMEM((m, 2 * cm), F32)],
        compiler_params=_cparams("arbitrary"),
        name="s5_scan",
    )(u, lamc, lamr, lamr2, bc, brow, ctc, dtile, rep_all, tile)


def _s5_mixer(r, mod, gain, lam_re, lam_im, log_dt, b_re, b_im, c_re, c_im, d_skip, glu_w, glu_b,
              n_ctx, n_ctx_tiles):
    b, t, d = r.shape
    g, m, c = d // S5_GROUP, S5_GROUP, SCAN_CHUNK
    hn = _norm_only(r, mod, gain, n_ctx_tiles)
    u = hn.reshape(b, t // c, c, g, m).transpose(3, 1, 0, 2, 4).reshape(g, (t // c) * b, c * m)
    y = _s5_scan(u, lam_re, lam_im, log_dt, b_re, b_im, c_re, c_im, d_skip, b, n_ctx)
    y = y.reshape(g, t // c, b, c, m).transpose(2, 1, 3, 0, 4).reshape(b, t, d)
    return _glu_res(r, y, mod, glu_w.astype(BF16), glu_b, n_ctx_tiles)


def kernel(x, c, ctx, c_ctx, ada_w, ada_b, norm_g, ffn_w_in, ffn_w_out, mla_w_in, mla_q_norm, mla_kv_norm,
           mla_w_uq, mla_w_ukv, mla_q_gain, mla_k_gain, mla_w_o, ret_w_in, ret_log1m_gamma, ret_gn_gain, ret_w_o,
           s5_lam_re, s5_lam_im, s5_log_dt, s5_b_re, s5_b_im, s5_c_re, s5_c_im, s5_d, s5_glu_w, s5_glu_b):
    b, s, d = x.shape
    n_ctx = ctx.shape[1]
    depth = ada_w.shape[0]
    assert n_ctx % TOKEN_TILE == 0 and s % TOKEN_TILE == 0 and s % GRID_W == 0
    n_ctx_tiles = n_ctx // TOKEN_TILE

    pad = (-(b + 1)) % 8
    cvec = jnp.concatenate([c, c_ctx[None, :], jnp.zeros((pad, d), F32)], axis=0)
    mod_all = _modulation(cvec, ada_w, ada_b)
    mod_lat = mod_all[:, :b].reshape(depth, b, 1, N_MOD, d)
    mod_ctx = jnp.broadcast_to(mod_all[:, b].reshape(depth, 1, 1, N_MOD, d), (depth, b, 1, N_MOD, d))
    mods = jnp.concatenate([mod_ctx, mod_lat], axis=2)

    r = jnp.concatenate([ctx, x], axis=1)
    mla_tabs = _mla_tables(n_ctx, s)
    ret_tabs = _ret_tables(n_ctx, s)

    for i in range(depth):
        kind, j = i % N_MIXERS, i // N_MIXERS
        mod = mods[i]
        r = _ffn(r, mod, norm_g[i, 0], ffn_w_in[i, 0].astype(BF16), ffn_w_out[i, 0].astype(BF16), 0, n_ctx_tiles)
        if kind == 0:
            weights = _mla_weights(mla_w_in[j], mla_q_norm[j], mla_kv_norm[j], mla_w_uq[j], mla_w_ukv[j],
                                   mla_q_gain[j], mla_k_gain[j])
            q, k, v = _mla_proj(r, *mla_tabs, mod, norm_g[i, 1], weights, n_ctx_tiles)
            o = _attention(q, k, v, n_ctx)
            r = _proj_res(r, o, mod, mla_w_o[j].astype(BF16), n_ctx_tiles)
        elif kind == 1:
            q, k, v, sg = _ret_proj(r, *ret_tabs, mod, norm_g[i, 1], ret_w_in[j].astype(BF16), n_ctx_tiles)
            y = _ret_scan(q, k, v, sg, ret_log1m_gamma[j], ret_gn_gain[j], n_ctx)
            r = _proj_res(r, y, mod, ret_w_o[j].astype(BF16), n_ctx_tiles)
        else:
            r = _s5_mixer(r, mod, norm_g[i, 1], s5_lam_re[j], s5_lam_im[j], s5_log_dt[j], s5_b_re[j], s5_b_im[j],
                          s5_c_re[j], s5_c_im[j], s5_d[j], s5_glu_w[j], s5_glu_b[j], n_ctx, n_ctx_tiles)
        r = _ffn(r, mod, norm_g[i, 2], ffn_w_in[i, 1].astype(BF16), ffn_w_out[i, 1].astype(BF16), 2, n_ctx_tiles)
    return r[:, n_ctx:, :]
```

```python
import functools
import math

import jax
import jax.numpy as jnp
import numpy as np
from jax import lax
from jax.experimental import pallas as pl
from jax.experimental.pallas import tpu as pltpu

F32 = jnp.float32
BF16 = jnp.bfloat16

EPS = 1e-6
ROPE_BASE = 10000.0
GRID_W = 64
N_MIXERS = 3
N_MOD = 9

MLA_HEADS = 8
MLA_NOPE = 128
MLA_ROPE = 64
MLA_V = 128
MLA_Q_RANK = 384
MLA_KV_RANK = 256
MLA_HEAD_PAD = 256

RET_HEADS = 4
RET_DK = 256
RET_DV = 512
SCAN_CHUNK = 128

S5_GROUP = 16
S5_STATE = 64

LANES = 128
TOKEN_TILE = 256
ATTN_HEADS_PER_STEP = 4
VMEM_LIMIT_BYTES = 56 * 1024 * 1024


def _cparams(*sem):
    return pltpu.CompilerParams(dimension_semantics=sem, vmem_limit_bytes=VMEM_LIMIT_BYTES)


def _resident(shape):
    zeros = (0,) * len(shape)
    return pl.BlockSpec(shape, lambda *_: zeros)


def _silu(x):
    return x * jax.nn.sigmoid(x)


def _rms(x, n):
    return x * lax.rsqrt(jnp.sum(x * x, axis=-1, keepdims=True) * (1.0 / n) + EPS)


def _pre_norm(x, gain, mod_ref, k):
    y = _rms(x, x.shape[-1]) * gain
    return y * (1.0 + mod_ref[3 * k + 1:3 * k + 2, :]) + mod_ref[3 * k:3 * k + 1, :]


def _dot(a, b):
    return jnp.dot(a, b, preferred_element_type=F32)


def _dot_nt(a, b):
    return lax.dot_general(a, b, (((1,), (1,)), ((), ())), preferred_element_type=F32)


def _mod_kernel(cv_ref, w_ref, b_ref, o_ref):
    s = _silu(cv_ref[...]).astype(BF16)
    o_ref[...] = _dot(s, w_ref[...].astype(BF16)) + b_ref[...]


def _modulation(cvec, ada_w, ada_b):
    depth, d, n = ada_w.shape
    rows = cvec.shape[0]
    tn = n // 4
    return pl.pallas_call(
        _mod_kernel,
        grid=(depth, n // tn),
        in_specs=[
            _resident((rows, d)),
            pl.BlockSpec((None, d, tn), lambda i, j: (i, 0, j)),
            pl.BlockSpec((None, 1, tn), lambda i, j: (i, 0, j)),
        ],
        out_specs=pl.BlockSpec((None, rows, tn), lambda i, j: (i, 0, j)),
        out_shape=jax.ShapeDtypeStruct((depth, rows, n), F32),
        compiler_params=_cparams("arbitrary", "arbitrary"),
        name="adaln_modulation",
    )(cvec, ada_w, ada_b.reshape(depth, 1, n))


def _row_call(kernel, name, r_like, row_inputs, mod, consts, outs, n_ctx_tiles):
    b, t, _ = r_like.shape
    tt = TOKEN_TILE
    in_specs = [pl.BlockSpec((None, tt, a.shape[-1]), lambda i, j: (i, j, 0)) for a in row_inputs]
    in_specs.append(pl.BlockSpec((None, None, N_MOD, mod.shape[-1]),
                                 lambda i, j: (i, (j >= n_ctx_tiles).astype(jnp.int32), 0, 0)))
    in_specs += [_resident(a.shape) for a in consts]
    out_specs = [pl.BlockSpec((None, tt, w), lambda i, j: (i, j, 0)) for w, _ in outs]
    out_shape = [jax.ShapeDtypeStruct((b, t, w), dt) for w, dt in outs]
    res = pl.pallas_call(
        kernel,
        grid=(b, t // tt),
        in_specs=in_specs,
        out_specs=out_specs,
        out_shape=out_shape,
        compiler_params=_cparams("arbitrary", "arbitrary"),
        name=name,
    )(*row_inputs, mod, *consts)
    return res


def _ffn_kernel(r_ref, mod_ref, g_ref, win_ref, wout_ref, o_ref, *, k):
    x = r_ref[...]
    xn = _pre_norm(x, g_ref[...], mod_ref, k).astype(BF16)
    f = wout_ref.shape[0]
    a = _dot(xn, win_ref[:, :f])
    b = _dot(xn, win_ref[:, f:])
    hm = (_silu(a) * b).astype(BF16)
    y = _dot(hm, wout_ref[...])
    o_ref[...] = x + (0.5 * mod_ref[3 * k + 2:3 * k + 3, :]) * y


def _ffn(r, mod, gain, w_in, w_out, k, n_ctx_tiles):
    d = r.shape[-1]
    (out,) = _row_call(functools.partial(_ffn_kernel, k=k), "swiglu_half_step", r, [r], mod,
                       [gain.reshape(1, d), w_in, w_out], [(d, F32)], n_ctx_tiles)
    return out


def _proj_res_kernel(r_ref, y_ref, mod_ref, w_ref, o_ref):
    y = _dot(y_ref[...], w_ref[...])
    o_ref[...] = r_ref[...] + mod_ref[5:6, :] * y


def _proj_res(r, y, mod, w, n_ctx_tiles):
    (out,) = _row_call(_proj_res_kernel, "mixer_out_proj", r, [r, y], mod, [w],
                       [(r.shape[-1], F32)], n_ctx_tiles)
    return out


def _gelu_tanh(x):
    cdf = 0.5 * (1.0 + jnp.tanh(math.sqrt(2.0 / math.pi) * (x + 0.044715 * (x * x * x))))
    return x * cdf


def _rope_tile_perm():
    src = np.full((LANES,), -1, np.int64)
    src[0:16] = np.arange(0, 16)
    src[16:32] = np.arange(32, 48)
    src[64:80] = np.arange(16, 32)
    src[80:96] = np.arange(48, 64)
    return src


def _pad_rope_cols(w):
    src = _rope_tile_perm()
    cols = jnp.take(w, jnp.asarray(np.maximum(src, 0)), axis=-1)
    return jnp.where(jnp.asarray(src >= 0), cols, 0.0)


def _mla_tables(n_ctx, n_lat):
    half = MLA_ROPE // 2
    inv = ROPE_BASE ** (-jnp.arange(0, half, 2, dtype=F32) / half)
    pos = jnp.arange(n_lat)
    ang_r = (pos // GRID_W).astype(F32)[:, None] * inv[None, :]
    ang_c = (pos % GRID_W).astype(F32)[:, None] * inv[None, :]
    z = jnp.zeros((n_lat, 32), F32)
    cos = jnp.concatenate([jnp.cos(ang_r), jnp.cos(ang_c), z, jnp.cos(ang_r), jnp.cos(ang_c), z], axis=-1)
    sin = jnp.concatenate([-jnp.sin(ang_r), -jnp.sin(ang_c), z, jnp.sin(ang_r), jnp.sin(ang_c), z], axis=-1)
    cos = jnp.concatenate([jnp.ones((n_ctx, LANES), F32), cos], axis=0)
    sin = jnp.concatenate([jnp.zeros((n_ctx, LANES), F32), sin], axis=0)
    return cos, sin


def _mla_proj_kernel(r_ref, cos_ref, sin_ref, mod_ref, g_ref, win_ref, qn_ref, kvn_ref, wuq_ref, wukv_ref,
                     qg_ref, kg_ref, q_ref, k_ref, v_ref):
    xn = _pre_norm(r_ref[...], g_ref[...], mod_ref, 1).astype(BF16)
    p = _dot(xn, win_ref[...])
    cos = cos_ref[...]
    sin = sin_ref[...]

    def rotate(x):
        return x * cos + pltpu.roll(x, LANES // 2, 1) * sin

    cq = (_rms(p[:, :MLA_Q_RANK], MLA_Q_RANK) * qn_ref[...]).astype(BF16)
    ckv = (_rms(p[:, MLA_Q_RANK:MLA_Q_RANK + MLA_KV_RANK], MLA_KV_RANK) * kvn_ref[...]).astype(BF16)
    kr = rotate(_rms(p[:, MLA_Q_RANK + MLA_KV_RANK:], MLA_ROPE) * kg_ref[:, LANES:]).astype(BF16)
    q = _dot(cq, wuq_ref[...])
    kv = _dot(ckv, wukv_ref[...])
    scale = (MLA_NOPE + MLA_ROPE) ** -0.5 * math.log2(math.e)
    for h in range(MLA_HEADS):
        a0 = h * MLA_HEAD_PAD
        qa = _rms(q[:, a0:a0 + LANES], MLA_NOPE) * qg_ref[:, :LANES]
        qb = rotate(_rms(q[:, a0 + LANES:a0 + 2 * LANES], MLA_ROPE) * qg_ref[:, LANES:])
        q_ref[:, a0:a0 + LANES] = (qa * scale).astype(BF16)
        q_ref[:, a0 + LANES:a0 + 2 * LANES] = (qb * scale).astype(BF16)
        kn = _rms(kv[:, h * LANES:(h + 1) * LANES], MLA_NOPE) * kg_ref[:, :LANES]
        k_ref[:, a0:a0 + LANES] = kn.astype(BF16)
        k_ref[:, a0 + LANES:a0 + 2 * LANES] = kr
    v_ref[...] = kv[:, MLA_HEADS * LANES:].astype(BF16)


def _mla_weights(w_in, q_norm, kv_norm, w_uq, w_ukv, q_gain, k_gain):
    lat = MLA_Q_RANK + MLA_KV_RANK
    w_in_p = jnp.concatenate([w_in[:, :lat], _pad_rope_cols(w_in[:, lat:])], axis=-1).astype(BF16)
    uq = w_uq.reshape(MLA_Q_RANK, MLA_HEADS, MLA_NOPE + MLA_ROPE)
    uq = jnp.concatenate([uq[..., :MLA_NOPE], _pad_rope_cols(uq[..., MLA_NOPE:])], axis=-1)
    uq = uq.reshape(MLA_Q_RANK, MLA_HEADS * MLA_HEAD_PAD).astype(BF16)
    ukv = w_ukv.reshape(MLA_KV_RANK, MLA_HEADS, MLA_NOPE + MLA_V)
    ukv = jnp.concatenate([ukv[..., :MLA_NOPE].reshape(MLA_KV_RANK, -1),
                           ukv[..., MLA_NOPE:].reshape(MLA_KV_RANK, -1)], axis=-1).astype(BF16)
    qg = jnp.concatenate([q_gain[:MLA_NOPE], _pad_rope_cols(q_gain[MLA_NOPE:])]).reshape(1, -1)
    kg = jnp.concatenate([k_gain[:MLA_NOPE], _pad_rope_cols(k_gain[MLA_NOPE:])]).reshape(1, -1)
    return [w_in_p, q_norm.reshape(1, -1), kv_norm.reshape(1, -1), uq, ukv, qg, kg]


def _mla_proj(r, cos, sin, mod, gain, weights, n_ctx_tiles):
    b, t, d = r.shape
    cos_b = jnp.broadcast_to(cos[None], (1,) + cos.shape)
    sin_b = jnp.broadcast_to(sin[None], (1,) + sin.shape)
    tt = TOKEN_TILE
    hw = MLA_HEADS * MLA_HEAD_PAD
    consts = [gain.reshape(1, d)] + weights
    in_specs = [pl.BlockSpec((None, tt, d), lambda i, j: (i, j, 0)),
                pl.BlockSpec((None, tt, LANES), lambda i, j: (0, j, 0)),
                pl.BlockSpec((None, tt, LANES), lambda i, j: (0, j, 0)),
                pl.BlockSpec((None, None, N_MOD, d),
                             lambda i, j: (i, (j >= n_ctx_tiles).astype(jnp.int32), 0, 0))]
    in_specs += [_resident(a.shape) for a in consts]
    widths = [(hw, BF16), (hw, BF16), (MLA_HEADS * MLA_V, BF16)]
    return pl.pallas_call(
        _mla_proj_kernel,
        grid=(b, t // tt),
        in_specs=in_specs,
        out_specs=[pl.BlockSpec((None, tt, w), lambda i, j: (i, j, 0)) for w, _ in widths],
        out_shape=[jax.ShapeDtypeStruct((b, t, w), dt) for w, dt in widths],
        compiler_params=_cparams("arbitrary", "arbitrary"),
        name="mla_projections",
    )(r, cos_b, sin_b, mod, *consts)


def _attn_kernel(q_ref, k_ref, v_ref, o_ref, *, n_ctx_tiles, n_ctx):
    def attend(n_keys):
        for h in range(ATTN_HEADS_PER_STEP):
            qk = slice(h * MLA_HEAD_PAD, (h + 1) * MLA_HEAD_PAD)
            vo = slice(h * MLA_V, (h + 1) * MLA_V)
            s = _dot_nt(q_ref[:, qk], k_ref[:n_keys, qk])
            p = jnp.exp2(s - jnp.max(s, axis=-1, keepdims=True))
            l = jnp.sum(p, axis=-1, keepdims=True)
            o = _dot(p.astype(BF16), v_ref[:n_keys, vo])
            o_ref[:, vo] = (o / l).astype(o_ref.dtype)

    is_ctx = pl.program_id(2) < n_ctx_tiles
    pl.when(is_ctx)(lambda: attend(n_ctx))
    pl.when(jnp.logical_not(is_ctx))(lambda: attend(k_ref.shape[0]))


def _attention(q, k, v, n_ctx):
    b, t, _ = q.shape
    tq = TOKEN_TILE
    hp = ATTN_HEADS_PER_STEP
    kern = functools.partial(_attn_kernel, n_ctx_tiles=n_ctx // tq, n_ctx=n_ctx)
    return pl.pallas_call(
        kern,
        grid=(b, MLA_HEADS // hp, t // tq),
        in_specs=[pl.BlockSpec((None, tq, hp * MLA_HEAD_PAD), lambda i, h, j: (i, j, h)),
                  pl.BlockSpec((None, t, hp * MLA_HEAD_PAD), lambda i, h, j: (i, 0, h)),
                  pl.BlockSpec((None, t, hp * MLA_V), lambda i, h, j: (i, 0, h))],
        out_specs=pl.BlockSpec((None, tq, hp * MLA_V), lambda i, h, j: (i, j, h)),
        out_shape=jax.ShapeDtypeStruct((b, t, MLA_HEADS * MLA_V), BF16),
        compiler_params=_cparams("arbitrary", "arbitrary", "arbitrary"),
        name="mla_attention",
    )(q, k, v)


def _ret_tables(n_ctx, n_lat):
    inv = ROPE_BASE ** (-jnp.arange(0, RET_DK, 2, dtype=F32) / RET_DK)
    ang = jnp.arange(n_lat, dtype=F32)[:, None] * inv[None, :]
    cos = jnp.concatenate([jnp.ones((n_ctx, LANES), F32), jnp.cos(ang)], axis=0)
    sin = jnp.concatenate([jnp.zeros((n_ctx, LANES), F32), jnp.sin(ang)], axis=0)
    return cos, sin


def _ret_proj_kernel(r_ref, cos_ref, sin_ref, mod_ref, g_ref, win_ref, q_ref, k_ref, v_ref, sg_ref):
    xn = _pre_norm(r_ref[...], g_ref[...], mod_ref, 1).astype(BF16)
    cos = cos_ref[...]
    sin = sin_ref[...]
    hdk = RET_HEADS * RET_DK
    hdv = RET_HEADS * RET_DV
    scale = RET_DK ** -0.5
    for out_ref, base, sc in ((q_ref, 0, None), (k_ref, hdk, scale)):
        for h in range(RET_HEADS):
            c0 = h * RET_DK
            x1 = _dot(xn, win_ref[:, base + c0:base + c0 + LANES])
            x2 = _dot(xn, win_ref[:, base + c0 + LANES:base + c0 + 2 * LANES])
            o1 = x1 * cos - x2 * sin
            o2 = x2 * cos + x1 * sin
            if sc is not None:
                o1 = o1 * sc
                o2 = o2 * sc
            out_ref[:, c0:c0 + LANES] = o1.astype(BF16)
            out_ref[:, c0 + LANES:c0 + 2 * LANES] = o2.astype(BF16)
    v_ref[...] = _dot(xn, win_ref[:, 2 * hdk:2 * hdk + hdv]).astype(BF16)
    sg_ref[...] = _silu(_dot(xn, win_ref[:, 2 * hdk + hdv:])).astype(BF16)


def _ret_proj(r, cos, sin, mod, gain, w_in, n_ctx_tiles):
    b, t, d = r.shape
    tt = TOKEN_TILE
    consts = [gain.reshape(1, d), w_in]
    in_specs = [pl.BlockSpec((None, tt, d), lambda i, j: (i, j, 0)),
                pl.BlockSpec((tt, LANES), lambda i, j: (j, 0)),
                pl.BlockSpec((tt, LANES), lambda i, j: (j, 0)),
                pl.BlockSpec((None, None, N_MOD, d),
                             lambda i, j: (i, (j >= n_ctx_tiles).astype(jnp.int32), 0, 0))]
    in_specs += [_resident(a.shape) for a in consts]
    widths = [(RET_HEADS * RET_DK, BF16), (RET_HEADS * RET_DK, BF16),
              (RET_HEADS * RET_DV, BF16), (RET_HEADS * RET_DV, BF16)]
    return pl.pallas_call(
        _ret_proj_kernel,
        grid=(b, t // tt),
        in_specs=in_specs,
        out_specs=[pl.BlockSpec((None, tt, w), lambda i, j: (i, j, 0)) for w, _ in widths],
        out_shape=[jax.ShapeDtypeStruct((b, t, w), dt) for w, dt in widths],
        compiler_params=_cparams("arbitrary", "arbitrary"),
        name="retention_projections",
    )(r, cos, sin, mod, *consts)


def _ret_scan_kernel(l1g_ref, q_ref, k_ref, v_ref, sg_ref, gain_ref, y_ref, ob_scr, st_scr, *, n_ctx, n_all):
    c = SCAN_CHUNK
    log_g = jnp.log1p(-jnp.exp(l1g_ref[...]))
    lgf = log_g[0:1, 0:1]
    lgb = log_g[1:2, 0:1]
    ii = lax.broadcasted_iota(jnp.int32, (c, c), 0)
    jj = lax.broadcasted_iota(jnp.int32, (c, c), 1)
    diff = (ii - jj).astype(F32)
    mask = jnp.where(diff >= 0.0, jnp.exp(lgf * jnp.maximum(diff, 0.0)), jnp.exp(lgb * jnp.maximum(-diff, 0.0)))
    ic = lax.broadcasted_iota(jnp.int32, (c, 1), 0).astype(F32)
    qdec_f = jnp.exp(lgf * (ic + 1.0))
    qdec_b = jnp.exp(lgb * (c - ic))
    kdec_f = jnp.exp(lgf * (c - 1.0 - ic))
    kdec_b = jnp.exp(lgb * ic)
    cdec_f = jnp.exp(lgf * c)
    cdec_b = jnp.exp(lgb * c)

    def kv_outer(kc, dec, vc):
        return _dot((kc.astype(F32) * dec).T.astype(BF16), vc)

    st_scr[...] = jnp.zeros_like(st_scr)

    def sweep_b(idx, carry):
        n = jnp.where(idx < n_ctx, n_ctx - 1 - idx, n_all - 1 - (idx - n_ctx))
        off = pl.multiple_of(n * c, c)
        qc = q_ref[pl.ds(off, c), :]
        state = st_scr[...]
        ob_scr[pl.ds(off, c), :] = _dot((qc.astype(F32) * qdec_b).astype(BF16), state.astype(BF16))
        st_scr[...] = state * cdec_b + kv_outer(k_ref[pl.ds(off, c), :], kdec_b, v_ref[pl.ds(off, c), :])
        return carry

    lax.fori_loop(0, n_all, sweep_b, 0)

    st_scr[...] = jnp.zeros_like(st_scr)
    gain = gain_ref[...]

    def sweep_f(n, carry):
        off = pl.multiple_of(n * c, c)
        qc = q_ref[pl.ds(off, c), :]
        kc = k_ref[pl.ds(off, c), :]
        vc = v_ref[pl.ds(off, c), :]
        state = st_scr[...]
        scores = _dot_nt(qc, kc) * mask
        o = (_dot(scores.astype(BF16), vc)
             + _dot((qc.astype(F32) * qdec_f).astype(BF16), state.astype(BF16))
             + ob_scr[pl.ds(off, c), :])
        mu = jnp.mean(o, axis=-1, keepdims=True)
        dev = o - mu
        on = dev * lax.rsqrt(jnp.mean(dev * dev, axis=-1, keepdims=True) + EPS)
        y_ref[pl.ds(off, c), :] = (on * gain * sg_ref[pl.ds(off, c), :].astype(F32)).astype(y_ref.dtype)
        st_scr[...] = state * cdec_f + kv_outer(kc, kdec_f, vc)
        return carry

    lax.fori_loop(0, n_all, sweep_f, 0)


def _ret_scan(q, k, v, sg, log1m_gamma, gn_gain, n_ctx):
    b, t, _ = q.shape
    l1g = jnp.broadcast_to(log1m_gamma.T[:, :, None], (RET_HEADS, 2, LANES))
    l1g = jnp.concatenate([l1g, jnp.zeros((RET_HEADS, 6, LANES), F32) - 1.0], axis=1)
    kern = functools.partial(_ret_scan_kernel, n_ctx=n_ctx // SCAN_CHUNK, n_all=t // SCAN_CHUNK)
    return pl.pallas_call(
        kern,
        grid=(b, RET_HEADS),
        in_specs=[pl.BlockSpec((None, 8, LANES), lambda i, h: (h, 0, 0)),
                  pl.BlockSpec((None, t, RET_DK), lambda i, h: (i, 0, h)),
                  pl.BlockSpec((None, t, RET_DK), lambda i, h: (i, 0, h)),
                  pl.BlockSpec((None, t, RET_DV), lambda i, h: (i, 0, h)),
                  pl.BlockSpec((None, t, RET_DV), lambda i, h: (i, 0, h)),
                  pl.BlockSpec((1, RET_DV), lambda i, h: (0, h))],
        out_specs=pl.BlockSpec((None, t, RET_DV), lambda i, h: (i, 0, h)),
        out_shape=jax.ShapeDtypeStruct((b, t, RET_HEADS * RET_DV), BF16),
        scratch_shapes=[pltpu.VMEM((t, RET_DV), F32), pltpu.VMEM((RET_DK, RET_DV), F32)],
        compiler_params=_cparams("arbitrary", "arbitrary"),
        name="retention_scan",
    )(l1g, q, k, v, sg, gn_gain.reshape(1, -1))


CHUNKS_PER_TILE = TOKEN_TILE // SCAN_CHUNK


def _row_specs(d, n_ctx_tiles):
    return (pl.BlockSpec((None, TOKEN_TILE, d), lambda i, j: (i, j, 0)),
            pl.BlockSpec((None, None, N_MOD, d), lambda i, j: (i, (j >= n_ctx_tiles).astype(jnp.int32), 0, 0)))


def _chunk_major_spec(g):
    return pl.BlockSpec((g, CHUNKS_PER_TILE, None, S5_GROUP, SCAN_CHUNK), lambda i, j: (0, j, i, 0, 0))


def _norm_t_kernel(r_ref, mod_ref, g_ref, o_ref):
    xn = _pre_norm(r_ref[...], g_ref[...], mod_ref, 1)
    c = SCAN_CHUNK
    for ci in range(CHUNKS_PER_TILE):
        o_ref[:, ci] = xn[ci * c:(ci + 1) * c, :].T.reshape(o_ref.shape[0], S5_GROUP, c)


def _norm_chunk_major(r, mod, gain, n_ctx_tiles):
    b, t, d = r.shape
    g = d // S5_GROUP
    row_spec, mod_spec = _row_specs(d, n_ctx_tiles)
    return pl.pallas_call(
        _norm_t_kernel,
        grid=(b, t // TOKEN_TILE),
        in_specs=[row_spec, mod_spec, _resident((1, d))],
        out_specs=_chunk_major_spec(g),
        out_shape=jax.ShapeDtypeStruct((g, t // SCAN_CHUNK, b, S5_GROUP, SCAN_CHUNK), F32),
        compiler_params=_cparams("arbitrary", "arbitrary"),
        name="s5_pre_norm",
    )(r, mod, gain.reshape(1, d))


def _glu_res_kernel(r_ref, y_ref, mod_ref, w_ref, b_ref, o_ref):
    d = r_ref.shape[-1]
    c = SCAN_CHUNK
    y = jnp.concatenate([y_ref[:, ci].reshape(d, c).T for ci in range(CHUNKS_PER_TILE)], axis=0)
    z = _gelu_tanh(y).astype(BF16)
    ag = _dot(z, w_ref[...]) + b_ref[...]
    o_ref[...] = r_ref[...] + mod_ref[5:6, :] * (ag[:, :d] * jax.nn.sigmoid(ag[:, d:]))


def _glu_res(r, y, mod, w, bias, n_ctx_tiles):
    b, t, d = r.shape
    row_spec, mod_spec = _row_specs(d, n_ctx_tiles)
    return pl.pallas_call(
        _glu_res_kernel,
        grid=(b, t // TOKEN_TILE),
        in_specs=[row_spec, _chunk_major_spec(d // S5_GROUP), mod_spec, _resident(w.shape), _resident((1, 2 * d))],
        out_specs=row_spec,
        out_shape=jax.ShapeDtypeStruct((b, t, d), F32),
        compiler_params=_cparams("arbitrary", "arbitrary"),
        name="s5_glu_out",
    )(r, y, mod, w, bias.reshape(1, -1))


def _s5_disc(lam_re, lam_im, log_dt):
    dt = jnp.exp(log_dt)
    a_re = lam_re * dt
    a_im = lam_im * dt
    mag = jnp.exp(a_re)
    e_re = mag * jnp.cos(a_im) - 1.0
    e_im = mag * jnp.sin(a_im)
    den = lam_re * lam_re + lam_im * lam_im
    return a_re, a_im, (e_re * lam_re + e_im * lam_im) / den, (e_im * lam_re - e_re * lam_im) / den


def _cmul(ar, ai, br, bi):
    return ar * br - ai * bi, ar * bi + ai * br


def _s5_kernel(u_ref, lamc_ref, lamr_ref, lamr2_ref, bc_ref, br_ref, ct_ref, d_ref,
               y_ref, t_scr, x_scr, w_scr, *, batch, n_ctx, n_all):
    c, m, p = SCAN_CHUNK, S5_GROUP, S5_STATE
    cm = c * m
    rows = n_all * batch
    for mi in range(m):
        x_scr[:, mi * c:(mi + 1) * c] = u_ref[pl.ds(mi, rows, stride=m), :].astype(BF16)

    kk = lax.broadcasted_iota(jnp.int32, (1, LANES), 1).astype(F32)

    def powers(a_re, a_im, expo):
        mag = jnp.exp(a_re * expo)
        return mag * jnp.cos(a_im * expo), mag * jnp.sin(a_im * expo)

    def lane_tiles(fn):
        parts = [fn(i) for i in range(m)]
        return (jnp.concatenate([q[0] for q in parts], axis=1), jnp.concatenate([q[1] for q in parts], axis=1))

    disc = [_s5_disc(lamc_ref[d][:, 0:1], lamc_ref[d][:, 1:2], lamc_ref[d][:, 2:3]) for d in range(2)]
    bbar = [_cmul(disc[d][2], disc[d][3], bc_ref[d, 0], bc_ref[d, 1]) for d in range(2)]
    ctr = [(ct_ref[d, 0], ct_ref[d, 1]) for d in range(2)]

    def col(pair, i):
        return pair[0][:, i:i + 1], pair[1][:, i:i + 1]

    f_up = powers(disc[0][0], disc[0][1], kk + 1.0)
    f_down = powers(disc[0][0], disc[0][1], (c - 1.0) - kk)
    b_up = powers(disc[1][0], disc[1][1], kk)
    b_down = powers(disc[1][0], disc[1][1], (c - 1.0) - kk)
    b_down1 = powers(disc[1][0], disc[1][1], c - kk)
    lag0 = jnp.where(kk == c - 1.0, 1.0, 0.0)

    q0f = lane_tiles(lambda mo: (col(ctr[0], mo)[0] * lag0, col(ctr[0], mo)[1] * lag0))
    q1f = lane_tiles(lambda mo: _cmul(*col(ctr[0], mo), *f_up))
    q0b = lane_tiles(lambda mo: _cmul(*col(ctr[1], mo), *b_down))
    qrb = lane_tiles(lambda mo: _cmul(*col(ctr[1], mo), *b_down1))

    hp = lax.Precision.HIGHEST
    btr = []
    for d in range(2):
        lam = lamr_ref[d]
        _, _, cf_re, cf_im = _s5_disc(lam[0:1, :], lam[1:2, :], lam[2:3, :])
        btr.append(_cmul(cf_re, cf_im, br_ref[d, 0], br_ref[d, 1]))

    def gen(bt, q):
        return (jnp.dot(bt[0], q[0], precision=hp, preferred_element_type=F32)
                - jnp.dot(bt[1], q[1], precision=hp, preferred_element_type=F32))

    w0 = gen(btr[0], q0f) + gen(btr[1], q0b)
    w1 = gen(btr[0], q1f)
    for mo in range(m):
        w_scr[:, 2 * mo * c:(2 * mo + 1) * c] = w0[:, mo * c:(mo + 1) * c]
        w_scr[:, (2 * mo + 1) * c:(2 * mo + 2) * c] = w1[:, mo * c:(mo + 1) * c]

    def fill(mi, carry):
        row = jnp.broadcast_to(w_scr[pl.ds(mi, 1), :], (c, 2 * cm))
        win = pltpu.roll(row, 2 * cm - (c - 1), 1, stride=1, stride_axis=0)
        blk = jnp.concatenate([win[:, 2 * mo * c:(2 * mo + 1) * c] for mo in range(m)], axis=1)
        t_scr[pl.ds(pl.multiple_of(mi * c, c), c), :] = blk.astype(BF16)
        return carry

    lax.fori_loop(0, m, fill, 0)

    for d, q in ((0, q1f), (1, qrb)):
        base = cm + 2 * p * d
        t_scr[base:base + p, :] = q[0].astype(BF16)
        t_scr[base + p:base + 2 * p, :] = (-q[1]).astype(BF16)

    wb_f = lane_tiles(lambda mi: _cmul(*col(bbar[0], mi), *f_down))
    wb_b = lane_tiles(lambda mi: _cmul(*col(bbar[1], mi), *b_up))
    wb = jnp.concatenate([wb_f[0], wb_f[1], wb_b[0], wb_b[1]], axis=0).astype(BF16)
    hloc = _dot_nt(x_scr[:, :cm], wb)

    sign = jnp.where(lax.broadcasted_iota(jnp.int32, (1, 2 * p), 1) < p, -1.0, 1.0)
    order_f = list(range(n_all))
    order_b = list(range(n_ctx - 1, -1, -1)) + list(range(n_all - 1, n_ctx - 1, -1))
    for d, order in ((0, order_f), (1, order_b)):
        lam = lamr2_ref[d]
        a_re, a_im, _, _ = _s5_disc(lam[0:1, :], lam[1:2, :], lam[2:3, :])
        mag = jnp.exp(a_re * c)
        ac_r = mag * jnp.cos(a_im * c)
        ac_i = mag * jnp.sin(a_im * c) * sign
        state = jnp.zeros((batch, 2 * p), F32)
        colx = cm + 2 * p * d
        for n in order:
            x_scr[n * batch:(n + 1) * batch, colx:colx + 2 * p] = state.astype(BF16)
            local = hloc[n * batch:(n + 1) * batch, 2 * p * d:2 * p * (d + 1)]
            state = state * ac_r + pltpu.roll(state, p, 1) * ac_i + local

    y = _dot(x_scr[...], t_scr[...])
    for mo in range(m):
        skip = d_ref[mo:mo + 1, :] * u_ref[pl.ds(mo, rows, stride=m), :]
        y_ref[pl.ds(mo, rows, stride=m), :] = y[:, mo * c:(mo + 1) * c] + skip


def _s5_operands(u, lam_re, lam_im, log_dt, b_re, b_im, c_re, c_im, d_skip):
    g, n_all, batch, m, c = u.shape
    dtb = jnp.broadcast_to(log_dt[:, :, None], lam_re.shape)
    lam3 = jnp.stack([lam_re, lam_im, dtb], axis=-1)
    lamc = jnp.pad(lam3, ((0, 0), (0, 0), (0, 0), (0, LANES - 3))).transpose(1, 0, 2, 3)
    lam3r = jnp.stack([lam_re, lam_im, dtb], axis=2)
    lamr = jnp.pad(lam3r, ((0, 0), (0, 0), (0, 5), (0, 0))).transpose(1, 0, 2, 3)
    lamr2 = jnp.concatenate([lamr, lamr], axis=-1)
    bcol = jnp.stack([b_re, b_im], axis=1)
    bc = jnp.pad(bcol, ((0, 0),) * 4 + ((0, LANES - m),)).transpose(2, 0, 1, 3, 4)
    brow = jnp.swapaxes(bcol, -1, -2).transpose(2, 0, 1, 3, 4)
    ctc = jnp.swapaxes(jnp.stack([c_re, c_im], axis=1), -1, -2)
    ctc = jnp.pad(ctc, ((0, 0),) * 4 + ((0, LANES - m),)).transpose(2, 0, 1, 3, 4)
    dvec = jnp.broadcast_to(d_skip.reshape(g, m, 1), (g, m, LANES))
    return (u.reshape(g, n_all * batch * m, c), lamc, lamr, lamr2, bc, brow, ctc, dvec)


def _s5_mixer(r, mod, gain, lam_re, lam_im, log_dt, b_re, b_im, c_re, c_im, d_skip, glu_w, glu_b,
              n_ctx, n_ctx_tiles):
    b = r.shape[0]
    u = _norm_chunk_major(r, mod, gain, n_ctx_tiles)
    g, n_all, _, m, c = u.shape
    p = S5_STATE
    rows = n_all * b
    cm = c * m
    kern = functools.partial(_s5_kernel, batch=b, n_ctx=n_ctx // c, n_all=n_all)
    per_g = lambda *tail: pl.BlockSpec((None,) + tail, lambda i: (i,) + (0,) * len(tail))
    y = pl.pallas_call(
        kern,
        grid=(g,),
        in_specs=[per_g(rows * m, c), per_g(2, p, LANES), per_g(2, 8, p), per_g(2, 8, 2 * p),
                  per_g(2, 2, p, LANES), per_g(2, 2, m, p), per_g(2, 2, p, LANES), per_g(m, LANES)],
        out_specs=per_g(rows * m, c),
        out_shape=jax.ShapeDtypeStruct((g, rows * m, c), F32),
        scratch_shapes=[pltpu.VMEM((cm + 4 * p, cm), BF16), pltpu.VMEM((rows, cm + 4 * p), BF16),
                        pltpu.VMEM((m, 2 * cm), F32)],
        compiler_params=_cparams("arbitrary"),
        name="s5_scan",
    )(*_s5_operands(u, lam_re, lam_im, log_dt, b_re, b_im, c_re, c_im, d_skip))
    return _glu_res(r, y.reshape(u.shape), mod, glu_w.astype(BF16), glu_b, n_ctx_tiles)


def kernel(x, c, ctx, c_ctx, ada_w, ada_b, norm_g, ffn_w_in, ffn_w_out, mla_w_in, mla_q_norm, mla_kv_norm,
           mla_w_uq, mla_w_ukv, mla_q_gain, mla_k_gain, mla_w_o, ret_w_in, ret_log1m_gamma, ret_gn_gain, ret_w_o,
           s5_lam_re, s5_lam_im, s5_log_dt, s5_b_re, s5_b_im, s5_c_re, s5_c_im, s5_d, s5_glu_w, s5_glu_b):
    b, s, d = x.shape
    n_ctx = ctx.shape[1]
    depth = ada_w.shape[0]
    assert n_ctx % TOKEN_TILE == 0 and s % TOKEN_TILE == 0 and s % GRID_W == 0
    n_ctx_tiles = n_ctx // TOKEN_TILE

    pad = (-(b + 1)) % 8
    cvec = jnp.concatenate([c, c_ctx[None, :], jnp.zeros((pad, d), F32)], axis=0)
    mod_all = _modulation(cvec, ada_w, ada_b)
    mod_lat = mod_all[:, :b].reshape(depth, b, 1, N_MOD, d)
    mod_ctx = jnp.broadcast_to(mod_all[:, b].reshape(depth, 1, 1, N_MOD, d), (depth, b, 1, N_MOD, d))
    mods = jnp.concatenate([mod_ctx, mod_lat], axis=2)

    r = jnp.concatenate([ctx, x], axis=1)
    mla_tabs = _mla_tables(n_ctx, s)
    ret_tabs = _ret_tables(n_ctx, s)

    for i in range(depth):
        kind, j = i % N_MIXERS, i // N_MIXERS
        mod = mods[i]
        r = _ffn(r, mod, norm_g[i, 0], ffn_w_in[i, 0].astype(BF16), ffn_w_out[i, 0].astype(BF16), 0, n_ctx_tiles)
        if kind == 0:
            weights = _mla_weights(mla_w_in[j], mla_q_norm[j], mla_kv_norm[j], mla_w_uq[j], mla_w_ukv[j],
                                   mla_q_gain[j], mla_k_gain[j])
            q, k, v = _mla_proj(r, *mla_tabs, mod, norm_g[i, 1], weights, n_ctx_tiles)
            o = _attention(q, k, v, n_ctx)
            r = _proj_res(r, o, mod, mla_w_o[j].astype(BF16), n_ctx_tiles)
        elif kind == 1:
            q, k, v, sg = _ret_proj(r, *ret_tabs, mod, norm_g[i, 1], ret_w_in[j].astype(BF16), n_ctx_tiles)
            y = _ret_scan(q, k, v, sg, ret_log1m_gamma[j], ret_gn_gain[j], n_ctx)
            r = _proj_res(r, y, mod, ret_w_o[j].astype(BF16), n_ctx_tiles)
        else:
            r = _s5_mixer(r, mod, norm_g[i, 1], s5_lam_re[j], s5_lam_im[j], s5_log_dt[j], s5_b_re[j], s5_b_im[j],
                          s5_c_re[j], s5_c_im[j], s5_d[j], s5_glu_w[j], s5_glu_b[j], n_ctx, n_ctx_tiles)
        r = _ffn(r, mod, norm_g[i, 2], ffn_w_in[i, 1].astype(BF16), ffn_w_out[i, 1].astype(BF16), 2, n_ctx_tiles)
    return r[:, n_ctx:, :]
```

```python
import functools
import math

import jax
import jax.numpy as jnp
import numpy as np
from jax import lax
from jax.experimental import pallas as pl
from jax.experimental.pallas import tpu as pltpu

F32 = jnp.float32
BF16 = jnp.bfloat16

EPS = 1e-6
ROPE_BASE = 10000.0
GRID_W = 64
N_MIXERS = 3
N_MOD = 9

MLA_HEADS = 8
MLA_NOPE = 128
MLA_ROPE = 64
MLA_V = 128
MLA_Q_RANK = 384
MLA_KV_RANK = 256
MLA_HEAD_PAD = 256

RET_HEADS = 4
RET_DK = 256
RET_DV = 512
SCAN_CHUNK = 128

S5_GROUP = 16
S5_STATE = 64

LANES = 128
TOKEN_TILE = 256
ATTN_HEADS_PER_STEP = 4
VMEM_LIMIT_BYTES = 56 * 1024 * 1024


def _cparams(*sem):
    return pltpu.CompilerParams(dimension_semantics=sem, vmem_limit_bytes=VMEM_LIMIT_BYTES)


def _resident(shape):
    zeros = (0,) * len(shape)
    return pl.BlockSpec(shape, lambda *_: zeros)


def _silu(x):
    return x * jax.nn.sigmoid(x)


def _rms(x, n):
    return x * lax.rsqrt(jnp.sum(x * x, axis=-1, keepdims=True) * (1.0 / n) + EPS)


def _pre_norm(x, gain, mod_ref, k):
    y = _rms(x, x.shape[-1]) * gain
    return y * (1.0 + mod_ref[3 * k + 1:3 * k + 2, :]) + mod_ref[3 * k:3 * k + 1, :]


def _dot(a, b):
    return jnp.dot(a, b, preferred_element_type=F32)


def _dot_nt(a, b):
    return lax.dot_general(a, b, (((1,), (1,)), ((), ())), preferred_element_type=F32)


def _mod_kernel(cv_ref, w_ref, b_ref, o_ref):
    s = _silu(cv_ref[...]).astype(BF16)
    o_ref[...] = _dot(s, w_ref[...].astype(BF16)) + b_ref[...]


def _modulation(cvec, ada_w, ada_b):
    depth, d, n = ada_w.shape
    rows = cvec.shape[0]
    tn = n // 4
    return pl.pallas_call(
        _mod_kernel,
        grid=(depth, n // tn),
        in_specs=[
            _resident((rows, d)),
            pl.BlockSpec((None, d, tn), lambda i, j: (i, 0, j)),
            pl.BlockSpec((None, 1, tn), lambda i, j: (i, 0, j)),
        ],
        out_specs=pl.BlockSpec((None, rows, tn), lambda i, j: (i, 0, j)),
        out_shape=jax.ShapeDtypeStruct((depth, rows, n), F32),
        compiler_params=_cparams("arbitrary", "arbitrary"),
        name="adaln_modulation",
    )(cvec, ada_w, ada_b.reshape(depth, 1, n))


def _rows_spec(width, tile0=0):
    return pl.BlockSpec((None, TOKEN_TILE, width), lambda i, j: (i, j + tile0, 0))


def _mod_spec(d, n_ctx_tiles, tile0=0):
    return pl.BlockSpec((None, None, N_MOD, d),
                        lambda i, j: (i, (j + tile0 >= n_ctx_tiles).astype(jnp.int32), 0, 0))


def _row_call(kernel, name, b, n_tiles, inputs, consts, outs):
    in_specs = [spec for _, spec in inputs] + [_resident(a.shape) for a in consts]
    return pl.pallas_call(
        kernel,
        grid=(b, n_tiles),
        in_specs=in_specs,
        out_specs=[_rows_spec(w) for w, _ in outs],
        out_shape=[jax.ShapeDtypeStruct((b, n_tiles * TOKEN_TILE, w), dt) for w, dt in outs],
        compiler_params=_cparams("arbitrary", "arbitrary"),
        name=name,
    )(*[a for a, _ in inputs], *consts)


def _gelu_tanh(x):
    cdf = 0.5 * (1.0 + jnp.tanh(math.sqrt(2.0 / math.pi) * (x + 0.044715 * (x * x * x))))
    return x * cdf


def _swiglu_step(x, mod_ref, g_ref, win_ref, wout_ref, k):
    xn = _pre_norm(x, g_ref[...], mod_ref, k).astype(BF16)
    f = wout_ref.shape[0]
    a = _dot(xn, win_ref[:, :f])
    b = _dot(xn, win_ref[:, f:])
    hm = (_silu(a) * b).astype(BF16)
    return x + (0.5 * mod_ref[3 * k + 2:3 * k + 3, :]) * _dot(hm, wout_ref[...])


def _ffn_first_kernel(ctx_ref, x_ref, mod_ref, g_ref, win_ref, wout_ref, o_ref, *, n_ctx_tiles):
    x = jnp.where(pl.program_id(1) < n_ctx_tiles, ctx_ref[...], x_ref[...])
    o_ref[...] = _swiglu_step(x, mod_ref, g_ref, win_ref, wout_ref, 0)


def _ffn_kernel(r_ref, mod_ref, g_ref, win_ref, wout_ref, o_ref):
    o_ref[...] = _swiglu_step(r_ref[...], mod_ref, g_ref, win_ref, wout_ref, 0)


def _ffn_proj_kernel(r_ref, y_ref, mod_ref, pw_ref, g_ref, win_ref, wout_ref, o_ref):
    x = r_ref[...] + mod_ref[5:6, :] * _dot(y_ref[...], pw_ref[...])
    o_ref[...] = _swiglu_step(x, mod_ref, g_ref, win_ref, wout_ref, 2)


def _ffn_glu_kernel(r_ref, y_ref, mod_ref, pw_ref, pb_ref, g_ref, win_ref, wout_ref, o_ref):
    d = r_ref.shape[-1]
    c = SCAN_CHUNK
    y = jnp.concatenate([y_ref[:, ci].reshape(d, c).T for ci in range(TOKEN_TILE // c)], axis=0)
    ag = _dot(_gelu_tanh(y).astype(BF16), pw_ref[...]) + pb_ref[...]
    x = r_ref[...] + mod_ref[5:6, :] * (ag[:, :d] * jax.nn.sigmoid(ag[:, d:]))
    o_ref[...] = _swiglu_step(x, mod_ref, g_ref, win_ref, wout_ref, 2)


def _ffn_first(ctx, x, mod, gain, w_in, w_out, n_ctx_tiles):
    b, s, d = x.shape
    last_ctx = n_ctx_tiles - 1
    inputs = [(ctx, pl.BlockSpec((None, TOKEN_TILE, d), lambda i, j: (i, jnp.minimum(j, last_ctx), 0))),
              (x, pl.BlockSpec((None, TOKEN_TILE, d), lambda i, j: (i, jnp.maximum(j - n_ctx_tiles, 0), 0))),
              (mod, _mod_spec(d, n_ctx_tiles))]
    (out,) = _row_call(functools.partial(_ffn_first_kernel, n_ctx_tiles=n_ctx_tiles), "swiglu_first",
                       b, n_ctx_tiles + s // TOKEN_TILE, inputs, [gain.reshape(1, d), w_in, w_out], [(d, F32)])
    return out


def _ffn(r, mod, gain, w_in, w_out, n_ctx_tiles):
    b, t, d = r.shape
    inputs = [(r, _rows_spec(d)), (mod, _mod_spec(d, n_ctx_tiles))]
    (out,) = _row_call(_ffn_kernel, "swiglu_half_step", b, t // TOKEN_TILE, inputs,
                       [gain.reshape(1, d), w_in, w_out], [(d, F32)])
    return out


def _ffn_after_mixer(r, y, y_spec, mix_consts, kern, name, mod, gain, w_in, w_out, n_ctx_tiles, tile0):
    b, t, d = r.shape
    inputs = [(r, _rows_spec(d, tile0)), (y, y_spec), (mod, _mod_spec(d, n_ctx_tiles, tile0))]
    (out,) = _row_call(kern, name, b, t // TOKEN_TILE - tile0, inputs,
                       mix_consts + [gain.reshape(1, d), w_in, w_out], [(d, F32)])
    return out


def _rope_tile_perm():
    src = np.full((LANES,), -1, np.int64)
    src[0:16] = np.arange(0, 16)
    src[16:32] = np.arange(32, 48)
    src[64:80] = np.arange(16, 32)
    src[80:96] = np.arange(48, 64)
    return src


def _pad_rope_cols(w):
    src = _rope_tile_perm()
    cols = jnp.take(w, jnp.asarray(np.maximum(src, 0)), axis=-1)
    return jnp.where(jnp.asarray(src >= 0), cols, 0.0)


def _mla_tables(n_ctx, n_lat):
    half = MLA_ROPE // 2
    inv = ROPE_BASE ** (-jnp.arange(0, half, 2, dtype=F32) / half)
    pos = jnp.arange(n_lat)
    ang_r = (pos // GRID_W).astype(F32)[:, None] * inv[None, :]
    ang_c = (pos % GRID_W).astype(F32)[:, None] * inv[None, :]
    z = jnp.zeros((n_lat, 32), F32)
    cos = jnp.concatenate([jnp.cos(ang_r), jnp.cos(ang_c), z, jnp.cos(ang_r), jnp.cos(ang_c), z], axis=-1)
    sin = jnp.concatenate([-jnp.sin(ang_r), -jnp.sin(ang_c), z, jnp.sin(ang_r), jnp.sin(ang_c), z], axis=-1)
    cos = jnp.concatenate([jnp.ones((n_ctx, LANES), F32), cos], axis=0)
    sin = jnp.concatenate([jnp.zeros((n_ctx, LANES), F32), sin], axis=0)
    return cos, sin


def _mla_proj_kernel(r_ref, cos_ref, sin_ref, mod_ref, g_ref, win_ref, qn_ref, kvn_ref, wuq_ref, wukv_ref,
                     qg_ref, kg_ref, q_ref, k_ref, v_ref):
    xn = _pre_norm(r_ref[...], g_ref[...], mod_ref, 1).astype(BF16)
    p = _dot(xn, win_ref[...])
    cos = cos_ref[...]
    sin = sin_ref[...]

    def rotate(x):
        return x * cos + pltpu.roll(x, LANES // 2, 1) * sin

    cq = (_rms(p[:, :MLA_Q_RANK], MLA_Q_RANK) * qn_ref[...]).astype(BF16)
    ckv = (_rms(p[:, MLA_Q_RANK:MLA_Q_RANK + MLA_KV_RANK], MLA_KV_RANK) * kvn_ref[...]).astype(BF16)
    kr = rotate(_rms(p[:, MLA_Q_RANK + MLA_KV_RANK:], MLA_ROPE) * kg_ref[:, LANES:]).astype(BF16)
    q = _dot(cq, wuq_ref[...])
    kv = _dot(ckv, wukv_ref[...])
    scale = (MLA_NOPE + MLA_ROPE) ** -0.5 * math.log2(math.e)
    for h in range(MLA_HEADS):
        a0 = h * MLA_HEAD_PAD
        qa = _rms(q[:, a0:a0 + LANES], MLA_NOPE) * qg_ref[:, :LANES]
        qb = rotate(_rms(q[:, a0 + LANES:a0 + 2 * LANES], MLA_ROPE) * qg_ref[:, LANES:])
        q_ref[:, a0:a0 + LANES] = (qa * scale).astype(BF16)
        q_ref[:, a0 + LANES:a0 + 2 * LANES] = (qb * scale).astype(BF16)
        kn = _rms(kv[:, h * LANES:(h + 1) * LANES], MLA_NOPE) * kg_ref[:, :LANES]
        k_ref[:, a0:a0 + LANES] = kn.astype(BF16)
        k_ref[:, a0 + LANES:a0 + 2 * LANES] = kr
    v_ref[...] = kv[:, MLA_HEADS * LANES:].astype(BF16)


def _mla_weights(w_in, q_norm, kv_norm, w_uq, w_ukv, q_gain, k_gain):
    lat = MLA_Q_RANK + MLA_KV_RANK
    w_in_p = jnp.concatenate([w_in[:, :lat], _pad_rope_cols(w_in[:, lat:])], axis=-1).astype(BF16)
    uq = w_uq.reshape(MLA_Q_RANK, MLA_HEADS, MLA_NOPE + MLA_ROPE)
    uq = jnp.concatenate([uq[..., :MLA_NOPE], _pad_rope_cols(uq[..., MLA_NOPE:])], axis=-1)
    uq = uq.reshape(MLA_Q_RANK, MLA_HEADS * MLA_HEAD_PAD).astype(BF16)
    ukv = w_ukv.reshape(MLA_KV_RANK, MLA_HEADS, MLA_NOPE + MLA_V)
    ukv = jnp.concatenate([ukv[..., :MLA_NOPE].reshape(MLA_KV_RANK, -1),
                           ukv[..., MLA_NOPE:].reshape(MLA_KV_RANK, -1)], axis=-1).astype(BF16)
    qg = jnp.concatenate([q_gain[:MLA_NOPE], _pad_rope_cols(q_gain[MLA_NOPE:])]).reshape(1, -1)
    kg = jnp.concatenate([k_gain[:MLA_NOPE], _pad_rope_cols(k_gain[MLA_NOPE:])]).reshape(1, -1)
    return [w_in_p, q_norm.reshape(1, -1), kv_norm.reshape(1, -1), uq, ukv, qg, kg]


def _mla_proj(r, cos, sin, mod, gain, weights, n_ctx_tiles):
    b, t, d = r.shape
    cos_b = jnp.broadcast_to(cos[None], (1,) + cos.shape)
    sin_b = jnp.broadcast_to(sin[None], (1,) + sin.shape)
    tt = TOKEN_TILE
    hw = MLA_HEADS * MLA_HEAD_PAD
    consts = [gain.reshape(1, d)] + weights
    in_specs = [pl.BlockSpec((None, tt, d), lambda i, j: (i, j, 0)),
                pl.BlockSpec((None, tt, LANES), lambda i, j: (0, j, 0)),
                pl.BlockSpec((None, tt, LANES), lambda i, j: (0, j, 0)),
                pl.BlockSpec((None, None, N_MOD, d),
                             lambda i, j: (i, (j >= n_ctx_tiles).astype(jnp.int32), 0, 0))]
    in_specs += [_resident(a.shape) for a in consts]
    widths = [(hw, BF16), (hw, BF16), (MLA_HEADS * MLA_V, BF16)]
    return pl.pallas_call(
        _mla_proj_kernel,
        grid=(b, t // tt),
        in_specs=in_specs,
        out_specs=[pl.BlockSpec((None, tt, w), lambda i, j: (i, j, 0)) for w, _ in widths],
        out_shape=[jax.ShapeDtypeStruct((b, t, w), dt) for w, dt in widths],
        compiler_params=_cparams("arbitrary", "arbitrary"),
        name="mla_projections",
    )(r, cos_b, sin_b, mod, *consts)


def _attn_kernel(q_ref, k_ref, v_ref, o_ref, *, n_ctx_tiles, n_ctx):
    def attend(n_keys):
        for h in range(ATTN_HEADS_PER_STEP):
            qk = slice(h * MLA_HEAD_PAD, (h + 1) * MLA_HEAD_PAD)
            vo = slice(h * MLA_V, (h + 1) * MLA_V)
            s = _dot_nt(q_ref[:, qk], k_ref[:n_keys, qk])
            p = jnp.exp2(s - jnp.max(s, axis=-1, keepdims=True))
            l = jnp.sum(p, axis=-1, keepdims=True)
            o = _dot(p.astype(BF16), v_ref[:n_keys, vo])
            o_ref[:, vo] = (o / l).astype(o_ref.dtype)

    is_ctx = pl.program_id(2) < n_ctx_tiles
    pl.when(is_ctx)(lambda: attend(n_ctx))
    pl.when(jnp.logical_not(is_ctx))(lambda: attend(k_ref.shape[0]))


def _attention(q, k, v, n_ctx):
    b, t, _ = q.shape
    tq = TOKEN_TILE
    hp = ATTN_HEADS_PER_STEP
    kern = functools.partial(_attn_kernel, n_ctx_tiles=n_ctx // tq, n_ctx=n_ctx)
    return pl.pallas_call(
        kern,
        grid=(b, MLA_HEADS // hp, t // tq),
        in_specs=[pl.BlockSpec((None, tq, hp * MLA_HEAD_PAD), lambda i, h, j: (i, j, h)),
                  pl.BlockSpec((None, t, hp * MLA_HEAD_PAD), lambda i, h, j: (i, 0, h)),
                  pl.BlockSpec((None, t, hp * MLA_V), lambda i, h, j: (i, 0, h))],
        out_specs=pl.BlockSpec((None, tq, hp * MLA_V), lambda i, h, j: (i, j, h)),
        out_shape=jax.ShapeDtypeStruct((b, t, MLA_HEADS * MLA_V), BF16),
        compiler_params=_cparams("arbitrary", "arbitrary", "arbitrary"),
        name="mla_attention",
    )(q, k, v)


def _ret_tables(n_ctx, n_lat):
    inv = ROPE_BASE ** (-jnp.arange(0, RET_DK, 2, dtype=F32) / RET_DK)
    ang = jnp.arange(n_lat, dtype=F32)[:, None] * inv[None, :]
    cos = jnp.concatenate([jnp.ones((n_ctx, LANES), F32), jnp.cos(ang)], axis=0)
    sin = jnp.concatenate([jnp.zeros((n_ctx, LANES), F32), jnp.sin(ang)], axis=0)
    return cos, sin


def _ret_proj_kernel(r_ref, cos_ref, sin_ref, mod_ref, g_ref, win_ref, q_ref, k_ref, v_ref, sg_ref):
    xn = _pre_norm(r_ref[...], g_ref[...], mod_ref, 1).astype(BF16)
    cos = cos_ref[...]
    sin = sin_ref[...]
    hdk = RET_HEADS * RET_DK
    hdv = RET_HEADS * RET_DV
    scale = RET_DK ** -0.5
    for out_ref, base, sc in ((q_ref, 0, None), (k_ref, hdk, scale)):
        for h in range(RET_HEADS):
            c0 = h * RET_DK
            x1 = _dot(xn, win_ref[:, base + c0:base + c0 + LANES])
            x2 = _dot(xn, win_ref[:, base + c0 + LANES:base + c0 + 2 * LANES])
            o1 = x1 * cos - x2 * sin
            o2 = x2 * cos + x1 * sin
            if sc is not None:
                o1 = o1 * sc
                o2 = o2 * sc
            out_ref[:, c0:c0 + LANES] = o1.astype(BF16)
            out_ref[:, c0 + LANES:c0 + 2 * LANES] = o2.astype(BF16)
    v_ref[...] = _dot(xn, win_ref[:, 2 * hdk:2 * hdk + hdv]).astype(BF16)
    sg_ref[...] = _silu(_dot(xn, win_ref[:, 2 * hdk + hdv:])).astype(BF16)


def _ret_proj(r, cos, sin, mod, gain, w_in, n_ctx_tiles):
    b, t, d = r.shape
    tt = TOKEN_TILE
    consts = [gain.reshape(1, d), w_in]
    in_specs = [pl.BlockSpec((None, tt, d), lambda i, j: (i, j, 0)),
                pl.BlockSpec((tt, LANES), lambda i, j: (j, 0)),
                pl.BlockSpec((tt, LANES), lambda i, j: (j, 0)),
                pl.BlockSpec((None, None, N_MOD, d),
                             lambda i, j: (i, (j >= n_ctx_tiles).astype(jnp.int32), 0, 0))]
    in_specs += [_resident(a.shape) for a in consts]
    widths = [(RET_HEADS * RET_DK, BF16), (RET_HEADS * RET_DK, BF16),
              (RET_HEADS * RET_DV, BF16), (RET_HEADS * RET_DV, BF16)]
    return pl.pallas_call(
        _ret_proj_kernel,
        grid=(b, t // tt),
        in_specs=in_specs,
        out_specs=[pl.BlockSpec((None, tt, w), lambda i, j: (i, j, 0)) for w, _ in widths],
        out_shape=[jax.ShapeDtypeStruct((b, t, w), dt) for w, dt in widths],
        compiler_params=_cparams("arbitrary", "arbitrary"),
        name="retention_projections",
    )(r, cos, sin, mod, *consts)


def _ret_scan_kernel(l1g_ref, q_ref, k_ref, v_ref, sg_ref, gain_ref, y_ref, ob_scr, st_scr, *, n_ctx, n_all):
    c = SCAN_CHUNK
    log_g = jnp.log1p(-jnp.exp(l1g_ref[...]))
    lgf = log_g[0:1, 0:1]
    lgb = log_g[1:2, 0:1]
    ii = lax.broadcasted_iota(jnp.int32, (c, c), 0)
    jj = lax.broadcasted_iota(jnp.int32, (c, c), 1)
    diff = (ii - jj).astype(F32)
    mask = jnp.where(diff >= 0.0, jnp.exp(lgf * jnp.maximum(diff, 0.0)), jnp.exp(lgb * jnp.maximum(-diff, 0.0)))
    ic = lax.broadcasted_iota(jnp.int32, (c, 1), 0).astype(F32)
    qdec_f = jnp.exp(lgf * (ic + 1.0))
    qdec_b = jnp.exp(lgb * (c - ic))
    kdec_f = jnp.exp(lgf * (c - 1.0 - ic))
    kdec_b = jnp.exp(lgb * ic)
    cdec_f = jnp.exp(lgf * c)
    cdec_b = jnp.exp(lgb * c)

    def kv_outer(kc, dec, vc):
        return _dot((kc.astype(F32) * dec).T.astype(BF16), vc)

    st_scr[...] = jnp.zeros_like(st_scr)
    for n in list(range(n_ctx - 1, -1, -1)) + list(range(n_all - 1, n_ctx - 1, -1)):
        rows = slice(n * c, (n + 1) * c)
        state = st_scr[...]
        ob_scr[rows, :] = _dot((q_ref[rows, :].astype(F32) * qdec_b).astype(BF16), state.astype(BF16))
        st_scr[...] = state * cdec_b + kv_outer(k_ref[rows, :], kdec_b, v_ref[rows, :])

    st_scr[...] = jnp.zeros_like(st_scr)
    gain = gain_ref[...]
    for n in range(n_all):
        rows = slice(n * c, (n + 1) * c)
        qc = q_ref[rows, :]
        kc = k_ref[rows, :]
        vc = v_ref[rows, :]
        state = st_scr[...]
        scores = _dot_nt(qc, kc) * mask
        o = (_dot(scores.astype(BF16), vc)
             + _dot((qc.astype(F32) * qdec_f).astype(BF16), state.astype(BF16))
             + ob_scr[rows, :])
        mu = jnp.mean(o, axis=-1, keepdims=True)
        dev = o - mu
        on = dev * lax.rsqrt(jnp.mean(dev * dev, axis=-1, keepdims=True) + EPS)
        y_ref[rows, :] = (on * gain * sg_ref[rows, :].astype(F32)).astype(y_ref.dtype)
        st_scr[...] = state * cdec_f + kv_outer(kc, kdec_f, vc)


def _ret_scan(q, k, v, sg, log1m_gamma, gn_gain, n_ctx):
    b, t, _ = q.shape
    l1g = jnp.broadcast_to(log1m_gamma.T[:, :, None], (RET_HEADS, 2, LANES))
    l1g = jnp.concatenate([l1g, jnp.zeros((RET_HEADS, 6, LANES), F32) - 1.0], axis=1)
    kern = functools.partial(_ret_scan_kernel, n_ctx=n_ctx // SCAN_CHUNK, n_all=t // SCAN_CHUNK)
    return pl.pallas_call(
        kern,
        grid=(b, RET_HEADS),
        in_specs=[pl.BlockSpec((None, 8, LANES), lambda i, h: (h, 0, 0)),
                  pl.BlockSpec((None, t, RET_DK), lambda i, h: (i, 0, h)),
                  pl.BlockSpec((None, t, RET_DK), lambda i, h: (i, 0, h)),
                  pl.BlockSpec((None, t, RET_DV), lambda i, h: (i, 0, h)),
                  pl.BlockSpec((None, t, RET_DV), lambda i, h: (i, 0, h)),
                  pl.BlockSpec((1, RET_DV), lambda i, h: (0, h))],
        out_specs=pl.BlockSpec((None, t, RET_DV), lambda i, h: (i, 0, h)),
        out_shape=jax.ShapeDtypeStruct((b, t, RET_HEADS * RET_DV), BF16),
        scratch_shapes=[pltpu.VMEM((t, RET_DV), F32), pltpu.VMEM((RET_DK, RET_DV), F32)],
        compiler_params=_cparams("arbitrary", "arbitrary"),
        name="retention_scan",
    )(l1g, q, k, v, sg, gn_gain.reshape(1, -1))


CHUNKS_PER_TILE = TOKEN_TILE // SCAN_CHUNK


def _chunk_major_spec(g, tile0=0):
    return pl.BlockSpec((g, CHUNKS_PER_TILE, None, S5_GROUP, SCAN_CHUNK), lambda i, j: (0, j + tile0, i, 0, 0))


def _norm_t_kernel(r_ref, mod_ref, g_ref, o_ref):
    xn = _pre_norm(r_ref[...], g_ref[...], mod_ref, 1)
    c = SCAN_CHUNK
    for ci in range(CHUNKS_PER_TILE):
        o_ref[:, ci] = xn[ci * c:(ci + 1) * c, :].T.reshape(o_ref.shape[0], S5_GROUP, c)


def _norm_chunk_major(r, mod, gain, n_ctx_tiles):
    b, t, d = r.shape
    g = d // S5_GROUP
    return pl.pallas_call(
        _norm_t_kernel,
        grid=(b, t // TOKEN_TILE),
        in_specs=[_rows_spec(d), _mod_spec(d, n_ctx_tiles), _resident((1, d))],
        out_specs=_chunk_major_spec(g),
        out_shape=jax.ShapeDtypeStruct((g, t // SCAN_CHUNK, b, S5_GROUP, SCAN_CHUNK), F32),
        compiler_params=_cparams("arbitrary", "arbitrary"),
        name="s5_pre_norm",
    )(r, mod, gain.reshape(1, d))


def _s5_disc(lam_re, lam_im, log_dt):
    dt = jnp.exp(log_dt)
    a_re = lam_re * dt
    a_im = lam_im * dt
    mag = jnp.exp(a_re)
    e_re = mag * jnp.cos(a_im) - 1.0
    e_im = mag * jnp.sin(a_im)
    den = lam_re * lam_re + lam_im * lam_im
    return a_re, a_im, (e_re * lam_re + e_im * lam_im) / den, (e_im * lam_re - e_re * lam_im) / den


def _cmul(ar, ai, br, bi):
    return ar * br - ai * bi, ar * bi + ai * br


def _s5_kernel(u_ref, lamc_ref, lamr_ref, lamr2_ref, bc_ref, br_ref, ct_ref, d_ref,
               y_ref, t_scr, x_scr, w_scr, *, batch, n_ctx, n_all):
    c, m, p = SCAN_CHUNK, S5_GROUP, S5_STATE
    cm = c * m
    rows = n_all * batch
    for mi in range(m):
        x_scr[:, mi * c:(mi + 1) * c] = u_ref[pl.ds(mi, rows, stride=m), :].astype(BF16)

    kk = lax.broadcasted_iota(jnp.int32, (1, LANES), 1).astype(F32)

    def powers(a_re, a_im, expo):
        mag = jnp.exp(a_re * expo)
        return mag * jnp.cos(a_im * expo), mag * jnp.sin(a_im * expo)

    def lane_tiles(fn):
        parts = [fn(i) for i in range(m)]
        return (jnp.concatenate([q[0] for q in parts], axis=1), jnp.concatenate([q[1] for q in parts], axis=1))

    disc = [_s5_disc(lamc_ref[d][:, 0:1], lamc_ref[d][:, 1:2], lamc_ref[d][:, 2:3]) for d in range(2)]
    bbar = [_cmul(disc[d][2], disc[d][3], bc_ref[d, 0], bc_ref[d, 1]) for d in range(2)]
    ctr = [(ct_ref[d, 0], ct_ref[d, 1]) for d in range(2)]

    def col(pair, i):
        return pair[0][:, i:i + 1], pair[1][:, i:i + 1]

    f_up = powers(disc[0][0], disc[0][1], kk + 1.0)
    f_down = powers(disc[0][0], disc[0][1], (c - 1.0) - kk)
    b_up = powers(disc[1][0], disc[1][1], kk)
    b_down = powers(disc[1][0], disc[1][1], (c - 1.0) - kk)
    b_down1 = powers(disc[1][0], disc[1][1], c - kk)
    lag0 = jnp.where(kk == c - 1.0, 1.0, 0.0)

    q0f = lane_tiles(lambda mo: (col(ctr[0], mo)[0] * lag0, col(ctr[0], mo)[1] * lag0))
    q1f = lane_tiles(lambda mo: _cmul(*col(ctr[0], mo), *f_up))
    q0b = lane_tiles(lambda mo: _cmul(*col(ctr[1], mo), *b_down))
    qrb = lane_tiles(lambda mo: _cmul(*col(ctr[1], mo), *b_down1))

    hp = lax.Precision.HIGHEST
    btr = []
    for d in range(2):
        lam = lamr_ref[d]
        _, _, cf_re, cf_im = _s5_disc(lam[0:1, :], lam[1:2, :], lam[2:3, :])
        btr.append(_cmul(cf_re, cf_im, br_ref[d, 0], br_ref[d, 1]))

    def gen(bt, q):
        return (jnp.dot(bt[0], q[0], precision=hp, preferred_element_type=F32)
                - jnp.dot(bt[1], q[1], precision=hp, preferred_element_type=F32))

    w0 = gen(btr[0], q0f) + gen(btr[1], q0b)
    w1 = gen(btr[0], q1f)
    for mo in range(m):
        w_scr[:, 2 * mo * c:(2 * mo + 1) * c] = w0[:, mo * c:(mo + 1) * c]
        w_scr[:, (2 * mo + 1) * c:(2 * mo + 2) * c] = w1[:, mo * c:(mo + 1) * c]

    def fill(mi, carry):
        row = jnp.broadcast_to(w_scr[pl.ds(mi, 1), :], (c, 2 * cm))
        win = pltpu.roll(row, 2 * cm - (c - 1), 1, stride=1, stride_axis=0)
        blk = jnp.concatenate([win[:, 2 * mo * c:(2 * mo + 1) * c] for mo in range(m)], axis=1)
        t_scr[pl.ds(pl.multiple_of(mi * c, c), c), :] = blk.astype(BF16)
        return carry

    lax.fori_loop(0, m, fill, 0)

    for d, q in ((0, q1f), (1, qrb)):
        base = cm + 2 * p * d
        t_scr[base:base + p, :] = q[0].astype(BF16)
        t_scr[base + p:base + 2 * p, :] = (-q[1]).astype(BF16)

    wb_f = lane_tiles(lambda mi: _cmul(*col(bbar[0], mi), *f_down))
    wb_b = lane_tiles(lambda mi: _cmul(*col(bbar[1], mi), *b_up))
    wb = jnp.concatenate([wb_f[0], wb_f[1], wb_b[0], wb_b[1]], axis=0).astype(BF16)
    hloc = _dot_nt(x_scr[:, :cm], wb)

    sign = jnp.where(lax.broadcasted_iota(jnp.int32, (1, 2 * p), 1) < p, -1.0, 1.0)
    order_f = list(range(n_all))
    order_b = list(range(n_ctx - 1, -1, -1)) + list(range(n_all - 1, n_ctx - 1, -1))
    for d, order in ((0, order_f), (1, order_b)):
        lam = lamr2_ref[d]
        a_re, a_im, _, _ = _s5_disc(lam[0:1, :], lam[1:2, :], lam[2:3, :])
        mag = jnp.exp(a_re * c)
        ac_r = mag * jnp.cos(a_im * c)
        ac_i = mag * jnp.sin(a_im * c) * sign
        state = jnp.zeros((batch, 2 * p), F32)
        colx = cm + 2 * p * d
        for n in order:
            x_scr[n * batch:(n + 1) * batch, colx:colx + 2 * p] = state.astype(BF16)
            local = hloc[n * batch:(n + 1) * batch, 2 * p * d:2 * p * (d + 1)]
            state = state * ac_r + pltpu.roll(state, p, 1) * ac_i + local

    y = _dot(x_scr[...], t_scr[...])
    for mo in range(m):
        skip = d_ref[mo:mo + 1, :] * u_ref[pl.ds(mo, rows, stride=m), :]
        y_ref[pl.ds(mo, rows, stride=m), :] = y[:, mo * c:(mo + 1) * c] + skip


def _s5_operands(u, lam_re, lam_im, log_dt, b_re, b_im, c_re, c_im, d_skip):
    g, n_all, batch, m, c = u.shape
    dtb = jnp.broadcast_to(log_dt[:, :, None], lam_re.shape)
    lam3 = jnp.stack([lam_re, lam_im, dtb], axis=-1)
    lamc = jnp.pad(lam3, ((0, 0), (0, 0), (0, 0), (0, LANES - 3))).transpose(1, 0, 2, 3)
    lam3r = jnp.stack([lam_re, lam_im, dtb], axis=2)
    lamr = jnp.pad(lam3r, ((0, 0), (0, 0), (0, 5), (0, 0))).transpose(1, 0, 2, 3)
    lamr2 = jnp.concatenate([lamr, lamr], axis=-1)
    bcol = jnp.stack([b_re, b_im], axis=1)
    bc = jnp.pad(bcol, ((0, 0),) * 4 + ((0, LANES - m),)).transpose(2, 0, 1, 3, 4)
    brow = jnp.swapaxes(bcol, -1, -2).transpose(2, 0, 1, 3, 4)
    ctc = jnp.swapaxes(jnp.stack([c_re, c_im], axis=1), -1, -2)
    ctc = jnp.pad(ctc, ((0, 0),) * 4 + ((0, LANES - m),)).transpose(2, 0, 1, 3, 4)
    dvec = jnp.broadcast_to(d_skip.reshape(g, m, 1), (g, m, LANES))
    return (u.reshape(g, n_all * batch * m, c), lamc, lamr, lamr2, bc, brow, ctc, dvec)


def _s5_mixer(r, mod, gain, lam_re, lam_im, log_dt, b_re, b_im, c_re, c_im, d_skip, n_ctx, n_ctx_tiles):
    b = r.shape[0]
    u = _norm_chunk_major(r, mod, gain, n_ctx_tiles)
    g, n_all, _, m, c = u.shape
    p = S5_STATE
    rows = n_all * b
    cm = c * m
    kern = functools.partial(_s5_kernel, batch=b, n_ctx=n_ctx // c, n_all=n_all)
    per_g = lambda *tail: pl.BlockSpec((None,) + tail, lambda i: (i,) + (0,) * len(tail))
    y = pl.pallas_call(
        kern,
        grid=(g,),
        in_specs=[per_g(rows * m, c), per_g(2, p, LANES), per_g(2, 8, p), per_g(2, 8, 2 * p),
                  per_g(2, 2, p, LANES), per_g(2, 2, m, p), per_g(2, 2, p, LANES), per_g(m, LANES)],
        out_specs=per_g(rows * m, c),
        out_shape=jax.ShapeDtypeStruct((g, rows * m, c), F32),
        scratch_shapes=[pltpu.VMEM((cm + 4 * p, cm), BF16), pltpu.VMEM((rows, cm + 4 * p), BF16),
                        pltpu.VMEM((m, 2 * cm), F32)],
        compiler_params=_cparams("arbitrary"),
        name="s5_scan",
    )(*_s5_operands(u, lam_re, lam_im, log_dt, b_re, b_im, c_re, c_im, d_skip))
    return y.reshape(u.shape)


def kernel(x, c, ctx, c_ctx, ada_w, ada_b, norm_g, ffn_w_in, ffn_w_out, mla_w_in, mla_q_norm, mla_kv_norm,
           mla_w_uq, mla_w_ukv, mla_q_gain, mla_k_gain, mla_w_o, ret_w_in, ret_log1m_gamma, ret_gn_gain, ret_w_o,
           s5_lam_re, s5_lam_im, s5_log_dt, s5_b_re, s5_b_im, s5_c_re, s5_c_im, s5_d, s5_glu_w, s5_glu_b):
    b, s, d = x.shape
    n_ctx = ctx.shape[1]
    depth = ada_w.shape[0]
    assert n_ctx % TOKEN_TILE == 0 and s % TOKEN_TILE == 0 and s % GRID_W == 0
    n_ctx_tiles = n_ctx // TOKEN_TILE

    pad = (-(b + 1)) % 8
    cvec = jnp.concatenate([c, c_ctx[None, :], jnp.zeros((pad, d), F32)], axis=0)
    mod_all = _modulation(cvec, ada_w, ada_b)
    mod_lat = mod_all[:, :b].reshape(depth, b, 1, N_MOD, d)
    mod_ctx = jnp.broadcast_to(mod_all[:, b].reshape(depth, 1, 1, N_MOD, d), (depth, b, 1, N_MOD, d))
    mods = jnp.concatenate([mod_ctx, mod_lat], axis=2)

    mla_tabs = _mla_tables(n_ctx, s)
    ret_tabs = _ret_tables(n_ctx, s)

    r = None
    for i in range(depth):
        kind, j = i % N_MIXERS, i // N_MIXERS
        mod = mods[i]
        ffn0 = (norm_g[i, 0], ffn_w_in[i, 0].astype(BF16), ffn_w_out[i, 0].astype(BF16))
        ffn1 = (norm_g[i, 2], ffn_w_in[i, 1].astype(BF16), ffn_w_out[i, 1].astype(BF16))
        r = _ffn_first(ctx, x, mod, *ffn0, n_ctx_tiles) if i == 0 else _ffn(r, mod, *ffn0, n_ctx_tiles)
        tile0 = n_ctx_tiles if i == depth - 1 else 0
        if kind == 0:
            weights = _mla_weights(mla_w_in[j], mla_q_norm[j], mla_kv_norm[j], mla_w_uq[j], mla_w_ukv[j],
                                   mla_q_gain[j], mla_k_gain[j])
            q, k, v = _mla_proj(r, *mla_tabs, mod, norm_g[i, 1], weights, n_ctx_tiles)
            y = _attention(q, k, v, n_ctx)
            mix = (y, _rows_spec(y.shape[-1], tile0), [mla_w_o[j].astype(BF16)], _ffn_proj_kernel, "mla_out_swiglu")
        elif kind == 1:
            q, k, v, sg = _ret_proj(r, *ret_tabs, mod, norm_g[i, 1], ret_w_in[j].astype(BF16), n_ctx_tiles)
            y = _ret_scan(q, k, v, sg, ret_log1m_gamma[j], ret_gn_gain[j], n_ctx)
            mix = (y, _rows_spec(y.shape[-1], tile0), [ret_w_o[j].astype(BF16)], _ffn_proj_kernel, "ret_out_swiglu")
        else:
            y = _s5_mixer(r, mod, norm_g[i, 1], s5_lam_re[j], s5_lam_im[j], s5_log_dt[j], s5_b_re[j], s5_b_im[j],
                          s5_c_re[j], s5_c_im[j], s5_d[j], n_ctx, n_ctx_tiles)
            mix = (y, _chunk_major_spec(y.shape[0], tile0), [s5_glu_w[j].astype(BF16), s5_glu_b[j].reshape(1, -1)],
                   _ffn_glu_kernel, "s5_out_swiglu")
        r = _ffn_after_mixer(r, *mix, mod, *ffn1, n_ctx_tiles, tile0)
    return r
```

```python
import functools
import math

import jax
import jax.numpy as jnp
import numpy as np
from jax import lax
from jax.experimental import pallas as pl
from jax.experimental.pallas import tpu as pltpu

F32 = jnp.float32
BF16 = jnp.bfloat16

EPS = 1e-6
ROPE_BASE = 10000.0
GRID_W = 64
N_MIXERS = 3
N_MOD = 9

MLA_HEADS = 8
MLA_NOPE = 128
MLA_ROPE = 64
MLA_V = 128
MLA_Q_RANK = 384
MLA_KV_RANK = 256
MLA_HEAD_PAD = 256

RET_HEADS = 4
RET_DK = 256
RET_DV = 512
SCAN_CHUNK = 128

S5_GROUP = 16
S5_STATE = 64

LANES = 128
TOKEN_TILE = 256
ATTN_HEADS_PER_STEP = 4
VMEM_LIMIT_BYTES = 56 * 1024 * 1024


def _cparams(*sem):
    return pltpu.CompilerParams(dimension_semantics=sem, vmem_limit_bytes=VMEM_LIMIT_BYTES)


def _resident(shape):
    zeros = (0,) * len(shape)
    return pl.BlockSpec(shape, lambda *_: zeros)


def _silu(x):
    return x * jax.nn.sigmoid(x)


def _rms(x, n):
    return x * lax.rsqrt(jnp.sum(x * x, axis=-1, keepdims=True) * (1.0 / n) + EPS)


def _pre_norm(x, gain, mod_ref, k):
    y = _rms(x, x.shape[-1]) * gain
    return y * (1.0 + mod_ref[3 * k + 1:3 * k + 2, :]) + mod_ref[3 * k:3 * k + 1, :]


def _dot(a, b):
    return jnp.dot(a, b, preferred_element_type=F32)


def _dot_nt(a, b):
    return lax.dot_general(a, b, (((1,), (1,)), ((), ())), preferred_element_type=F32)


def _mod_kernel(cv_ref, w_ref, b_ref, o_ref):
    s = _silu(cv_ref[...]).astype(BF16)
    o_ref[...] = _dot(s, w_ref[...].astype(BF16)) + b_ref[...]


def _modulation(cvec, ada_w, ada_b):
    depth, d, n = ada_w.shape
    rows = cvec.shape[0]
    tn = n // 4
    return pl.pallas_call(
        _mod_kernel,
        grid=(depth, n // tn),
        in_specs=[
            _resident((rows, d)),
            pl.BlockSpec((None, d, tn), lambda i, j: (i, 0, j)),
            pl.BlockSpec((None, 1, tn), lambda i, j: (i, 0, j)),
        ],
        out_specs=pl.BlockSpec((None, rows, tn), lambda i, j: (i, 0, j)),
        out_shape=jax.ShapeDtypeStruct((depth, rows, n), F32),
        compiler_params=_cparams("arbitrary", "arbitrary"),
        name="adaln_modulation",
    )(cvec, ada_w, ada_b.reshape(depth, 1, n))


def _rows_spec(width, tile0=0):
    return pl.BlockSpec((None, TOKEN_TILE, width), lambda i, j: (i, j + tile0, 0))


def _mod_spec(d, n_ctx_tiles, tile0=0):
    return pl.BlockSpec((None, None, N_MOD, d),
                        lambda i, j: (i, (j + tile0 >= n_ctx_tiles).astype(jnp.int32), 0, 0))


def _row_call(kernel, name, b, n_tiles, inputs, consts, outs):
    in_specs = [spec for _, spec in inputs] + [_resident(a.shape) for a in consts]
    return pl.pallas_call(
        kernel,
        grid=(b, n_tiles),
        in_specs=in_specs,
        out_specs=[_rows_spec(w) for w, _ in outs],
        out_shape=[jax.ShapeDtypeStruct((b, n_tiles * TOKEN_TILE, w), dt) for w, dt in outs],
        compiler_params=_cparams("arbitrary", "arbitrary"),
        name=name,
    )(*[a for a, _ in inputs], *consts)


def _gelu_tanh(x):
    cdf = 0.5 * (1.0 + jnp.tanh(math.sqrt(2.0 / math.pi) * (x + 0.044715 * (x * x * x))))
    return x * cdf


def _swiglu_step(x, mod_ref, g_ref, win_ref, wout_ref, k):
    xn = _pre_norm(x, g_ref[...], mod_ref, k).astype(BF16)
    f = wout_ref.shape[0]
    a = _dot(xn, win_ref[:, :f])
    b = _dot(xn, win_ref[:, f:])
    hm = (_silu(a) * b).astype(BF16)
    return x + (0.5 * mod_ref[3 * k + 2:3 * k + 3, :]) * _dot(hm, wout_ref[...])


def _ffn_first_kernel(ctx_ref, x_ref, mod_ref, g_ref, win_ref, wout_ref, o_ref, *, n_ctx_tiles):
    x = jnp.where(pl.program_id(1) < n_ctx_tiles, ctx_ref[...], x_ref[...])
    o_ref[...] = _swiglu_step(x, mod_ref, g_ref, win_ref, wout_ref, 0)


def _ffn_kernel(r_ref, mod_ref, g_ref, win_ref, wout_ref, o_ref):
    o_ref[...] = _swiglu_step(r_ref[...], mod_ref, g_ref, win_ref, wout_ref, 0)


def _ffn_proj_kernel(r_ref, y_ref, mod_ref, pw_ref, g_ref, win_ref, wout_ref, o_ref):
    x = r_ref[...] + mod_ref[5:6, :] * _dot(y_ref[...], pw_ref[...])
    o_ref[...] = _swiglu_step(x, mod_ref, g_ref, win_ref, wout_ref, 2)


def _ffn_glu_kernel(r_ref, y_ref, mod_ref, pw_ref, pb_ref, g_ref, win_ref, wout_ref, o_ref):
    d = r_ref.shape[-1]
    c = SCAN_CHUNK
    y = jnp.concatenate([y_ref[:, ci].reshape(d, c).T for ci in range(TOKEN_TILE // c)], axis=0)
    ag = _dot(_gelu_tanh(y).astype(BF16), pw_ref[...]) + pb_ref[...]
    x = r_ref[...] + mod_ref[5:6, :] * (ag[:, :d] * jax.nn.sigmoid(ag[:, d:]))
    o_ref[...] = _swiglu_step(x, mod_ref, g_ref, win_ref, wout_ref, 2)


def _ffn_first(ctx, x, mod, gain, w_in, w_out, n_ctx_tiles):
    b, s, d = x.shape
    last_ctx = n_ctx_tiles - 1
    inputs = [(ctx, pl.BlockSpec((None, TOKEN_TILE, d), lambda i, j: (i, jnp.minimum(j, last_ctx), 0))),
              (x, pl.BlockSpec((None, TOKEN_TILE, d), lambda i, j: (i, jnp.maximum(j - n_ctx_tiles, 0), 0))),
              (mod, _mod_spec(d, n_ctx_tiles))]
    (out,) = _row_call(functools.partial(_ffn_first_kernel, n_ctx_tiles=n_ctx_tiles), "swiglu_first",
                       b, n_ctx_tiles + s // TOKEN_TILE, inputs, [gain.reshape(1, d), w_in, w_out], [(d, F32)])
    return out


def _ffn(r, mod, gain, w_in, w_out, n_ctx_tiles):
    b, t, d = r.shape
    inputs = [(r, _rows_spec(d)), (mod, _mod_spec(d, n_ctx_tiles))]
    (out,) = _row_call(_ffn_kernel, "swiglu_half_step", b, t // TOKEN_TILE, inputs,
                       [gain.reshape(1, d), w_in, w_out], [(d, F32)])
    return out


def _ffn_after_mixer(r, y, y_spec, mix_consts, kern, name, mod, gain, w_in, w_out, n_ctx_tiles, tile0):
    b, t, d = r.shape
    inputs = [(r, _rows_spec(d, tile0)), (y, y_spec), (mod, _mod_spec(d, n_ctx_tiles, tile0))]
    (out,) = _row_call(kern, name, b, t // TOKEN_TILE - tile0, inputs,
                       mix_consts + [gain.reshape(1, d), w_in, w_out], [(d, F32)])
    return out


def _rope_tile_perm():
    src = np.full((LANES,), -1, np.int64)
    src[0:16] = np.arange(0, 16)
    src[16:32] = np.arange(32, 48)
    src[64:80] = np.arange(16, 32)
    src[80:96] = np.arange(48, 64)
    return src


def _pad_rope_cols(w):
    src = _rope_tile_perm()
    cols = jnp.take(w, jnp.asarray(np.maximum(src, 0)), axis=-1)
    return jnp.where(jnp.asarray(src >= 0), cols, 0.0)


def _mla_tables(n_ctx, n_lat):
    half = MLA_ROPE // 2
    inv = ROPE_BASE ** (-jnp.arange(0, half, 2, dtype=F32) / half)
    pos = jnp.arange(n_lat)
    ang_r = (pos // GRID_W).astype(F32)[:, None] * inv[None, :]
    ang_c = (pos % GRID_W).astype(F32)[:, None] * inv[None, :]
    z = jnp.zeros((n_lat, 32), F32)
    cos = jnp.concatenate([jnp.cos(ang_r), jnp.cos(ang_c), z, jnp.cos(ang_r), jnp.cos(ang_c), z], axis=-1)
    sin = jnp.concatenate([-jnp.sin(ang_r), -jnp.sin(ang_c), z, jnp.sin(ang_r), jnp.sin(ang_c), z], axis=-1)
    cos = jnp.concatenate([jnp.ones((n_ctx, LANES), F32), cos], axis=0)
    sin = jnp.concatenate([jnp.zeros((n_ctx, LANES), F32), sin], axis=0)
    return cos, sin


def _mla_proj_kernel(r_ref, cos_ref, sin_ref, mod_ref, g_ref, win_ref, qn_ref, kvn_ref, wuq_ref, wukv_ref,
                     qg_ref, kg_ref, q_ref, k_ref, v_ref):
    xn = _pre_norm(r_ref[...], g_ref[...], mod_ref, 1).astype(BF16)
    p = _dot(xn, win_ref[...])
    cos = cos_ref[...]
    sin = sin_ref[...]

    def rotate(x):
        return x * cos + pltpu.roll(x, LANES // 2, 1) * sin

    cq = (_rms(p[:, :MLA_Q_RANK], MLA_Q_RANK) * qn_ref[...]).astype(BF16)
    ckv = (_rms(p[:, MLA_Q_RANK:MLA_Q_RANK + MLA_KV_RANK], MLA_KV_RANK) * kvn_ref[...]).astype(BF16)
    kr = rotate(_rms(p[:, MLA_Q_RANK + MLA_KV_RANK:], MLA_ROPE) * kg_ref[:, LANES:]).astype(BF16)
    q = _dot(cq, wuq_ref[...])
    kv = _dot(ckv, wukv_ref[...])
    scale = (MLA_NOPE + MLA_ROPE) ** -0.5 * math.log2(math.e)
    ii = lax.broadcasted_iota(jnp.int32, (2 * LANES, 2 * LANES), 0)
    jj = lax.broadcasted_iota(jnp.int32, (2 * LANES, 2 * LANES), 1)
    same = (ii // LANES) == (jj // LANES)
    mean_q = jnp.where(same, jnp.where(ii < LANES, 1.0 / MLA_NOPE, 1.0 / MLA_ROPE), 0.0).astype(BF16)
    mean_k = jnp.where(same, 1.0 / MLA_NOPE, 0.0).astype(BF16)
    q_gain = qg_ref[...] * scale
    for h in range(MLA_HEADS):
        a0 = h * MLA_HEAD_PAD
        qh = q[:, a0:a0 + 2 * LANES]
        qn = qh * lax.rsqrt(_dot((qh * qh).astype(BF16), mean_q) + EPS) * q_gain
        q_ref[:, a0:a0 + LANES] = qn[:, :LANES].astype(BF16)
        q_ref[:, a0 + LANES:a0 + 2 * LANES] = rotate(qn[:, LANES:]).astype(BF16)
        k_ref[:, a0 + LANES:a0 + 2 * LANES] = kr
    for hh in range(MLA_HEADS // 2):
        kh = kv[:, 2 * hh * LANES:(2 * hh + 2) * LANES]
        kn = kh * lax.rsqrt(_dot((kh * kh).astype(BF16), mean_k) + EPS)
        for e in range(2):
            a0 = (2 * hh + e) * MLA_HEAD_PAD
            k_ref[:, a0:a0 + LANES] = (kn[:, e * LANES:(e + 1) * LANES] * kg_ref[:, :LANES]).astype(BF16)
    v_ref[...] = kv[:, MLA_HEADS * LANES:].astype(BF16)


def _mla_weights(w_in, q_norm, kv_norm, w_uq, w_ukv, q_gain, k_gain):
    lat = MLA_Q_RANK + MLA_KV_RANK
    w_in_p = jnp.concatenate([w_in[:, :lat], _pad_rope_cols(w_in[:, lat:])], axis=-1).astype(BF16)
    uq = w_uq.reshape(MLA_Q_RANK, MLA_HEADS, MLA_NOPE + MLA_ROPE)
    uq = jnp.concatenate([uq[..., :MLA_NOPE], _pad_rope_cols(uq[..., MLA_NOPE:])], axis=-1)
    uq = uq.reshape(MLA_Q_RANK, MLA_HEADS * MLA_HEAD_PAD).astype(BF16)
    ukv = w_ukv.reshape(MLA_KV_RANK, MLA_HEADS, MLA_NOPE + MLA_V)
    ukv = jnp.concatenate([ukv[..., :MLA_NOPE].reshape(MLA_KV_RANK, -1),
                           ukv[..., MLA_NOPE:].reshape(MLA_KV_RANK, -1)], axis=-1).astype(BF16)
    qg = jnp.concatenate([q_gain[:MLA_NOPE], _pad_rope_cols(q_gain[MLA_NOPE:])]).reshape(1, -1)
    kg = jnp.concatenate([k_gain[:MLA_NOPE], _pad_rope_cols(k_gain[MLA_NOPE:])]).reshape(1, -1)
    return [w_in_p, q_norm.reshape(1, -1), kv_norm.reshape(1, -1), uq, ukv, qg, kg]


def _mla_proj(r, cos, sin, mod, gain, weights, n_ctx_tiles):
    b, t, d = r.shape
    cos_b = jnp.broadcast_to(cos[None], (1,) + cos.shape)
    sin_b = jnp.broadcast_to(sin[None], (1,) + sin.shape)
    tt = TOKEN_TILE
    hw = MLA_HEADS * MLA_HEAD_PAD
    consts = [gain.reshape(1, d)] + weights
    in_specs = [pl.BlockSpec((None, tt, d), lambda i, j: (i, j, 0)),
                pl.BlockSpec((None, tt, LANES), lambda i, j: (0, j, 0)),
                pl.BlockSpec((None, tt, LANES), lambda i, j: (0, j, 0)),
                pl.BlockSpec((None, None, N_MOD, d),
                             lambda i, j: (i, (j >= n_ctx_tiles).astype(jnp.int32), 0, 0))]
    in_specs += [_resident(a.shape) for a in consts]
    widths = [(hw, BF16), (hw, BF16), (MLA_HEADS * MLA_V, BF16)]
    return pl.pallas_call(
        _mla_proj_kernel,
        grid=(b, t // tt),
        in_specs=in_specs,
        out_specs=[pl.BlockSpec((None, tt, w), lambda i, j: (i, j, 0)) for w, _ in widths],
        out_shape=[jax.ShapeDtypeStruct((b, t, w), dt) for w, dt in widths],
        compiler_params=_cparams("arbitrary", "arbitrary"),
        name="mla_projections",
    )(r, cos_b, sin_b, mod, *consts)


def _attn_kernel(q_ref, k_ref, v_ref, o_ref, *, n_ctx_tiles, n_ctx):
    def attend(n_keys):
        for h in range(ATTN_HEADS_PER_STEP):
            qk = slice(h * MLA_HEAD_PAD, (h + 1) * MLA_HEAD_PAD)
            vo = slice(h * MLA_V, (h + 1) * MLA_V)
            s = _dot_nt(q_ref[:, qk], k_ref[:n_keys, qk])
            p = jnp.exp2(s - jnp.max(s, axis=-1, keepdims=True))
            l = jnp.sum(p, axis=-1, keepdims=True)
            o = _dot(p.astype(BF16), v_ref[:n_keys, vo])
            o_ref[:, vo] = (o / l).astype(o_ref.dtype)

    is_ctx = pl.program_id(2) < n_ctx_tiles
    pl.when(is_ctx)(lambda: attend(n_ctx))
    pl.when(jnp.logical_not(is_ctx))(lambda: attend(k_ref.shape[0]))


def _attention(q, k, v, n_ctx):
    b, t, _ = q.shape
    tq = TOKEN_TILE
    hp = ATTN_HEADS_PER_STEP
    kern = functools.partial(_attn_kernel, n_ctx_tiles=n_ctx // tq, n_ctx=n_ctx)
    return pl.pallas_call(
        kern,
        grid=(b, MLA_HEADS // hp, t // tq),
        in_specs=[pl.BlockSpec((None, tq, hp * MLA_HEAD_PAD), lambda i, h, j: (i, j, h)),
                  pl.BlockSpec((None, t, hp * MLA_HEAD_PAD), lambda i, h, j: (i, 0, h)),
                  pl.BlockSpec((None, t, hp * MLA_V), lambda i, h, j: (i, 0, h))],
        out_specs=pl.BlockSpec((None, tq, hp * MLA_V), lambda i, h, j: (i, j, h)),
        out_shape=jax.ShapeDtypeStruct((b, t, MLA_HEADS * MLA_V), BF16),
        compiler_params=_cparams("arbitrary", "arbitrary", "arbitrary"),
        name="mla_attention",
    )(q, k, v)


def _ret_tables(n_ctx, n_lat):
    inv = ROPE_BASE ** (-jnp.arange(0, RET_DK, 2, dtype=F32) / RET_DK)
    ang = jnp.arange(n_lat, dtype=F32)[:, None] * inv[None, :]
    cos = jnp.concatenate([jnp.ones((n_ctx, LANES), F32), jnp.cos(ang)], axis=0)
    sin = jnp.concatenate([jnp.zeros((n_ctx, LANES), F32), jnp.sin(ang)], axis=0)
    return cos, sin


def _ret_proj_kernel(r_ref, cos_ref, sin_ref, mod_ref, g_ref, win_ref, q_ref, k_ref, v_ref, sg_ref):
    xn = _pre_norm(r_ref[...], g_ref[...], mod_ref, 1).astype(BF16)
    cos = cos_ref[...]
    sin = sin_ref[...]
    hdk = RET_HEADS * RET_DK
    hdv = RET_HEADS * RET_DV
    scale = RET_DK ** -0.5
    for out_ref, base, sc in ((q_ref, 0, None), (k_ref, hdk, scale)):
        qk = _dot(xn, win_ref[:, base:base + hdk])
        for h in range(RET_HEADS):
            c0 = h * RET_DK
            x1 = qk[:, c0:c0 + LANES]
            x2 = qk[:, c0 + LANES:c0 + 2 * LANES]
            o1 = x1 * cos - x2 * sin
            o2 = x2 * cos + x1 * sin
            if sc is not None:
                o1 = o1 * sc
                o2 = o2 * sc
            out_ref[:, c0:c0 + LANES] = o1.astype(BF16)
            out_ref[:, c0 + LANES:c0 + 2 * LANES] = o2.astype(BF16)
    v_ref[...] = _dot(xn, win_ref[:, 2 * hdk:2 * hdk + hdv]).astype(BF16)
    sg_ref[...] = _silu(_dot(xn, win_ref[:, 2 * hdk + hdv:])).astype(BF16)


def _ret_proj(r, cos, sin, mod, gain, w_in, n_ctx_tiles):
    b, t, d = r.shape
    tt = TOKEN_TILE
    consts = [gain.reshape(1, d), w_in]
    in_specs = [pl.BlockSpec((None, tt, d), lambda i, j: (i, j, 0)),
                pl.BlockSpec((tt, LANES), lambda i, j: (j, 0)),
                pl.BlockSpec((tt, LANES), lambda i, j: (j, 0)),
                pl.BlockSpec((None, None, N_MOD, d),
                             lambda i, j: (i, (j >= n_ctx_tiles).astype(jnp.int32), 0, 0))]
    in_specs += [_resident(a.shape) for a in consts]
    widths = [(RET_HEADS * RET_DK, BF16), (RET_HEADS * RET_DK, BF16),
              (RET_HEADS * RET_DV, BF16), (RET_HEADS * RET_DV, BF16)]
    return pl.pallas_call(
        _ret_proj_kernel,
        grid=(b, t // tt),
        in_specs=in_specs,
        out_specs=[pl.BlockSpec((None, tt, w), lambda i, j: (i, j, 0)) for w, _ in widths],
        out_shape=[jax.ShapeDtypeStruct((b, t, w), dt) for w, dt in widths],
        compiler_params=_cparams("arbitrary", "arbitrary"),
        name="retention_projections",
    )(r, cos, sin, mod, *consts)


def _ret_scan_kernel(l1g_ref, q_ref, k_ref, v_ref, sg_ref, gain_ref, y_ref, ob_scr, st_scr, *, n_ctx, n_all):
    c = SCAN_CHUNK
    log_g = jnp.log1p(-jnp.exp(l1g_ref[...]))
    lgf = log_g[0:1, 0:1]
    lgb = log_g[1:2, 0:1]
    ii = lax.broadcasted_iota(jnp.int32, (c, c), 0)
    jj = lax.broadcasted_iota(jnp.int32, (c, c), 1)
    diff = (ii - jj).astype(F32)
    mask = jnp.where(diff >= 0.0, jnp.exp(lgf * jnp.maximum(diff, 0.0)), jnp.exp(lgb * jnp.maximum(-diff, 0.0)))
    ic = lax.broadcasted_iota(jnp.int32, (c, 1), 0).astype(F32)
    qdec_f = jnp.exp(lgf * (ic + 1.0))
    qdec_b = jnp.exp(lgb * (c - ic))
    kdec_f = jnp.exp(lgf * (c - 1.0 - ic))
    kdec_b = jnp.exp(lgb * ic)
    cdec_f = jnp.exp(lgf * c)
    cdec_b = jnp.exp(lgb * c)

    def kv_outer(kc, dec, vc):
        return _dot((kc.astype(F32) * dec).T.astype(BF16), vc)

    st_scr[...] = jnp.zeros_like(st_scr)
    for n in list(range(n_ctx - 1, -1, -1)) + list(range(n_all - 1, n_ctx - 1, -1)):
        rows = slice(n * c, (n + 1) * c)
        state = st_scr[...]
        ob_scr[rows, :] = _dot((q_ref[rows, :].astype(F32) * qdec_b).astype(BF16), state.astype(BF16))
        st_scr[...] = state * cdec_b + kv_outer(k_ref[rows, :], kdec_b, v_ref[rows, :])

    st_scr[...] = jnp.zeros_like(st_scr)
    gain = gain_ref[...]
    for n in range(n_all):
        rows = slice(n * c, (n + 1) * c)
        qc = q_ref[rows, :]
        kc = k_ref[rows, :]
        vc = v_ref[rows, :]
        state = st_scr[...]
        scores = _dot_nt(qc, kc) * mask
        o = (_dot(scores.astype(BF16), vc)
             + _dot((qc.astype(F32) * qdec_f).astype(BF16), state.astype(BF16))
             + ob_scr[rows, :])
        mu = jnp.mean(o, axis=-1, keepdims=True)
        dev = o - mu
        on = dev * lax.rsqrt(jnp.mean(dev * dev, axis=-1, keepdims=True) + EPS)
        y_ref[rows, :] = (on * gain * sg_ref[rows, :].astype(F32)).astype(y_ref.dtype)
        st_scr[...] = state * cdec_f + kv_outer(kc, kdec_f, vc)


def _ret_scan(q, k, v, sg, log1m_gamma, gn_gain, n_ctx):
    b, t, _ = q.shape
    l1g = jnp.broadcast_to(log1m_gamma.T[:, :, None], (RET_HEADS, 2, LANES))
    l1g = jnp.concatenate([l1g, jnp.zeros((RET_HEADS, 6, LANES), F32) - 1.0], axis=1)
    kern = functools.partial(_ret_scan_kernel, n_ctx=n_ctx // SCAN_CHUNK, n_all=t // SCAN_CHUNK)
    return pl.pallas_call(
        kern,
        grid=(b, RET_HEADS),
        in_specs=[pl.BlockSpec((None, 8, LANES), lambda i, h: (h, 0, 0)),
                  pl.BlockSpec((None, t, RET_DK), lambda i, h: (i, 0, h)),
                  pl.BlockSpec((None, t, RET_DK), lambda i, h: (i, 0, h)),
                  pl.BlockSpec((None, t, RET_DV), lambda i, h: (i, 0, h)),
                  pl.BlockSpec((None, t, RET_DV), lambda i, h: (i, 0, h)),
                  pl.BlockSpec((1, RET_DV), lambda i, h: (0, h))],
        out_specs=pl.BlockSpec((None, t, RET_DV), lambda i, h: (i, 0, h)),
        out_shape=jax.ShapeDtypeStruct((b, t, RET_HEADS * RET_DV), BF16),
        scratch_shapes=[pltpu.VMEM((t, RET_DV), F32), pltpu.VMEM((RET_DK, RET_DV), F32)],
        compiler_params=_cparams("arbitrary", "arbitrary"),
        name="retention_scan",
    )(l1g, q, k, v, sg, gn_gain.reshape(1, -1))


CHUNKS_PER_TILE = TOKEN_TILE // SCAN_CHUNK


def _chunk_major_spec(g, tile0=0):
    return pl.BlockSpec((g, CHUNKS_PER_TILE, None, S5_GROUP, SCAN_CHUNK), lambda i, j: (0, j + tile0, i, 0, 0))


def _norm_t_kernel(r_ref, mod_ref, g_ref, o_ref):
    xn = _pre_norm(r_ref[...], g_ref[...], mod_ref, 1)
    c = SCAN_CHUNK
    for ci in range(CHUNKS_PER_TILE):
        o_ref[:, ci] = xn[ci * c:(ci + 1) * c, :].T.reshape(o_ref.shape[0], S5_GROUP, c)


def _norm_chunk_major(r, mod, gain, n_ctx_tiles):
    b, t, d = r.shape
    g = d // S5_GROUP
    return pl.pallas_call(
        _norm_t_kernel,
        grid=(b, t // TOKEN_TILE),
        in_specs=[_rows_spec(d), _mod_spec(d, n_ctx_tiles), _resident((1, d))],
        out_specs=_chunk_major_spec(g),
        out_shape=jax.ShapeDtypeStruct((g, t // SCAN_CHUNK, b, S5_GROUP, SCAN_CHUNK), F32),
        compiler_params=_cparams("arbitrary", "arbitrary"),
        name="s5_pre_norm",
    )(r, mod, gain.reshape(1, d))


def _s5_disc(lam_re, lam_im, log_dt):
    dt = jnp.exp(log_dt)
    a_re = lam_re * dt
    a_im = lam_im * dt
    mag = jnp.exp(a_re)
    e_re = mag * jnp.cos(a_im) - 1.0
    e_im = mag * jnp.sin(a_im)
    den = lam_re * lam_re + lam_im * lam_im
    return a_re, a_im, (e_re * lam_re + e_im * lam_im) / den, (e_im * lam_re - e_re * lam_im) / den


def _cmul(ar, ai, br, bi):
    return ar * br - ai * bi, ar * bi + ai * br


def _s5_kernel(u_ref, lamc_ref, lamr_ref, lamr2_ref, bc_ref, br_ref, ct_ref, d_ref,
               y_ref, t_scr, x_scr, w_scr, *, batch, n_ctx, n_all):
    c, m, p = SCAN_CHUNK, S5_GROUP, S5_STATE
    cm = c * m
    rows = n_all * batch
    for mi in range(m):
        x_scr[:, mi * c:(mi + 1) * c] = u_ref[pl.ds(mi, rows, stride=m), :].astype(BF16)

    kk = lax.broadcasted_iota(jnp.int32, (1, LANES), 1).astype(F32)

    def powers(a_re, a_im, expo):
        mag = jnp.exp(a_re * expo)
        return mag * jnp.cos(a_im * expo), mag * jnp.sin(a_im * expo)

    def lane_tiles(fn):
        parts = [fn(i) for i in range(m)]
        return (jnp.concatenate([q[0] for q in parts], axis=1), jnp.concatenate([q[1] for q in parts], axis=1))

    disc = [_s5_disc(lamc_ref[d][:, 0:1], lamc_ref[d][:, 1:2], lamc_ref[d][:, 2:3]) for d in range(2)]
    bbar = [_cmul(disc[d][2], disc[d][3], bc_ref[d, 0], bc_ref[d, 1]) for d in range(2)]
    ctr = [(ct_ref[d, 0], ct_ref[d, 1]) for d in range(2)]

    def col(pair, i):
        return pair[0][:, i:i + 1], pair[1][:, i:i + 1]

    f_up = powers(disc[0][0], disc[0][1], kk + 1.0)
    f_down = powers(disc[0][0], disc[0][1], (c - 1.0) - kk)
    b_up = powers(disc[1][0], disc[1][1], kk)
    b_down = powers(disc[1][0], disc[1][1], (c - 1.0) - kk)
    b_down1 = powers(disc[1][0], disc[1][1], c - kk)
    lag0 = jnp.where(kk == c - 1.0, 1.0, 0.0)

    q0f = lane_tiles(lambda mo: (col(ctr[0], mo)[0] * lag0, col(ctr[0], mo)[1] * lag0))
    q1f = lane_tiles(lambda mo: _cmul(*col(ctr[0], mo), *f_up))
    q0b = lane_tiles(lambda mo: _cmul(*col(ctr[1], mo), *b_down))
    qrb = lane_tiles(lambda mo: _cmul(*col(ctr[1], mo), *b_down1))

    hp = lax.Precision.HIGHEST
    btr = []
    for d in range(2):
        lam = lamr_ref[d]
        _, _, cf_re, cf_im = _s5_disc(lam[0:1, :], lam[1:2, :], lam[2:3, :])
        btr.append(_cmul(cf_re, cf_im, br_ref[d, 0], br_ref[d, 1]))

    def gen(bt, q):
        return (jnp.dot(bt[0], q[0], precision=hp, preferred_element_type=F32)
                - jnp.dot(bt[1], q[1], precision=hp, preferred_element_type=F32))

    w0 = gen(btr[0], q0f) + gen(btr[1], q0b)
    w1 = gen(btr[0], q1f)
    for mo in range(m):
        w_scr[:, 2 * mo * c:(2 * mo + 1) * c] = w0[:, mo * c:(mo + 1) * c]
        w_scr[:, (2 * mo + 1) * c:(2 * mo + 2) * c] = w1[:, mo * c:(mo + 1) * c]

    def fill(mi, carry):
        row = jnp.broadcast_to(w_scr[pl.ds(mi, 1), :], (c, 2 * cm))
        win = pltpu.roll(row, 2 * cm - (c - 1), 1, stride=1, stride_axis=0)
        blk = jnp.concatenate([win[:, 2 * mo * c:(2 * mo + 1) * c] for mo in range(m)], axis=1)
        t_scr[pl.ds(pl.multiple_of(mi * c, c), c), :] = blk.astype(BF16)
        return carry

    lax.fori_loop(0, m, fill, 0)

    for d, q in ((0, q1f), (1, qrb)):
        base = cm + 2 * p * d
        t_scr[base:base + p, :] = q[0].astype(BF16)
        t_scr[base + p:base + 2 * p, :] = (-q[1]).astype(BF16)

    wb_f = lane_tiles(lambda mi: _cmul(*col(bbar[0], mi), *f_down))
    wb_b = lane_tiles(lambda mi: _cmul(*col(bbar[1], mi), *b_up))
    wb = jnp.concatenate([wb_f[0], wb_f[1], wb_b[0], wb_b[1]], axis=0).astype(BF16)
    hloc = _dot_nt(x_scr[:, :cm], wb)

    sign = jnp.where(lax.broadcasted_iota(jnp.int32, (1, 2 * p), 1) < p, -1.0, 1.0)
    order_f = list(range(n_all))
    order_b = list(range(n_ctx - 1, -1, -1)) + list(range(n_all - 1, n_ctx - 1, -1))
    for d, order in ((0, order_f), (1, order_b)):
        lam = lamr2_ref[d]
        a_re, a_im, _, _ = _s5_disc(lam[0:1, :], lam[1:2, :], lam[2:3, :])
        mag = jnp.exp(a_re * c)
        ac_r = mag * jnp.cos(a_im * c)
        ac_i = mag * jnp.sin(a_im * c) * sign
        state = jnp.zeros((batch, 2 * p), F32)
        colx = cm + 2 * p * d
        for n in order:
            x_scr[n * batch:(n + 1) * batch, colx:colx + 2 * p] = state.astype(BF16)
            local = hloc[n * batch:(n + 1) * batch, 2 * p * d:2 * p * (d + 1)]
            state = state * ac_r + pltpu.roll(state, p, 1) * ac_i + local

    y = _dot(x_scr[...], t_scr[...])
    for mo in range(m):
        skip = d_ref[mo:mo + 1, :] * u_ref[pl.ds(mo, rows, stride=m), :]
        y_ref[pl.ds(mo, rows, stride=m), :] = y[:, mo * c:(mo + 1) * c] + skip


def _s5_operands(u, lam_re, lam_im, log_dt, b_re, b_im, c_re, c_im, d_skip):
    g, n_all, batch, m, c = u.shape
    dtb = jnp.broadcast_to(log_dt[:, :, None], lam_re.shape)
    lam3 = jnp.stack([lam_re, lam_im, dtb], axis=-1)
    lamc = jnp.pad(lam3, ((0, 0), (0, 0), (0, 0), (0, LANES - 3))).transpose(1, 0, 2, 3)
    lam3r = jnp.stack([lam_re, lam_im, dtb], axis=2)
    lamr = jnp.pad(lam3r, ((0, 0), (0, 0), (0, 5), (0, 0))).transpose(1, 0, 2, 3)
    lamr2 = jnp.concatenate([lamr, lamr], axis=-1)
    bcol = jnp.stack([b_re, b_im], axis=1)
    bc = jnp.pad(bcol, ((0, 0),) * 4 + ((0, LANES - m),)).transpose(2, 0, 1, 3, 4)
    brow = jnp.swapaxes(bcol, -1, -2).transpose(2, 0, 1, 3, 4)
    ctc = jnp.swapaxes(jnp.stack([c_re, c_im], axis=1), -1, -2)
    ctc = jnp.pad(ctc, ((0, 0),) * 4 + ((0, LANES - m),)).transpose(2, 0, 1, 3, 4)
    dvec = jnp.broadcast_to(d_skip.reshape(g, m, 1), (g, m, LANES))
    return (u.reshape(g, n_all * batch * m, c), lamc, lamr, lamr2, bc, brow, ctc, dvec)


def _s5_mixer(r, mod, gain, lam_re, lam_im, log_dt, b_re, b_im, c_re, c_im, d_skip, n_ctx, n_ctx_tiles):
    b = r.shape[0]
    u = _norm_chunk_major(r, mod, gain, n_ctx_tiles)
    g, n_all, _, m, c = u.shape
    p = S5_STATE
    rows = n_all * b
    cm = c * m
    kern = functools.partial(_s5_kernel, batch=b, n_ctx=n_ctx // c, n_all=n_all)
    per_g = lambda *tail: pl.BlockSpec((None,) + tail, lambda i: (i,) + (0,) * len(tail))
    y = pl.pallas_call(
        kern,
        grid=(g,),
        in_specs=[per_g(rows * m, c), per_g(2, p, LANES), per_g(2, 8, p), per_g(2, 8, 2 * p),
                  per_g(2, 2, p, LANES), per_g(2, 2, m, p), per_g(2, 2, p, LANES), per_g(m, LANES)],
        out_specs=per_g(rows * m, c),
        out_shape=jax.ShapeDtypeStruct((g, rows * m, c), F32),
        scratch_shapes=[pltpu.VMEM((cm + 4 * p, cm), BF16), pltpu.VMEM((rows, cm + 4 * p), BF16),
                        pltpu.VMEM((m, 2 * cm), F32)],
        compiler_params=_cparams("arbitrary"),
        name="s5_scan",
    )(*_s5_operands(u, lam_re, lam_im, log_dt, b_re, b_im, c_re, c_im, d_skip))
    return y.reshape(u.shape)


def kernel(x, c, ctx, c_ctx, ada_w, ada_b, norm_g, ffn_w_in, ffn_w_out, mla_w_in, mla_q_norm, mla_kv_norm,
           mla_w_uq, mla_w_ukv, mla_q_gain, mla_k_gain, mla_w_o, ret_w_in, ret_log1m_gamma, ret_gn_gain, ret_w_o,
           s5_lam_re, s5_lam_im, s5_log_dt, s5_b_re, s5_b_im, s5_c_re, s5_c_im, s5_d, s5_glu_w, s5_glu_b):
    b, s, d = x.shape
    n_ctx = ctx.shape[1]
    depth = ada_w.shape[0]
    assert n_ctx % TOKEN_TILE == 0 and s % TOKEN_TILE == 0 and s % GRID_W == 0
    n_ctx_tiles = n_ctx // TOKEN_TILE

    pad = (-(b + 1)) % 8
    cvec = jnp.concatenate([c, c_ctx[None, :], jnp.zeros((pad, d), F32)], axis=0)
    mod_all = _modulation(cvec, ada_w, ada_b)
    mod_lat = mod_all[:, :b].reshape(depth, b, 1, N_MOD, d)
    mod_ctx = jnp.broadcast_to(mod_all[:, b].reshape(depth, 1, 1, N_MOD, d), (depth, b, 1, N_MOD, d))
    mods = jnp.concatenate([mod_ctx, mod_lat], axis=2)

    mla_tabs = _mla_tables(n_ctx, s)
    ret_tabs = _ret_tables(n_ctx, s)

    r = None
    for i in range(depth):
        kind, j = i % N_MIXERS, i // N_MIXERS
        mod = mods[i]
        ffn0 = (norm_g[i, 0], ffn_w_in[i, 0].astype(BF16), ffn_w_out[i, 0].astype(BF16))
        ffn1 = (norm_g[i, 2], ffn_w_in[i, 1].astype(BF16), ffn_w_out[i, 1].astype(BF16))
        r = _ffn_first(ctx, x, mod, *ffn0, n_ctx_tiles) if i == 0 else _ffn(r, mod, *ffn0, n_ctx_tiles)
        tile0 = n_ctx_tiles if i == depth - 1 else 0
        if kind == 0:
            weights = _mla_weights(mla_w_in[j], mla_q_norm[j], mla_kv_norm[j], mla_w_uq[j], mla_w_ukv[j],
                                   mla_q_gain[j], mla_k_gain[j])
            q, k, v = _mla_proj(r, *mla_tabs, mod, norm_g[i, 1], weights, n_ctx_tiles)
            y = _attention(q, k, v, n_ctx)
            mix = (y, _rows_spec(y.shape[-1], tile0), [mla_w_o[j].astype(BF16)], _ffn_proj_kernel, "mla_out_swiglu")
        elif kind == 1:
            q, k, v, sg = _ret_proj(r, *ret_tabs, mod, norm_g[i, 1], ret_w_in[j].astype(BF16), n_ctx_tiles)
            y = _ret_scan(q, k, v, sg, ret_log1m_gamma[j], ret_gn_gain[j], n_ctx)
            mix = (y, _rows_spec(y.shape[-1], tile0), [ret_w_o[j].astype(BF16)], _ffn_proj_kernel, "ret_out_swiglu")
        else:
            y = _s5_mixer(r, mod, norm_g[i, 1], s5_lam_re[j], s5_lam_im[j], s5_log_dt[j], s5_b_re[j], s5_b_im[j],
                          s5_c_re[j], s5_c_im[j], s5_d[j], n_ctx, n_ctx_tiles)
            mix = (y, _chunk_major_spec(y.shape[0], tile0), [s5_glu_w[j].astype(BF16), s5_glu_b[j].reshape(1, -1)],
                   _ffn_glu_kernel, "s5_out_swiglu")
        r = _ffn_after_mixer(r, *mix, mod, *ffn1, n_ctx_tiles, tile0)
    return r
```

```python
import functools
import math

import jax
import jax.numpy as jnp
import numpy as np
from jax import lax
from jax.experimental import pallas as pl
from jax.experimental.pallas import tpu as pltpu

F32 = jnp.float32
BF16 = jnp.bfloat16

EPS = 1e-6
ROPE_BASE = 10000.0
GRID_W = 64
N_MIXERS = 3
N_MOD = 9

MLA_HEADS = 8
MLA_NOPE = 128
MLA_ROPE = 64
MLA_V = 128
MLA_Q_RANK = 384
MLA_KV_RANK = 256
MLA_HEAD_PAD = 256

RET_HEADS = 4
RET_DK = 256
RET_DV = 512
SCAN_CHUNK = 128

S5_GROUP = 16
S5_STATE = 64

LANES = 128
TOKEN_TILE = 256
ATTN_HEADS_PER_STEP = 4
VMEM_LIMIT_BYTES = 56 * 1024 * 1024


def _cparams(*sem):
    return pltpu.CompilerParams(dimension_semantics=sem, vmem_limit_bytes=VMEM_LIMIT_BYTES)


def _resident(shape):
    zeros = (0,) * len(shape)
    return pl.BlockSpec(shape, lambda *_: zeros)


def _silu(x):
    return x * jax.nn.sigmoid(x)


def _rms(x, n):
    return x * lax.rsqrt(jnp.sum(x * x, axis=-1, keepdims=True) * (1.0 / n) + EPS)


def _pre_norm(x, gain, mod_ref, k):
    y = _rms(x, x.shape[-1]) * gain
    return y * (1.0 + mod_ref[3 * k + 1:3 * k + 2, :]) + mod_ref[3 * k:3 * k + 1, :]


def _dot(a, b):
    return jnp.dot(a, b, preferred_element_type=F32)


def _dot_nt(a, b):
    return lax.dot_general(a, b, (((1,), (1,)), ((), ())), preferred_element_type=F32)


def _mod_kernel(cv_ref, w_ref, b_ref, o_ref):
    s = _silu(cv_ref[...]).astype(BF16)
    o_ref[...] = _dot(s, w_ref[...].astype(BF16)) + b_ref[...]


def _modulation(cvec, ada_w, ada_b):
    depth, d, n = ada_w.shape
    rows = cvec.shape[0]
    tn = n // 4
    return pl.pallas_call(
        _mod_kernel,
        grid=(depth, n // tn),
        in_specs=[
            _resident((rows, d)),
            pl.BlockSpec((None, d, tn), lambda i, j: (i, 0, j)),
            pl.BlockSpec((None, 1, tn), lambda i, j: (i, 0, j)),
        ],
        out_specs=pl.BlockSpec((None, rows, tn), lambda i, j: (i, 0, j)),
        out_shape=jax.ShapeDtypeStruct((depth, rows, n), F32),
        compiler_params=_cparams("arbitrary", "arbitrary"),
        name="adaln_modulation",
    )(cvec, ada_w, ada_b.reshape(depth, 1, n))


def _rows_spec(width, tile0=0):
    return pl.BlockSpec((None, TOKEN_TILE, width), lambda i, j: (i, j + tile0, 0))


def _mod_spec(d, n_ctx_tiles, tile0=0):
    return pl.BlockSpec((None, None, N_MOD, d),
                        lambda i, j: (i, (j + tile0 >= n_ctx_tiles).astype(jnp.int32), 0, 0))


def _row_call(kernel, name, b, n_tiles, inputs, consts, outs):
    in_specs = [spec for _, spec in inputs] + [_resident(a.shape) for a in consts]
    return pl.pallas_call(
        kernel,
        grid=(b, n_tiles),
        in_specs=in_specs,
        out_specs=[_rows_spec(w) for w, _ in outs],
        out_shape=[jax.ShapeDtypeStruct((b, n_tiles * TOKEN_TILE, w), dt) for w, dt in outs],
        compiler_params=_cparams("arbitrary", "arbitrary"),
        name=name,
    )(*[a for a, _ in inputs], *consts)


def _gelu_tanh(x):
    cdf = 0.5 * (1.0 + jnp.tanh(math.sqrt(2.0 / math.pi) * (x + 0.044715 * (x * x * x))))
    return x * cdf


def _swiglu_step(x, mod_ref, g_ref, win_ref, wout_ref, k):
    xn = _pre_norm(x, g_ref[...], mod_ref, k).astype(BF16)
    f = wout_ref.shape[0]
    a = _dot(xn, win_ref[:, :f])
    b = _dot(xn, win_ref[:, f:])
    hm = (_silu(a) * b).astype(BF16)
    return x + (0.5 * mod_ref[3 * k + 2:3 * k + 3, :]) * _dot(hm, wout_ref[...])


def _ffn_first_kernel(ctx_ref, x_ref, mod_ref, g_ref, win_ref, wout_ref, o_ref, *, n_ctx_tiles):
    x = jnp.where(pl.program_id(1) < n_ctx_tiles, ctx_ref[...], x_ref[...])
    o_ref[...] = _swiglu_step(x, mod_ref, g_ref, win_ref, wout_ref, 0)


def _ffn_kernel(r_ref, mod_ref, g_ref, win_ref, wout_ref, o_ref):
    o_ref[...] = _swiglu_step(r_ref[...], mod_ref, g_ref, win_ref, wout_ref, 0)


def _ffn_proj_kernel(r_ref, y_ref, mod_ref, pw_ref, g_ref, win_ref, wout_ref, o_ref):
    x = r_ref[...] + mod_ref[5:6, :] * _dot(y_ref[...], pw_ref[...])
    o_ref[...] = _swiglu_step(x, mod_ref, g_ref, win_ref, wout_ref, 2)


def _ffn_glu_kernel(r_ref, y_ref, mod_ref, pw_ref, pb_ref, g_ref, win_ref, wout_ref, o_ref):
    d = r_ref.shape[-1]
    c = SCAN_CHUNK
    y = jnp.concatenate([y_ref[:, ci].reshape(d, c).T for ci in range(TOKEN_TILE // c)], axis=0)
    ag = _dot(_gelu_tanh(y).astype(BF16), pw_ref[...]) + pb_ref[...]
    x = r_ref[...] + mod_ref[5:6, :] * (ag[:, :d] * jax.nn.sigmoid(ag[:, d:]))
    o_ref[...] = _swiglu_step(x, mod_ref, g_ref, win_ref, wout_ref, 2)


def _ffn_first(ctx, x, mod, gain, w_in, w_out, n_ctx_tiles):
    b, s, d = x.shape
    last_ctx = n_ctx_tiles - 1
    inputs = [(ctx, pl.BlockSpec((None, TOKEN_TILE, d), lambda i, j: (i, jnp.minimum(j, last_ctx), 0))),
              (x, pl.BlockSpec((None, TOKEN_TILE, d), lambda i, j: (i, jnp.maximum(j - n_ctx_tiles, 0), 0))),
              (mod, _mod_spec(d, n_ctx_tiles))]
    (out,) = _row_call(functools.partial(_ffn_first_kernel, n_ctx_tiles=n_ctx_tiles), "swiglu_first",
                       b, n_ctx_tiles + s // TOKEN_TILE, inputs, [gain.reshape(1, d), w_in, w_out], [(d, F32)])
    return out


def _ffn(r, mod, gain, w_in, w_out, n_ctx_tiles):
    b, t, d = r.shape
    inputs = [(r, _rows_spec(d)), (mod, _mod_spec(d, n_ctx_tiles))]
    (out,) = _row_call(_ffn_kernel, "swiglu_half_step", b, t // TOKEN_TILE, inputs,
                       [gain.reshape(1, d), w_in, w_out], [(d, F32)])
    return out


def _ffn_after_mixer(r, y, y_spec, mix_consts, kern, name, mod, gain, w_in, w_out, n_ctx_tiles, tile0):
    b, t, d = r.shape
    inputs = [(r, _rows_spec(d, tile0)), (y, y_spec), (mod, _mod_spec(d, n_ctx_tiles, tile0))]
    (out,) = _row_call(kern, name, b, t // TOKEN_TILE - tile0, inputs,
                       mix_consts + [gain.reshape(1, d), w_in, w_out], [(d, F32)])
    return out


def _rope_tile_perm():
    src = np.full((LANES,), -1, np.int64)
    src[0:16] = np.arange(0, 16)
    src[16:32] = np.arange(32, 48)
    src[64:80] = np.arange(16, 32)
    src[80:96] = np.arange(48, 64)
    return src


def _pad_rope_cols(w):
    src = _rope_tile_perm()
    cols = jnp.take(w, jnp.asarray(np.maximum(src, 0)), axis=-1)
    return jnp.where(jnp.asarray(src >= 0), cols, 0.0)


def _mla_tables(n_ctx, n_lat):
    half = MLA_ROPE // 2
    inv = ROPE_BASE ** (-jnp.arange(0, half, 2, dtype=F32) / half)
    pos = jnp.arange(n_lat)
    ang_r = (pos // GRID_W).astype(F32)[:, None] * inv[None, :]
    ang_c = (pos % GRID_W).astype(F32)[:, None] * inv[None, :]
    z = jnp.zeros((n_lat, 32), F32)
    cos = jnp.concatenate([jnp.cos(ang_r), jnp.cos(ang_c), z, jnp.cos(ang_r), jnp.cos(ang_c), z], axis=-1)
    sin = jnp.concatenate([-jnp.sin(ang_r), -jnp.sin(ang_c), z, jnp.sin(ang_r), jnp.sin(ang_c), z], axis=-1)
    cos = jnp.concatenate([jnp.ones((n_ctx, LANES), F32), cos], axis=0)
    sin = jnp.concatenate([jnp.zeros((n_ctx, LANES), F32), sin], axis=0)
    return cos, sin


def _mla_proj_kernel(r_ref, cos_ref, sin_ref, mod_ref, g_ref, win_ref, qn_ref, kvn_ref, wuq_ref, wukv_ref,
                     qg_ref, kg_ref, q_ref, k_ref, v_ref):
    xn = _pre_norm(r_ref[...], g_ref[...], mod_ref, 1).astype(BF16)
    p = _dot(xn, win_ref[...])
    cos = cos_ref[...]
    sin = sin_ref[...]

    def rotate(x):
        return x * cos + pltpu.roll(x, LANES // 2, 1) * sin

    cq = (_rms(p[:, :MLA_Q_RANK], MLA_Q_RANK) * qn_ref[...]).astype(BF16)
    ckv = (_rms(p[:, MLA_Q_RANK:MLA_Q_RANK + MLA_KV_RANK], MLA_KV_RANK) * kvn_ref[...]).astype(BF16)
    kr = rotate(_rms(p[:, MLA_Q_RANK + MLA_KV_RANK:], MLA_ROPE) * kg_ref[:, LANES:]).astype(BF16)
    q = _dot(cq, wuq_ref[...])
    kv = _dot(ckv, wukv_ref[...])
    scale = (MLA_NOPE + MLA_ROPE) ** -0.5 * math.log2(math.e)
    ii = lax.broadcasted_iota(jnp.int32, (2 * LANES, 2 * LANES), 0)
    jj = lax.broadcasted_iota(jnp.int32, (2 * LANES, 2 * LANES), 1)
    same = (ii // LANES) == (jj // LANES)
    mean_q = jnp.where(same, jnp.where(ii < LANES, 1.0 / MLA_NOPE, 1.0 / MLA_ROPE), 0.0).astype(BF16)
    mean_k = jnp.where(same, 1.0 / MLA_NOPE, 0.0).astype(BF16)
    q_gain = qg_ref[...] * scale
    for h in range(MLA_HEADS):
        a0 = h * MLA_HEAD_PAD
        qh = q[:, a0:a0 + 2 * LANES]
        qn = qh * lax.rsqrt(_dot((qh * qh).astype(BF16), mean_q) + EPS) * q_gain
        q_ref[:, a0:a0 + LANES] = qn[:, :LANES].astype(BF16)
        q_ref[:, a0 + LANES:a0 + 2 * LANES] = rotate(qn[:, LANES:]).astype(BF16)
        k_ref[:, a0 + LANES:a0 + 2 * LANES] = kr
    for hh in range(MLA_HEADS // 2):
        kh = kv[:, 2 * hh * LANES:(2 * hh + 2) * LANES]
        kn = kh * lax.rsqrt(_dot((kh * kh).astype(BF16), mean_k) + EPS)
        for e in range(2):
            a0 = (2 * hh + e) * MLA_HEAD_PAD
            k_ref[:, a0:a0 + LANES] = (kn[:, e * LANES:(e + 1) * LANES] * kg_ref[:, :LANES]).astype(BF16)
    v_ref[...] = kv[:, MLA_HEADS * LANES:].astype(BF16)


def _mla_weights(w_in, q_norm, kv_norm, w_uq, w_ukv, q_gain, k_gain):
    lat = MLA_Q_RANK + MLA_KV_RANK
    w_in_p = jnp.concatenate([w_in[:, :lat], _pad_rope_cols(w_in[:, lat:])], axis=-1).astype(BF16)
    uq = w_uq.reshape(MLA_Q_RANK, MLA_HEADS, MLA_NOPE + MLA_ROPE)
    uq = jnp.concatenate([uq[..., :MLA_NOPE], _pad_rope_cols(uq[..., MLA_NOPE:])], axis=-1)
    uq = uq.reshape(MLA_Q_RANK, MLA_HEADS * MLA_HEAD_PAD).astype(BF16)
    ukv = w_ukv.reshape(MLA_KV_RANK, MLA_HEADS, MLA_NOPE + MLA_V)
    ukv = jnp.concatenate([ukv[..., :MLA_NOPE].reshape(MLA_KV_RANK, -1),
                           ukv[..., MLA_NOPE:].reshape(MLA_KV_RANK, -1)], axis=-1).astype(BF16)
    qg = jnp.concatenate([q_gain[:MLA_NOPE], _pad_rope_cols(q_gain[MLA_NOPE:])]).reshape(1, -1)
    kg = jnp.concatenate([k_gain[:MLA_NOPE], _pad_rope_cols(k_gain[MLA_NOPE:])]).reshape(1, -1)
    return [w_in_p, q_norm.reshape(1, -1), kv_norm.reshape(1, -1), uq, ukv, qg, kg]


def _mla_proj(r, cos, sin, mod, gain, weights, n_ctx_tiles):
    b, t, d = r.shape
    cos_b = jnp.broadcast_to(cos[None], (1,) + cos.shape)
    sin_b = jnp.broadcast_to(sin[None], (1,) + sin.shape)
    tt = TOKEN_TILE
    hw = MLA_HEADS * MLA_HEAD_PAD
    consts = [gain.reshape(1, d)] + weights
    in_specs = [pl.BlockSpec((None, tt, d), lambda i, j: (i, j, 0)),
                pl.BlockSpec((None, tt, LANES), lambda i, j: (0, j, 0)),
                pl.BlockSpec((None, tt, LANES), lambda i, j: (0, j, 0)),
                pl.BlockSpec((None, None, N_MOD, d),
                             lambda i, j: (i, (j >= n_ctx_tiles).astype(jnp.int32), 0, 0))]
    in_specs += [_resident(a.shape) for a in consts]
    widths = [(hw, BF16), (hw, BF16), (MLA_HEADS * MLA_V, BF16)]
    return pl.pallas_call(
        _mla_proj_kernel,
        grid=(b, t // tt),
        in_specs=in_specs,
        out_specs=[pl.BlockSpec((None, tt, w), lambda i, j: (i, j, 0)) for w, _ in widths],
        out_shape=[jax.ShapeDtypeStruct((b, t, w), dt) for w, dt in widths],
        compiler_params=_cparams("arbitrary", "arbitrary"),
        name="mla_projections",
    )(r, cos_b, sin_b, mod, *consts)


ATTN_MIN_ROW_SUM = 2.0 ** -100


def _attn_kernel(bound_ref, q_ref, k_ref, v_ref, o_ref, *, n_ctx_tiles, n_ctx):
    def attend(n_keys, use_bound):
        low = None
        for h in range(ATTN_HEADS_PER_STEP):
            qk = slice(h * MLA_HEAD_PAD, (h + 1) * MLA_HEAD_PAD)
            vo = slice(h * MLA_V, (h + 1) * MLA_V)
            s = _dot_nt(q_ref[:, qk], k_ref[:n_keys, qk])
            shift = bound_ref[0:1, 0:1] if use_bound else jnp.max(s, axis=-1, keepdims=True)
            p = jnp.exp2(s - shift)
            l = jnp.sum(p, axis=-1, keepdims=True)
            o = _dot(p.astype(BF16), v_ref[:n_keys, vo])
            o_ref[:, vo] = (o / l).astype(o_ref.dtype)
            low = l if low is None else jnp.minimum(low, l)
        return low

    def attend_keys(n_keys):
        low = attend(n_keys, True)
        safe = jnp.min(low) >= ATTN_MIN_ROW_SUM

        @pl.when(jnp.logical_not(safe))
        def _():
            attend(n_keys, False)

    is_ctx = pl.program_id(2) < n_ctx_tiles
    pl.when(is_ctx)(lambda: attend_keys(n_ctx))
    pl.when(jnp.logical_not(is_ctx))(lambda: attend_keys(k_ref.shape[0]))


def _attn_score_bound(q_gain, k_gain):
    scale = (MLA_NOPE + MLA_ROPE) ** -0.5 * math.log2(math.e)
    gq, gk = jnp.abs(q_gain), jnp.abs(k_gain)
    dot_max = (MLA_NOPE * jnp.max(gq[:MLA_NOPE]) * jnp.max(gk[:MLA_NOPE])
               + MLA_ROPE * jnp.max(gq[MLA_NOPE:]) * jnp.max(gk[MLA_NOPE:]))
    return jnp.full((8, LANES), 1.01 * scale, F32) * dot_max


def _attention(q, k, v, bound, n_ctx):
    b, t, _ = q.shape
    tq = TOKEN_TILE
    hp = ATTN_HEADS_PER_STEP
    kern = functools.partial(_attn_kernel, n_ctx_tiles=n_ctx // tq, n_ctx=n_ctx)
    return pl.pallas_call(
        kern,
        grid=(b, MLA_HEADS // hp, t // tq),
        in_specs=[_resident(bound.shape),
                  pl.BlockSpec((None, tq, hp * MLA_HEAD_PAD), lambda i, h, j: (i, j, h)),
                  pl.BlockSpec((None, t, hp * MLA_HEAD_PAD), lambda i, h, j: (i, 0, h)),
                  pl.BlockSpec((None, t, hp * MLA_V), lambda i, h, j: (i, 0, h))],
        out_specs=pl.BlockSpec((None, tq, hp * MLA_V), lambda i, h, j: (i, j, h)),
        out_shape=jax.ShapeDtypeStruct((b, t, MLA_HEADS * MLA_V), BF16),
        compiler_params=_cparams("arbitrary", "arbitrary", "arbitrary"),
        name="mla_attention",
    )(bound, q, k, v)


def _ret_tables(n_ctx, n_lat):
    inv = ROPE_BASE ** (-jnp.arange(0, RET_DK, 2, dtype=F32) / RET_DK)
    ang = jnp.arange(n_lat, dtype=F32)[:, None] * inv[None, :]
    cos = jnp.concatenate([jnp.ones((n_ctx, LANES), F32), jnp.cos(ang)], axis=0)
    sin = jnp.concatenate([jnp.zeros((n_ctx, LANES), F32), jnp.sin(ang)], axis=0)
    return cos, sin


def _ret_proj_kernel(r_ref, cos_ref, sin_ref, mod_ref, g_ref, win_ref, q_ref, k_ref, v_ref, sg_ref):
    xn = _pre_norm(r_ref[...], g_ref[...], mod_ref, 1).astype(BF16)
    cos = cos_ref[...]
    sin = sin_ref[...]
    hdk = RET_HEADS * RET_DK
    hdv = RET_HEADS * RET_DV
    scale = RET_DK ** -0.5
    for out_ref, base, sc in ((q_ref, 0, None), (k_ref, hdk, scale)):
        qk = _dot(xn, win_ref[:, base:base + hdk])
        for h in range(RET_HEADS):
            c0 = h * RET_DK
            x1 = qk[:, c0:c0 + LANES]
            x2 = qk[:, c0 + LANES:c0 + 2 * LANES]
            o1 = x1 * cos - x2 * sin
            o2 = x2 * cos + x1 * sin
            if sc is not None:
                o1 = o1 * sc
                o2 = o2 * sc
            out_ref[:, c0:c0 + LANES] = o1.astype(BF16)
            out_ref[:, c0 + LANES:c0 + 2 * LANES] = o2.astype(BF16)
    v_ref[...] = _dot(xn, win_ref[:, 2 * hdk:2 * hdk + hdv]).astype(BF16)
    sg_ref[...] = _silu(_dot(xn, win_ref[:, 2 * hdk + hdv:])).astype(BF16)


def _ret_proj(r, cos, sin, mod, gain, w_in, n_ctx_tiles):
    b, t, d = r.shape
    tt = TOKEN_TILE
    consts = [gain.reshape(1, d), w_in]
    in_specs = [pl.BlockSpec((None, tt, d), lambda i, j: (i, j, 0)),
                pl.BlockSpec((tt, LANES), lambda i, j: (j, 0)),
                pl.BlockSpec((tt, LANES), lambda i, j: (j, 0)),
                pl.BlockSpec((None, None, N_MOD, d),
                             lambda i, j: (i, (j >= n_ctx_tiles).astype(jnp.int32), 0, 0))]
    in_specs += [_resident(a.shape) for a in consts]
    widths = [(RET_HEADS * RET_DK, BF16), (RET_HEADS * RET_DK, BF16),
              (RET_HEADS * RET_DV, BF16), (RET_HEADS * RET_DV, BF16)]
    return pl.pallas_call(
        _ret_proj_kernel,
        grid=(b, t // tt),
        in_specs=in_specs,
        out_specs=[pl.BlockSpec((None, tt, w), lambda i, j: (i, j, 0)) for w, _ in widths],
        out_shape=[jax.ShapeDtypeStruct((b, t, w), dt) for w, dt in widths],
        compiler_params=_cparams("arbitrary", "arbitrary"),
        name="retention_projections",
    )(r, cos, sin, mod, *consts)


def _ret_scan_kernel(l1g_ref, q_ref, k_ref, v_ref, sg_ref, gain_ref, y_ref, ob_scr, st_scr, *, n_ctx, n_all):
    c = SCAN_CHUNK
    log_g = jnp.log1p(-jnp.exp(l1g_ref[...]))
    lgf = log_g[0:1, 0:1]
    lgb = log_g[1:2, 0:1]
    ii = lax.broadcasted_iota(jnp.int32, (c, c), 0)
    jj = lax.broadcasted_iota(jnp.int32, (c, c), 1)
    diff = (ii - jj).astype(F32)
    mask = jnp.where(diff >= 0.0, jnp.exp(lgf * jnp.maximum(diff, 0.0)), jnp.exp(lgb * jnp.maximum(-diff, 0.0)))
    ic = lax.broadcasted_iota(jnp.int32, (c, 1), 0).astype(F32)
    qdec_f = jnp.exp(lgf * (ic + 1.0))
    qdec_b = jnp.exp(lgb * (c - ic))
    kdec_f = jnp.exp(lgf * (c - 1.0 - ic))
    kdec_b = jnp.exp(lgb * ic)
    cdec_f = jnp.exp(lgf * c)
    cdec_b = jnp.exp(lgb * c)

    def kv_outer(kc, dec, vc):
        return _dot((kc.astype(F32) * dec).T.astype(BF16), vc)

    st_scr[...] = jnp.zeros_like(st_scr)
    for n in list(range(n_ctx - 1, -1, -1)) + list(range(n_all - 1, n_ctx - 1, -1)):
        rows = slice(n * c, (n + 1) * c)
        state = st_scr[...]
        ob_scr[rows, :] = _dot((q_ref[rows, :].astype(F32) * qdec_b).astype(BF16), state.astype(BF16))
        st_scr[...] = state * cdec_b + kv_outer(k_ref[rows, :], kdec_b, v_ref[rows, :])

    st_scr[...] = jnp.zeros_like(st_scr)
    gain = gain_ref[...]
    for n in range(n_all):
        rows = slice(n * c, (n + 1) * c)
        qc = q_ref[rows, :]
        kc = k_ref[rows, :]
        vc = v_ref[rows, :]
        state = st_scr[...]
        scores = _dot_nt(qc, kc) * mask
        o = (_dot(scores.astype(BF16), vc)
             + _dot((qc.astype(F32) * qdec_f).astype(BF16), state.astype(BF16))
             + ob_scr[rows, :])
        mu = jnp.mean(o, axis=-1, keepdims=True)
        dev = o - mu
        on = dev * lax.rsqrt(jnp.mean(dev * dev, axis=-1, keepdims=True) + EPS)
        y_ref[rows, :] = (on * gain * sg_ref[rows, :].astype(F32)).astype(y_ref.dtype)
        st_scr[...] = state * cdec_f + kv_outer(kc, kdec_f, vc)


def _ret_scan(q, k, v, sg, log1m_gamma, gn_gain, n_ctx):
    b, t, _ = q.shape
    l1g = jnp.broadcast_to(log1m_gamma.T[:, :, None], (RET_HEADS, 2, LANES))
    l1g = jnp.concatenate([l1g, jnp.zeros((RET_HEADS, 6, LANES), F32) - 1.0], axis=1)
    kern = functools.partial(_ret_scan_kernel, n_ctx=n_ctx // SCAN_CHUNK, n_all=t // SCAN_CHUNK)
    return pl.pallas_call(
        kern,
        grid=(b, RET_HEADS),
        in_specs=[pl.BlockSpec((None, 8, LANES), lambda i, h: (h, 0, 0)),
                  pl.BlockSpec((None, t, RET_DK), lambda i, h: (i, 0, h)),
                  pl.BlockSpec((None, t, RET_DK), lambda i, h: (i, 0, h)),
                  pl.BlockSpec((None, t, RET_DV), lambda i, h: (i, 0, h)),
                  pl.BlockSpec((None, t, RET_DV), lambda i, h: (i, 0, h)),
                  pl.BlockSpec((1, RET_DV), lambda i, h: (0, h))],
        out_specs=pl.BlockSpec((None, t, RET_DV), lambda i, h: (i, 0, h)),
        out_shape=jax.ShapeDtypeStruct((b, t, RET_HEADS * RET_DV), BF16),
        scratch_shapes=[pltpu.VMEM((t, RET_DV), F32), pltpu.VMEM((RET_DK, RET_DV), F32)],
        compiler_params=_cparams("arbitrary", "arbitrary"),
        name="retention_scan",
    )(l1g, q, k, v, sg, gn_gain.reshape(1, -1))


CHUNKS_PER_TILE = TOKEN_TILE // SCAN_CHUNK


def _chunk_major_spec(g, tile0=0):
    return pl.BlockSpec((g, CHUNKS_PER_TILE, None, S5_GROUP, SCAN_CHUNK), lambda i, j: (0, j + tile0, i, 0, 0))


def _norm_t_kernel(r_ref, mod_ref, g_ref, o_ref):
    xn = _pre_norm(r_ref[...], g_ref[...], mod_ref, 1)
    c = SCAN_CHUNK
    for ci in range(CHUNKS_PER_TILE):
        o_ref[:, ci] = xn[ci * c:(ci + 1) * c, :].T.reshape(o_ref.shape[0], S5_GROUP, c)


def _norm_chunk_major(r, mod, gain, n_ctx_tiles):
    b, t, d = r.shape
    g = d // S5_GROUP
    return pl.pallas_call(
        _norm_t_kernel,
        grid=(b, t // TOKEN_TILE),
        in_specs=[_rows_spec(d), _mod_spec(d, n_ctx_tiles), _resident((1, d))],
        out_specs=_chunk_major_spec(g),
        out_shape=jax.ShapeDtypeStruct((g, t // SCAN_CHUNK, b, S5_GROUP, SCAN_CHUNK), F32),
        compiler_params=_cparams("arbitrary", "arbitrary"),
        name="s5_pre_norm",
    )(r, mod, gain.reshape(1, d))


def _s5_disc(lam_re, lam_im, log_dt):
    dt = jnp.exp(log_dt)
    a_re = lam_re * dt
    a_im = lam_im * dt
    mag = jnp.exp(a_re)
    e_re = mag * jnp.cos(a_im) - 1.0
    e_im = mag * jnp.sin(a_im)
    den = lam_re * lam_re + lam_im * lam_im
    return a_re, a_im, (e_re * lam_re + e_im * lam_im) / den, (e_im * lam_re - e_re * lam_im) / den


def _cmul(ar, ai, br, bi):
    return ar * br - ai * bi, ar * bi + ai * br


def _s5_kernel(u_ref, lamc_ref, lamr_ref, lamr2_ref, bc_ref, br_ref, ct_ref, d_ref,
               y_ref, t_scr, x_scr, w_scr, *, batch, n_ctx, n_all):
    c, m, p = SCAN_CHUNK, S5_GROUP, S5_STATE
    cm = c * m
    rows = n_all * batch
    for mi in range(m):
        x_scr[:, mi * c:(mi + 1) * c] = u_ref[pl.ds(mi, rows, stride=m), :].astype(BF16)

    kk = lax.broadcasted_iota(jnp.int32, (1, LANES), 1).astype(F32)

    def powers(a_re, a_im, expo):
        mag = jnp.exp(a_re * expo)
        return mag * jnp.cos(a_im * expo), mag * jnp.sin(a_im * expo)

    def lane_tiles(fn):
        parts = [fn(i) for i in range(m)]
        return (jnp.concatenate([q[0] for q in parts], axis=1), jnp.concatenate([q[1] for q in parts], axis=1))

    disc = [_s5_disc(lamc_ref[d][:, 0:1], lamc_ref[d][:, 1:2], lamc_ref[d][:, 2:3]) for d in range(2)]
    bbar = [_cmul(disc[d][2], disc[d][3], bc_ref[d, 0], bc_ref[d, 1]) for d in range(2)]
    ctr = [(ct_ref[d, 0], ct_ref[d, 1]) for d in range(2)]

    def col(pair, i):
        return pair[0][:, i:i + 1], pair[1][:, i:i + 1]

    f_up = powers(disc[0][0], disc[0][1], kk + 1.0)
    f_down = powers(disc[0][0], disc[0][1], (c - 1.0) - kk)
    b_up = powers(disc[1][0], disc[1][1], kk)
    b_down = powers(disc[1][0], disc[1][1], (c - 1.0) - kk)
    b_down1 = powers(disc[1][0], disc[1][1], c - kk)
    lag0 = jnp.where(kk == c - 1.0, 1.0, 0.0)

    q0f = lane_tiles(lambda mo: (col(ctr[0], mo)[0] * lag0, col(ctr[0], mo)[1] * lag0))
    q1f = lane_tiles(lambda mo: _cmul(*col(ctr[0], mo), *f_up))
    q0b = lane_tiles(lambda mo: _cmul(*col(ctr[1], mo), *b_down))
    qrb = lane_tiles(lambda mo: _cmul(*col(ctr[1], mo), *b_down1))

    hp = lax.Precision.HIGHEST
    btr = []
    for d in range(2):
        lam = lamr_ref[d]
        _, _, cf_re, cf_im = _s5_disc(lam[0:1, :], lam[1:2, :], lam[2:3, :])
        btr.append(_cmul(cf_re, cf_im, br_ref[d, 0], br_ref[d, 1]))

    def gen(bt, q):
        return (jnp.dot(bt[0], q[0], precision=hp, preferred_element_type=F32)
                - jnp.dot(bt[1], q[1], precision=hp, preferred_element_type=F32))

    w0 = gen(btr[0], q0f) + gen(btr[1], q0b)
    w1 = gen(btr[0], q1f)
    for mo in range(m):
        w_scr[:, 2 * mo * c:(2 * mo + 1) * c] = w0[:, mo * c:(mo + 1) * c]
        w_scr[:, (2 * mo + 1) * c:(2 * mo + 2) * c] = w1[:, mo * c:(mo + 1) * c]

    for mi in range(m):
        row = jnp.broadcast_to(w_scr[mi:mi + 1, :], (c, 2 * cm))
        win = pltpu.roll(row, 2 * cm - (c - 1), 1, stride=1, stride_axis=0)
        blk = jnp.concatenate([win[:, 2 * mo * c:(2 * mo + 1) * c] for mo in range(m)], axis=1)
        t_scr[mi * c:(mi + 1) * c, :] = blk.astype(BF16)

    for d, q in ((0, q1f), (1, qrb)):
        base = cm + 2 * p * d
        t_scr[base:base + p, :] = q[0].astype(BF16)
        t_scr[base + p:base + 2 * p, :] = (-q[1]).astype(BF16)

    wb_f = lane_tiles(lambda mi: _cmul(*col(bbar[0], mi), *f_down))
    wb_b = lane_tiles(lambda mi: _cmul(*col(bbar[1], mi), *b_up))
    wb = jnp.concatenate([wb_f[0], wb_f[1], wb_b[0], wb_b[1]], axis=0).astype(BF16)
    hloc = _dot_nt(x_scr[:, :cm], wb)

    sign = jnp.where(lax.broadcasted_iota(jnp.int32, (1, 2 * p), 1) < p, -1.0, 1.0)
    order_f = list(range(n_all))
    order_b = list(range(n_ctx - 1, -1, -1)) + list(range(n_all - 1, n_ctx - 1, -1))
    for d, order in ((0, order_f), (1, order_b)):
        lam = lamr2_ref[d]
        a_re, a_im, _, _ = _s5_disc(lam[0:1, :], lam[1:2, :], lam[2:3, :])
        mag = jnp.exp(a_re * c)
        ac_r = mag * jnp.cos(a_im * c)
        ac_i = mag * jnp.sin(a_im * c) * sign
        state = jnp.zeros((batch, 2 * p), F32)
        colx = cm + 2 * p * d
        for n in order:
            x_scr[n * batch:(n + 1) * batch, colx:colx + 2 * p] = state.astype(BF16)
            local = hloc[n * batch:(n + 1) * batch, 2 * p * d:2 * p * (d + 1)]
            state = state * ac_r + pltpu.roll(state, p, 1) * ac_i + local

    y = _dot(x_scr[...], t_scr[...])
    for mo in range(m):
        skip = d_ref[mo:mo + 1, :] * u_ref[pl.ds(mo, rows, stride=m), :]
        y_ref[pl.ds(mo, rows, stride=m), :] = y[:, mo * c:(mo + 1) * c] + skip


def _s5_operands(u, lam_re, lam_im, log_dt, b_re, b_im, c_re, c_im, d_skip):
    g, n_all, batch, m, c = u.shape
    dtb = jnp.broadcast_to(log_dt[:, :, None], lam_re.shape)
    lam3 = jnp.stack([lam_re, lam_im, dtb], axis=-1)
    lamc = jnp.pad(lam3, ((0, 0), (0, 0), (0, 0), (0, LANES - 3))).transpose(1, 0, 2, 3)
    lam3r = jnp.stack([lam_re, lam_im, dtb], axis=2)
    lamr = jnp.pad(lam3r, ((0, 0), (0, 0), (0, 5), (0, 0))).transpose(1, 0, 2, 3)
    lamr2 = jnp.concatenate([lamr, lamr], axis=-1)
    bcol = jnp.stack([b_re, b_im], axis=1)
    bc = jnp.pad(bcol, ((0, 0),) * 4 + ((0, LANES - m),)).transpose(2, 0, 1, 3, 4)
    brow = jnp.swapaxes(bcol, -1, -2).transpose(2, 0, 1, 3, 4)
    ctc = jnp.swapaxes(jnp.stack([c_re, c_im], axis=1), -1, -2)
    ctc = jnp.pad(ctc, ((0, 0),) * 4 + ((0, LANES - m),)).transpose(2, 0, 1, 3, 4)
    dvec = jnp.broadcast_to(d_skip.reshape(g, m, 1), (g, m, LANES))
    return (u.reshape(g, n_all * batch * m, c), lamc, lamr, lamr2, bc, brow, ctc, dvec)


def _s5_mixer(r, mod, gain, lam_re, lam_im, log_dt, b_re, b_im, c_re, c_im, d_skip, n_ctx, n_ctx_tiles):
    b = r.shape[0]
    u = _norm_chunk_major(r, mod, gain, n_ctx_tiles)
    g, n_all, _, m, c = u.shape
    p = S5_STATE
    rows = n_all * b
    cm = c * m
    kern = functools.partial(_s5_kernel, batch=b, n_ctx=n_ctx // c, n_all=n_all)
    per_g = lambda *tail: pl.BlockSpec((None,) + tail, lambda i: (i,) + (0,) * len(tail))
    y = pl.pallas_call(
        kern,
        grid=(g,),
        in_specs=[per_g(rows * m, c), per_g(2, p, LANES), per_g(2, 8, p), per_g(2, 8, 2 * p),
                  per_g(2, 2, p, LANES), per_g(2, 2, m, p), per_g(2, 2, p, LANES), per_g(m, LANES)],
        out_specs=per_g(rows * m, c),
        out_shape=jax.ShapeDtypeStruct((g, rows * m, c), F32),
        scratch_shapes=[pltpu.VMEM((cm + 4 * p, cm), BF16), pltpu.VMEM((rows, cm + 4 * p), BF16),
                        pltpu.VMEM((m, 2 * cm), F32)],
        compiler_params=_cparams("arbitrary"),
        name="s5_scan",
    )(*_s5_operands(u, lam_re, lam_im, log_dt, b_re, b_im, c_re, c_im, d_skip))
    return y.reshape(u.shape)


def kernel(x, c, ctx, c_ctx, ada_w, ada_b, norm_g, ffn_w_in, ffn_w_out, mla_w_in, mla_q_norm, mla_kv_norm,
           mla_w_uq, mla_w_ukv, mla_q_gain, mla_k_gain, mla_w_o, ret_w_in, ret_log1m_gamma, ret_gn_gain, ret_w_o,
           s5_lam_re, s5_lam_im, s5_log_dt, s5_b_re, s5_b_im, s5_c_re, s5_c_im, s5_d, s5_glu_w, s5_glu_b):
    b, s, d = x.shape
    n_ctx = ctx.shape[1]
    depth = ada_w.shape[0]
    assert n_ctx % TOKEN_TILE == 0 and s % TOKEN_TILE == 0 and s % GRID_W == 0
    n_ctx_tiles = n_ctx // TOKEN_TILE

    pad = (-(b + 1)) % 8
    cvec = jnp.concatenate([c, c_ctx[None, :], jnp.zeros((pad, d), F32)], axis=0)
    mod_all = _modulation(cvec, ada_w, ada_b)
    mod_lat = mod_all[:, :b].reshape(depth, b, 1, N_MOD, d)
    mod_ctx = jnp.broadcast_to(mod_all[:, b].reshape(depth, 1, 1, N_MOD, d), (depth, b, 1, N_MOD, d))
    mods = jnp.concatenate([mod_ctx, mod_lat], axis=2)

    mla_tabs = _mla_tables(n_ctx, s)
    ret_tabs = _ret_tables(n_ctx, s)

    r = None
    for i in range(depth):
        kind, j = i % N_MIXERS, i // N_MIXERS
        mod = mods[i]
        ffn0 = (norm_g[i, 0], ffn_w_in[i, 0].astype(BF16), ffn_w_out[i, 0].astype(BF16))
        ffn1 = (norm_g[i, 2], ffn_w_in[i, 1].astype(BF16), ffn_w_out[i, 1].astype(BF16))
        r = _ffn_first(ctx, x, mod, *ffn0, n_ctx_tiles) if i == 0 else _ffn(r, mod, *ffn0, n_ctx_tiles)
        tile0 = n_ctx_tiles if i == depth - 1 else 0
        if kind == 0:
            weights = _mla_weights(mla_w_in[j], mla_q_norm[j], mla_kv_norm[j], mla_w_uq[j], mla_w_ukv[j],
                                   mla_q_gain[j], mla_k_gain[j])
            q, k, v = _mla_proj(r, *mla_tabs, mod, norm_g[i, 1], weights, n_ctx_tiles)
            y = _attention(q, k, v, _attn_score_bound(mla_q_gain[j], mla_k_gain[j]), n_ctx)
            mix = (y, _rows_spec(y.shape[-1], tile0), [mla_w_o[j].astype(BF16)], _ffn_proj_kernel, "mla_out_swiglu")
        elif kind == 1:
            q, k, v, sg = _ret_proj(r, *ret_tabs, mod, norm_g[i, 1], ret_w_in[j].astype(BF16), n_ctx_tiles)
            y = _ret_scan(q, k, v, sg, ret_log1m_gamma[j], ret_gn_gain[j], n_ctx)
            mix = (y, _rows_spec(y.shape[-1], tile0), [ret_w_o[j].astype(BF16)], _ffn_proj_kernel, "ret_out_swiglu")
        else:
            y = _s5_mixer(r, mod, norm_g[i, 1], s5_lam_re[j], s5_lam_im[j], s5_log_dt[j], s5_b_re[j], s5_b_im[j],
                          s5_c_re[j], s5_c_im[j], s5_d[j], n_ctx, n_ctx_tiles)
            mix = (y, _chunk_major_spec(y.shape[0], tile0), [s5_glu_w[j].astype(BF16), s5_glu_b[j].reshape(1, -1)],
                   _ffn_glu_kernel, "s5_out_swiglu")
        r = _ffn_after_mixer(r, *mix, mod, *ffn1, n_ctx_tiles, tile0)
    return r
```

```python
import functools
import math

import jax
import jax.numpy as jnp
import numpy as np
from jax import lax
from jax.experimental import pallas as pl
from jax.experimental.pallas import tpu as pltpu

F32 = jnp.float32
BF16 = jnp.bfloat16

EPS = 1e-6
ROPE_BASE = 10000.0
GRID_W = 64
N_MIXERS = 3
N_MOD = 9

MLA_HEADS = 8
MLA_NOPE = 128
MLA_ROPE = 64
MLA_V = 128
MLA_Q_RANK = 384
MLA_KV_RANK = 256
MLA_HEAD_PAD = 256

RET_HEADS = 4
RET_DK = 256
RET_DV = 512
SCAN_CHUNK = 128

S5_GROUP = 16
S5_STATE = 64

LANES = 128
TOKEN_TILE = 256
ATTN_HEADS_PER_STEP = 4
VMEM_LIMIT_BYTES = 56 * 1024 * 1024


def _cparams(*sem):
    return pltpu.CompilerParams(dimension_semantics=sem, vmem_limit_bytes=VMEM_LIMIT_BYTES)


def _resident(shape):
    zeros = (0,) * len(shape)
    return pl.BlockSpec(shape, lambda *_: zeros)


def _silu(x):
    return x * jax.nn.sigmoid(x)


def _rms(x, n):
    return x * lax.rsqrt(jnp.sum(x * x, axis=-1, keepdims=True) * (1.0 / n) + EPS)


def _pre_norm(x, gain, mod_ref, k):
    y = _rms(x, x.shape[-1]) * gain
    return y * (1.0 + mod_ref[3 * k + 1:3 * k + 2, :]) + mod_ref[3 * k:3 * k + 1, :]


def _dot(a, b):
    return jnp.dot(a, b, preferred_element_type=F32)


def _dot_nt(a, b):
    return lax.dot_general(a, b, (((1,), (1,)), ((), ())), preferred_element_type=F32)


def _mod_kernel(cv_ref, w_ref, b_ref, o_ref):
    s = _silu(cv_ref[...]).astype(BF16)
    o_ref[...] = _dot(s, w_ref[...].astype(BF16)) + b_ref[...]


def _modulation(cvec, ada_w, ada_b):
    depth, d, n = ada_w.shape
    rows = cvec.shape[0]
    tn = n // 4
    return pl.pallas_call(
        _mod_kernel,
        grid=(depth, n // tn),
        in_specs=[
            _resident((rows, d)),
            pl.BlockSpec((None, d, tn), lambda i, j: (i, 0, j)),
            pl.BlockSpec((None, 1, tn), lambda i, j: (i, 0, j)),
        ],
        out_specs=pl.BlockSpec((None, rows, tn), lambda i, j: (i, 0, j)),
        out_shape=jax.ShapeDtypeStruct((depth, rows, n), F32),
        compiler_params=_cparams("arbitrary", "arbitrary"),
        name="adaln_modulation",
    )(cvec, ada_w, ada_b.reshape(depth, 1, n))


def _rows_spec(width, tile0=0):
    return pl.BlockSpec((None, TOKEN_TILE, width), lambda i, j: (i, j + tile0, 0))


def _mod_spec(d, n_ctx_tiles, tile0=0):
    return pl.BlockSpec((None, None, N_MOD, d),
                        lambda i, j: (i, (j + tile0 >= n_ctx_tiles).astype(jnp.int32), 0, 0))


def _row_call(kernel, name, b, n_tiles, inputs, consts, outs):
    in_specs = [spec for _, spec in inputs] + [_resident(a.shape) for a in consts]
    return pl.pallas_call(
        kernel,
        grid=(b, n_tiles),
        in_specs=in_specs,
        out_specs=[_rows_spec(w) for w, _ in outs],
        out_shape=[jax.ShapeDtypeStruct((b, n_tiles * TOKEN_TILE, w), dt) for w, dt in outs],
        compiler_params=_cparams("arbitrary", "arbitrary"),
        name=name,
    )(*[a for a, _ in inputs], *consts)


def _gelu_tanh(x):
    cdf = 0.5 * (1.0 + jnp.tanh(math.sqrt(2.0 / math.pi) * (x + 0.044715 * (x * x * x))))
    return x * cdf


def _swiglu_step(x, mod_ref, g_ref, win_ref, wout_ref, k):
    xn = _pre_norm(x, g_ref[...], mod_ref, k).astype(BF16)
    f = wout_ref.shape[0]
    a = _dot(xn, win_ref[:, :f])
    b = _dot(xn, win_ref[:, f:])
    hm = (_silu(a) * b).astype(BF16)
    return x + (0.5 * mod_ref[3 * k + 2:3 * k + 3, :]) * _dot(hm, wout_ref[...])


def _ffn_first_kernel(ctx_ref, x_ref, mod_ref, g_ref, win_ref, wout_ref, o_ref, *, n_ctx_tiles):
    x = jnp.where(pl.program_id(1) < n_ctx_tiles, ctx_ref[...], x_ref[...])
    o_ref[...] = _swiglu_step(x, mod_ref, g_ref, win_ref, wout_ref, 0)


def _ffn_kernel(r_ref, mod_ref, g_ref, win_ref, wout_ref, o_ref):
    o_ref[...] = _swiglu_step(r_ref[...], mod_ref, g_ref, win_ref, wout_ref, 0)


def _ffn_proj_kernel(r_ref, y_ref, mod_ref, pw_ref, g_ref, win_ref, wout_ref, o_ref):
    x = r_ref[...] + mod_ref[5:6, :] * _dot(y_ref[...], pw_ref[...])
    o_ref[...] = _swiglu_step(x, mod_ref, g_ref, win_ref, wout_ref, 2)


def _ffn_glu_kernel(r_ref, y_ref, mod_ref, pw_ref, pb_ref, g_ref, win_ref, wout_ref, o_ref):
    d = r_ref.shape[-1]
    c = SCAN_CHUNK
    y = jnp.concatenate([y_ref[:, ci].reshape(d, c).T for ci in range(TOKEN_TILE // c)], axis=0)
    ag = _dot(_gelu_tanh(y).astype(BF16), pw_ref[...]) + pb_ref[...]
    x = r_ref[...] + mod_ref[5:6, :] * (ag[:, :d] * jax.nn.sigmoid(ag[:, d:]))
    o_ref[...] = _swiglu_step(x, mod_ref, g_ref, win_ref, wout_ref, 2)


def _ffn_first(ctx, x, mod, gain, w_in, w_out, n_ctx_tiles):
    b, s, d = x.shape
    last_ctx = n_ctx_tiles - 1
    inputs = [(ctx, pl.BlockSpec((None, TOKEN_TILE, d), lambda i, j: (i, jnp.minimum(j, last_ctx), 0))),
              (x, pl.BlockSpec((None, TOKEN_TILE, d), lambda i, j: (i, jnp.maximum(j - n_ctx_tiles, 0), 0))),
              (mod, _mod_spec(d, n_ctx_tiles))]
    (out,) = _row_call(functools.partial(_ffn_first_kernel, n_ctx_tiles=n_ctx_tiles), "swiglu_first",
                       b, n_ctx_tiles + s // TOKEN_TILE, inputs, [gain.reshape(1, d), w_in, w_out], [(d, F32)])
    return out


def _ffn(r, mod, gain, w_in, w_out, n_ctx_tiles):
    b, t, d = r.shape
    inputs = [(r, _rows_spec(d)), (mod, _mod_spec(d, n_ctx_tiles))]
    (out,) = _row_call(_ffn_kernel, "swiglu_half_step", b, t // TOKEN_TILE, inputs,
                       [gain.reshape(1, d), w_in, w_out], [(d, F32)])
    return out


def _ffn_after_mixer(r, y, y_spec, mix_consts, kern, name, mod, gain, w_in, w_out, n_ctx_tiles, tile0):
    b, t, d = r.shape
    inputs = [(r, _rows_spec(d, tile0)), (y, y_spec), (mod, _mod_spec(d, n_ctx_tiles, tile0))]
    (out,) = _row_call(kern, name, b, t // TOKEN_TILE - tile0, inputs,
                       mix_consts + [gain.reshape(1, d), w_in, w_out], [(d, F32)])
    return out


def _rope_tile_perm():
    src = np.full((LANES,), -1, np.int64)
    src[0:16] = np.arange(0, 16)
    src[16:32] = np.arange(32, 48)
    src[64:80] = np.arange(16, 32)
    src[80:96] = np.arange(48, 64)
    return src


def _pad_rope_cols(w):
    src = _rope_tile_perm()
    cols = jnp.take(w, jnp.asarray(np.maximum(src, 0)), axis=-1)
    return jnp.where(jnp.asarray(src >= 0), cols, 0.0)


def _mla_tables(n_ctx, n_lat):
    half = MLA_ROPE // 2
    inv = ROPE_BASE ** (-jnp.arange(0, half, 2, dtype=F32) / half)
    pos = jnp.arange(n_lat)
    ang_r = (pos // GRID_W).astype(F32)[:, None] * inv[None, :]
    ang_c = (pos % GRID_W).astype(F32)[:, None] * inv[None, :]
    z = jnp.zeros((n_lat, 32), F32)
    cos = jnp.concatenate([jnp.cos(ang_r), jnp.cos(ang_c), z, jnp.cos(ang_r), jnp.cos(ang_c), z], axis=-1)
    sin = jnp.concatenate([-jnp.sin(ang_r), -jnp.sin(ang_c), z, jnp.sin(ang_r), jnp.sin(ang_c), z], axis=-1)
    cos = jnp.concatenate([jnp.ones((n_ctx, LANES), F32), cos], axis=0)
    sin = jnp.concatenate([jnp.zeros((n_ctx, LANES), F32), sin], axis=0)
    return cos, sin


def _mla_proj_kernel(r_ref, cos_ref, sin_ref, mod_ref, g_ref, win_ref, qn_ref, kvn_ref, wuq_ref, wukv_ref,
                     qg_ref, kg_ref, q_ref, k_ref, v_ref):
    xn = _pre_norm(r_ref[...], g_ref[...], mod_ref, 1).astype(BF16)
    p = _dot(xn, win_ref[...])
    cos = cos_ref[...]
    sin = sin_ref[...]

    def rotate(x):
        return x * cos + pltpu.roll(x, LANES // 2, 1) * sin

    cq = (_rms(p[:, :MLA_Q_RANK], MLA_Q_RANK) * qn_ref[...]).astype(BF16)
    ckv = (_rms(p[:, MLA_Q_RANK:MLA_Q_RANK + MLA_KV_RANK], MLA_KV_RANK) * kvn_ref[...]).astype(BF16)
    kr = rotate(_rms(p[:, MLA_Q_RANK + MLA_KV_RANK:], MLA_ROPE) * kg_ref[:, LANES:]).astype(BF16)
    q = _dot(cq, wuq_ref[...])
    kv = _dot(ckv, wukv_ref[...])
    scale = (MLA_NOPE + MLA_ROPE) ** -0.5 * math.log2(math.e)
    ii = lax.broadcasted_iota(jnp.int32, (2 * LANES, 2 * LANES), 0)
    jj = lax.broadcasted_iota(jnp.int32, (2 * LANES, 2 * LANES), 1)
    same = (ii // LANES) == (jj // LANES)
    mean_q = jnp.where(same, jnp.where(ii < LANES, 1.0 / MLA_NOPE, 1.0 / MLA_ROPE), 0.0).astype(BF16)
    mean_k = jnp.where(same, 1.0 / MLA_NOPE, 0.0).astype(BF16)
    q_gain = qg_ref[...] * scale
    for h in range(MLA_HEADS):
        a0 = h * MLA_HEAD_PAD
        qh = q[:, a0:a0 + 2 * LANES]
        qn = qh * lax.rsqrt(_dot((qh * qh).astype(BF16), mean_q) + EPS) * q_gain
        q_ref[:, a0:a0 + LANES] = qn[:, :LANES].astype(BF16)
        q_ref[:, a0 + LANES:a0 + 2 * LANES] = rotate(qn[:, LANES:]).astype(BF16)
        k_ref[:, a0 + LANES:a0 + 2 * LANES] = kr
    for hh in range(MLA_HEADS // 2):
        kh = kv[:, 2 * hh * LANES:(2 * hh + 2) * LANES]
        kn = kh * lax.rsqrt(_dot((kh * kh).astype(BF16), mean_k) + EPS)
        for e in range(2):
            a0 = (2 * hh + e) * MLA_HEAD_PAD
            k_ref[:, a0:a0 + LANES] = (kn[:, e * LANES:(e + 1) * LANES] * kg_ref[:, :LANES]).astype(BF16)
    v_ref[...] = kv[:, MLA_HEADS * LANES:].astype(BF16)


def _mla_weights(w_in, q_norm, kv_norm, w_uq, w_ukv, q_gain, k_gain):
    lat = MLA_Q_RANK + MLA_KV_RANK
    w_in_p = jnp.concatenate([w_in[:, :lat], _pad_rope_cols(w_in[:, lat:])], axis=-1).astype(BF16)
    uq = w_uq.reshape(MLA_Q_RANK, MLA_HEADS, MLA_NOPE + MLA_ROPE)
    uq = jnp.concatenate([uq[..., :MLA_NOPE], _pad_rope_cols(uq[..., MLA_NOPE:])], axis=-1)
    uq = uq.reshape(MLA_Q_RANK, MLA_HEADS * MLA_HEAD_PAD).astype(BF16)
    ukv = w_ukv.reshape(MLA_KV_RANK, MLA_HEADS, MLA_NOPE + MLA_V)
    ukv = jnp.concatenate([ukv[..., :MLA_NOPE].reshape(MLA_KV_RANK, -1),
                           ukv[..., MLA_NOPE:].reshape(MLA_KV_RANK, -1)], axis=-1).astype(BF16)
    qg = jnp.concatenate([q_gain[:MLA_NOPE], _pad_rope_cols(q_gain[MLA_NOPE:])]).reshape(1, -1)
    kg = jnp.concatenate([k_gain[:MLA_NOPE], _pad_rope_cols(k_gain[MLA_NOPE:])]).reshape(1, -1)
    return [w_in_p, q_norm.reshape(1, -1), kv_norm.reshape(1, -1), uq, ukv, qg, kg]


def _mla_proj(r, cos, sin, mod, gain, weights, n_ctx_tiles):
    b, t, d = r.shape
    cos_b = jnp.broadcast_to(cos[None], (1,) + cos.shape)
    sin_b = jnp.broadcast_to(sin[None], (1,) + sin.shape)
    tt = TOKEN_TILE
    hw = MLA_HEADS * MLA_HEAD_PAD
    consts = [gain.reshape(1, d)] + weights
    in_specs = [pl.BlockSpec((None, tt, d), lambda i, j: (i, j, 0)),
                pl.BlockSpec((None, tt, LANES), lambda i, j: (0, j, 0)),
                pl.BlockSpec((None, tt, LANES), lambda i, j: (0, j, 0)),
                pl.BlockSpec((None, None, N_MOD, d),
                             lambda i, j: (i, (j >= n_ctx_tiles).astype(jnp.int32), 0, 0))]
    in_specs += [_resident(a.shape) for a in consts]
    widths = [(hw, BF16), (hw, BF16), (MLA_HEADS * MLA_V, BF16)]
    return pl.pallas_call(
        _mla_proj_kernel,
        grid=(b, t // tt),
        in_specs=in_specs,
        out_specs=[pl.BlockSpec((None, tt, w), lambda i, j: (i, j, 0)) for w, _ in widths],
        out_shape=[jax.ShapeDtypeStruct((b, t, w), dt) for w, dt in widths],
        compiler_params=_cparams("arbitrary", "arbitrary"),
        name="mla_projections",
    )(r, cos_b, sin_b, mod, *consts)


ATTN_MIN_ROW_SUM = 2.0 ** -100


def _attn_kernel(bound_ref, q_ref, k_ref, v_ref, o_ref, *, n_ctx_tiles, n_ctx):
    def attend(n_keys, use_bound):
        low = None
        for h in range(ATTN_HEADS_PER_STEP):
            qk = slice(h * MLA_HEAD_PAD, (h + 1) * MLA_HEAD_PAD)
            vo = slice(h * MLA_V, (h + 1) * MLA_V)
            s = _dot_nt(q_ref[:, qk], k_ref[:n_keys, qk])
            shift = bound_ref[0:1, 0:1] if use_bound else jnp.max(s, axis=-1, keepdims=True)
            p = jnp.exp2(s - shift)
            l = jnp.sum(p, axis=-1, keepdims=True)
            o = _dot(p.astype(BF16), v_ref[:n_keys, vo])
            o_ref[:, vo] = (o / l).astype(o_ref.dtype)
            low = l if low is None else jnp.minimum(low, l)
        return low

    def attend_keys(n_keys):
        low = attend(n_keys, True)
        safe = jnp.min(low) >= ATTN_MIN_ROW_SUM

        @pl.when(jnp.logical_not(safe))
        def _():
            attend(n_keys, False)

    is_ctx = pl.program_id(2) < n_ctx_tiles
    pl.when(is_ctx)(lambda: attend_keys(n_ctx))
    pl.when(jnp.logical_not(is_ctx))(lambda: attend_keys(k_ref.shape[0]))


def _attn_score_bound(q_gain, k_gain):
    scale = (MLA_NOPE + MLA_ROPE) ** -0.5 * math.log2(math.e)
    gq, gk = jnp.abs(q_gain), jnp.abs(k_gain)
    dot_max = (MLA_NOPE * jnp.max(gq[:MLA_NOPE]) * jnp.max(gk[:MLA_NOPE])
               + MLA_ROPE * jnp.max(gq[MLA_NOPE:]) * jnp.max(gk[MLA_NOPE:]))
    return jnp.full((8, LANES), 1.01 * scale, F32) * dot_max


def _attention(q, k, v, bound, n_ctx):
    b, t, _ = q.shape
    tq = TOKEN_TILE
    hp = ATTN_HEADS_PER_STEP
    kern = functools.partial(_attn_kernel, n_ctx_tiles=n_ctx // tq, n_ctx=n_ctx)
    return pl.pallas_call(
        kern,
        grid=(b, MLA_HEADS // hp, t // tq),
        in_specs=[_resident(bound.shape),
                  pl.BlockSpec((None, tq, hp * MLA_HEAD_PAD), lambda i, h, j: (i, j, h)),
                  pl.BlockSpec((None, t, hp * MLA_HEAD_PAD), lambda i, h, j: (i, 0, h)),
                  pl.BlockSpec((None, t, hp * MLA_V), lambda i, h, j: (i, 0, h))],
        out_specs=pl.BlockSpec((None, tq, hp * MLA_V), lambda i, h, j: (i, j, h)),
        out_shape=jax.ShapeDtypeStruct((b, t, MLA_HEADS * MLA_V), BF16),
        compiler_params=_cparams("arbitrary", "arbitrary", "arbitrary"),
        name="mla_attention",
    )(bound, q, k, v)


def _ret_tables(n_ctx, n_lat):
    inv = ROPE_BASE ** (-jnp.arange(0, RET_DK, 2, dtype=F32) / RET_DK)
    ang = jnp.arange(n_lat, dtype=F32)[:, None] * inv[None, :]
    cos = jnp.concatenate([jnp.ones((n_ctx, LANES), F32), jnp.cos(ang)], axis=0)
    sin = jnp.concatenate([jnp.zeros((n_ctx, LANES), F32), jnp.sin(ang)], axis=0)
    return cos, sin


def _ret_proj_kernel(r_ref, cos_ref, sin_ref, mod_ref, g_ref, win_ref, q_ref, k_ref, v_ref, sg_ref):
    xn = _pre_norm(r_ref[...], g_ref[...], mod_ref, 1).astype(BF16)
    cos = cos_ref[...]
    sin = sin_ref[...]
    hdk = RET_HEADS * RET_DK
    hdv = RET_HEADS * RET_DV
    scale = RET_DK ** -0.5
    for out_ref, base, sc in ((q_ref, 0, None), (k_ref, hdk, scale)):
        qk = _dot(xn, win_ref[:, base:base + hdk])
        for h in range(RET_HEADS):
            c0 = h * RET_DK
            x1 = qk[:, c0:c0 + LANES]
            x2 = qk[:, c0 + LANES:c0 + 2 * LANES]
            o1 = x1 * cos - x2 * sin
            o2 = x2 * cos + x1 * sin
            if sc is not None:
                o1 = o1 * sc
                o2 = o2 * sc
            out_ref[:, c0:c0 + LANES] = o1.astype(BF16)
            out_ref[:, c0 + LANES:c0 + 2 * LANES] = o2.astype(BF16)
    v_ref[...] = _dot(xn, win_ref[:, 2 * hdk:2 * hdk + hdv]).astype(BF16)
    sg_ref[...] = _silu(_dot(xn, win_ref[:, 2 * hdk + hdv:])).astype(BF16)


def _ret_proj(r, cos, sin, mod, gain, w_in, n_ctx_tiles):
    b, t, d = r.shape
    tt = TOKEN_TILE
    consts = [gain.reshape(1, d), w_in]
    in_specs = [pl.BlockSpec((None, tt, d), lambda i, j: (i, j, 0)),
                pl.BlockSpec((tt, LANES), lambda i, j: (j, 0)),
                pl.BlockSpec((tt, LANES), lambda i, j: (j, 0)),
                pl.BlockSpec((None, None, N_MOD, d),
                             lambda i, j: (i, (j >= n_ctx_tiles).astype(jnp.int32), 0, 0))]
    in_specs += [_resident(a.shape) for a in consts]
    widths = [(RET_HEADS * RET_DK, BF16), (RET_HEADS * RET_DK, BF16),
              (RET_HEADS * RET_DV, BF16), (RET_HEADS * RET_DV, BF16)]
    return pl.pallas_call(
        _ret_proj_kernel,
        grid=(b, t // tt),
        in_specs=in_specs,
        out_specs=[pl.BlockSpec((None, tt, w), lambda i, j: (i, j, 0)) for w, _ in widths],
        out_shape=[jax.ShapeDtypeStruct((b, t, w), dt) for w, dt in widths],
        compiler_params=_cparams("arbitrary", "arbitrary"),
        name="retention_projections",
    )(r, cos, sin, mod, *consts)


def _ret_scan_kernel(l1g_ref, q_ref, k_ref, v_ref, sg_ref, gain_ref, y_ref, cross_scr, st_scr, *, n_ctx, n_all):
    c = SCAN_CHUNK
    log_g = jnp.log1p(-jnp.exp(l1g_ref[...]))
    lgf = log_g[0:1, 0:1]
    lgb = log_g[1:2, 0:1]
    ii = lax.broadcasted_iota(jnp.int32, (c, c), 0)
    jj = lax.broadcasted_iota(jnp.int32, (c, c), 1)
    diff = (ii - jj).astype(F32)
    mask = jnp.where(diff >= 0.0, jnp.exp(lgf * jnp.maximum(diff, 0.0)), jnp.exp(lgb * jnp.maximum(-diff, 0.0)))
    ic = lax.broadcasted_iota(jnp.int32, (c, 1), 0).astype(F32)
    qdec_f = jnp.exp(lgf * (ic + 1.0))
    qdec_b = jnp.exp(lgb * (c - ic))
    kdec_f = jnp.exp(lgf * (c - 1.0 - ic))
    kdec_b = jnp.exp(lgb * ic)
    cdec_f = jnp.exp(lgf * c)
    cdec_b = jnp.exp(lgb * c)

    def kv_outer(kc, dec, vc):
        return _dot((kc.astype(F32) * dec).T.astype(BF16), vc)

    sf_scr, sb_scr = st_scr.at[0], st_scr.at[1]
    sf_scr[...] = jnp.zeros_like(sf_scr)
    sb_scr[...] = jnp.zeros_like(sb_scr)
    order_b = list(range(n_ctx - 1, -1, -1)) + list(range(n_all - 1, n_ctx - 1, -1))
    seen = set()
    for nf, nb in zip(range(n_all), order_b):
        for n, s_ref, qdec, kdec, cdec in ((nf, sf_scr, qdec_f, kdec_f, cdec_f), (nb, sb_scr, qdec_b, kdec_b, cdec_b)):
            rows = slice(n * c, (n + 1) * c)
            state = s_ref[...]
            part = _dot((q_ref[rows, :].astype(F32) * qdec).astype(BF16), state.astype(BF16))
            if n in seen:
                cross_scr[rows, :] += part
            else:
                cross_scr[rows, :] = part
                seen.add(n)
            s_ref[...] = state * cdec + kv_outer(k_ref[rows, :], kdec, v_ref[rows, :])

    gain = gain_ref[...]
    for n in range(n_all):
        rows = slice(n * c, (n + 1) * c)
        scores = _dot_nt(q_ref[rows, :], k_ref[rows, :]) * mask
        o = _dot(scores.astype(BF16), v_ref[rows, :]) + cross_scr[rows, :]
        mu = jnp.mean(o, axis=-1, keepdims=True)
        dev = o - mu
        on = dev * lax.rsqrt(jnp.mean(dev * dev, axis=-1, keepdims=True) + EPS)
        y_ref[rows, :] = (on * gain * sg_ref[rows, :].astype(F32)).astype(y_ref.dtype)


def _ret_scan(q, k, v, sg, log1m_gamma, gn_gain, n_ctx):
    b, t, _ = q.shape
    l1g = jnp.broadcast_to(log1m_gamma.T[:, :, None], (RET_HEADS, 2, LANES))
    l1g = jnp.concatenate([l1g, jnp.zeros((RET_HEADS, 6, LANES), F32) - 1.0], axis=1)
    kern = functools.partial(_ret_scan_kernel, n_ctx=n_ctx // SCAN_CHUNK, n_all=t // SCAN_CHUNK)
    return pl.pallas_call(
        kern,
        grid=(b, RET_HEADS),
        in_specs=[pl.BlockSpec((None, 8, LANES), lambda i, h: (h, 0, 0)),
                  pl.BlockSpec((None, t, RET_DK), lambda i, h: (i, 0, h)),
                  pl.BlockSpec((None, t, RET_DK), lambda i, h: (i, 0, h)),
                  pl.BlockSpec((None, t, RET_DV), lambda i, h: (i, 0, h)),
                  pl.BlockSpec((None, t, RET_DV), lambda i, h: (i, 0, h)),
                  pl.BlockSpec((1, RET_DV), lambda i, h: (0, h))],
        out_specs=pl.BlockSpec((None, t, RET_DV), lambda i, h: (i, 0, h)),
        out_shape=jax.ShapeDtypeStruct((b, t, RET_HEADS * RET_DV), BF16),
        scratch_shapes=[pltpu.VMEM((t, RET_DV), F32), pltpu.VMEM((2, RET_DK, RET_DV), F32)],
        compiler_params=_cparams("arbitrary", "arbitrary"),
        name="retention_scan",
    )(l1g, q, k, v, sg, gn_gain.reshape(1, -1))


CHUNKS_PER_TILE = TOKEN_TILE // SCAN_CHUNK


def _chunk_major_spec(g, tile0=0):
    return pl.BlockSpec((g, CHUNKS_PER_TILE, None, S5_GROUP, SCAN_CHUNK), lambda i, j: (0, j + tile0, i, 0, 0))


def _ffn_s5_in_kernel(r_ref, mod_ref, g_ref, win_ref, wout_ref, g1_ref, o_ref, u_ref):
    x = _swiglu_step(r_ref[...], mod_ref, g_ref, win_ref, wout_ref, 0)
    o_ref[...] = x
    xn = _pre_norm(x, g1_ref[...], mod_ref, 1)
    c = SCAN_CHUNK
    for ci in range(CHUNKS_PER_TILE):
        u_ref[:, ci] = xn[ci * c:(ci + 1) * c, :].T.reshape(u_ref.shape[0], S5_GROUP, c)


def _ffn_s5_in(r, mod, gain, w_in, w_out, mixer_gain, n_ctx_tiles):
    b, t, d = r.shape
    g = d // S5_GROUP
    consts = [gain.reshape(1, d), w_in, w_out, mixer_gain.reshape(1, d)]
    return pl.pallas_call(
        _ffn_s5_in_kernel,
        grid=(b, t // TOKEN_TILE),
        in_specs=[_rows_spec(d), _mod_spec(d, n_ctx_tiles)] + [_resident(a.shape) for a in consts],
        out_specs=[_rows_spec(d), _chunk_major_spec(g)],
        out_shape=[jax.ShapeDtypeStruct((b, t, d), F32),
                   jax.ShapeDtypeStruct((g, t // SCAN_CHUNK, b, S5_GROUP, SCAN_CHUNK), F32)],
        compiler_params=_cparams("arbitrary", "arbitrary"),
        name="swiglu_s5_in",
    )(r, mod, *consts)


def _s5_disc(lam_re, lam_im, log_dt):
    dt = jnp.exp(log_dt)
    a_re = lam_re * dt
    a_im = lam_im * dt
    mag = jnp.exp(a_re)
    e_re = mag * jnp.cos(a_im) - 1.0
    e_im = mag * jnp.sin(a_im)
    den = lam_re * lam_re + lam_im * lam_im
    return a_re, a_im, (e_re * lam_re + e_im * lam_im) / den, (e_im * lam_re - e_re * lam_im) / den


def _cmul(ar, ai, br, bi):
    return ar * br - ai * bi, ar * bi + ai * br


def _s5_kernel(u_ref, lamc_ref, lamr_ref, lamr2_ref, bc_ref, br_ref, ct_ref, d_ref,
               y_ref, t_scr, x_scr, w_scr, *, batch, n_ctx, n_all):
    c, m, p = SCAN_CHUNK, S5_GROUP, S5_STATE
    cm = c * m
    rows = n_all * batch
    for mi in range(m):
        x_scr[:, mi * c:(mi + 1) * c] = u_ref[pl.ds(mi, rows, stride=m), :].astype(BF16)

    kk = lax.broadcasted_iota(jnp.int32, (1, LANES), 1).astype(F32)

    def powers(a_re, a_im, expo):
        mag = jnp.exp(a_re * expo)
        return mag * jnp.cos(a_im * expo), mag * jnp.sin(a_im * expo)

    def lane_tiles(fn):
        parts = [fn(i) for i in range(m)]
        return (jnp.concatenate([q[0] for q in parts], axis=1), jnp.concatenate([q[1] for q in parts], axis=1))

    disc = [_s5_disc(lamc_ref[d][:, 0:1], lamc_ref[d][:, 1:2], lamc_ref[d][:, 2:3]) for d in range(2)]
    bbar = [_cmul(disc[d][2], disc[d][3], bc_ref[d, 0], bc_ref[d, 1]) for d in range(2)]
    ctr = [(ct_ref[d, 0], ct_ref[d, 1]) for d in range(2)]

    def col(pair, i):
        return pair[0][:, i:i + 1], pair[1][:, i:i + 1]

    f_up = powers(disc[0][0], disc[0][1], kk + 1.0)
    f_down = powers(disc[0][0], disc[0][1], (c - 1.0) - kk)
    b_up = powers(disc[1][0], disc[1][1], kk)
    b_down = powers(disc[1][0], disc[1][1], (c - 1.0) - kk)
    b_down1 = powers(disc[1][0], disc[1][1], c - kk)
    lag0 = jnp.where(kk == c - 1.0, 1.0, 0.0)

    q0f = lane_tiles(lambda mo: (col(ctr[0], mo)[0] * lag0, col(ctr[0], mo)[1] * lag0))
    q1f = lane_tiles(lambda mo: _cmul(*col(ctr[0], mo), *f_up))
    q0b = lane_tiles(lambda mo: _cmul(*col(ctr[1], mo), *b_down))
    qrb = lane_tiles(lambda mo: _cmul(*col(ctr[1], mo), *b_down1))

    hp = lax.Precision.HIGHEST
    btr = []
    for d in range(2):
        lam = lamr_ref[d]
        _, _, cf_re, cf_im = _s5_disc(lam[0:1, :], lam[1:2, :], lam[2:3, :])
        btr.append(_cmul(cf_re, cf_im, br_ref[d, 0], br_ref[d, 1]))

    def gen(bt, q):
        return (jnp.dot(bt[0], q[0], precision=hp, preferred_element_type=F32)
                - jnp.dot(bt[1], q[1], precision=hp, preferred_element_type=F32))

    w0 = gen(btr[0], q0f) + gen(btr[1], q0b)
    w1 = gen(btr[0], q1f)
    for mo in range(m):
        w_scr[:, 2 * mo * c:(2 * mo + 1) * c] = w0[:, mo * c:(mo + 1) * c]
        w_scr[:, (2 * mo + 1) * c:(2 * mo + 2) * c] = w1[:, mo * c:(mo + 1) * c]

    for mi in range(m):
        row = jnp.broadcast_to(w_scr[mi:mi + 1, :], (c, 2 * cm))
        win = pltpu.roll(row, 2 * cm - (c - 1), 1, stride=1, stride_axis=0)
        blk = jnp.concatenate([win[:, 2 * mo * c:(2 * mo + 1) * c] for mo in range(m)], axis=1)
        t_scr[mi * c:(mi + 1) * c, :] = blk.astype(BF16)

    for d, q in ((0, q1f), (1, qrb)):
        base = cm + 2 * p * d
        t_scr[base:base + p, :] = q[0].astype(BF16)
        t_scr[base + p:base + 2 * p, :] = (-q[1]).astype(BF16)

    wb_f = lane_tiles(lambda mi: _cmul(*col(bbar[0], mi), *f_down))
    wb_b = lane_tiles(lambda mi: _cmul(*col(bbar[1], mi), *b_up))
    wb = jnp.concatenate([wb_f[0], wb_f[1], wb_b[0], wb_b[1]], axis=0).astype(BF16)
    hloc = _dot_nt(x_scr[:, :cm], wb)

    sign = jnp.where(lax.broadcasted_iota(jnp.int32, (1, 2 * p), 1) < p, -1.0, 1.0)
    order_f = list(range(n_all))
    order_b = list(range(n_ctx - 1, -1, -1)) + list(range(n_all - 1, n_ctx - 1, -1))
    for d, order in ((0, order_f), (1, order_b)):
        lam = lamr2_ref[d]
        a_re, a_im, _, _ = _s5_disc(lam[0:1, :], lam[1:2, :], lam[2:3, :])
        mag = jnp.exp(a_re * c)
        ac_r = mag * jnp.cos(a_im * c)
        ac_i = mag * jnp.sin(a_im * c) * sign
        state = jnp.zeros((batch, 2 * p), F32)
        colx = cm + 2 * p * d
        for n in order:
            x_scr[n * batch:(n + 1) * batch, colx:colx + 2 * p] = state.astype(BF16)
            local = hloc[n * batch:(n + 1) * batch, 2 * p * d:2 * p * (d + 1)]
            state = state * ac_r + pltpu.roll(state, p, 1) * ac_i + local

    y = _dot(x_scr[...], t_scr[...])
    for mo in range(m):
        skip = d_ref[mo:mo + 1, :] * u_ref[pl.ds(mo, rows, stride=m), :]
        y_ref[pl.ds(mo, rows, stride=m), :] = y[:, mo * c:(mo + 1) * c] + skip


def _s5_operands(u, lam_re, lam_im, log_dt, b_re, b_im, c_re, c_im, d_skip):
    g, n_all, batch, m, c = u.shape
    dtb = jnp.broadcast_to(log_dt[:, :, None], lam_re.shape)
    lam3 = jnp.stack([lam_re, lam_im, dtb], axis=-1)
    lamc = jnp.pad(lam3, ((0, 0), (0, 0), (0, 0), (0, LANES - 3))).transpose(1, 0, 2, 3)
    lam3r = jnp.stack([lam_re, lam_im, dtb], axis=2)
    lamr = jnp.pad(lam3r, ((0, 0), (0, 0), (0, 5), (0, 0))).transpose(1, 0, 2, 3)
    lamr2 = jnp.concatenate([lamr, lamr], axis=-1)
    bcol = jnp.stack([b_re, b_im], axis=1)
    bc = jnp.pad(bcol, ((0, 0),) * 4 + ((0, LANES - m),)).transpose(2, 0, 1, 3, 4)
    brow = jnp.swapaxes(bcol, -1, -2).transpose(2, 0, 1, 3, 4)
    ctc = jnp.swapaxes(jnp.stack([c_re, c_im], axis=1), -1, -2)
    ctc = jnp.pad(ctc, ((0, 0),) * 4 + ((0, LANES - m),)).transpose(2, 0, 1, 3, 4)
    dvec = jnp.broadcast_to(d_skip.reshape(g, m, 1), (g, m, LANES))
    return (u.reshape(g, n_all * batch * m, c), lamc, lamr, lamr2, bc, brow, ctc, dvec)


def _s5_mixer(u, lam_re, lam_im, log_dt, b_re, b_im, c_re, c_im, d_skip, n_ctx):
    g, n_all, b, m, c = u.shape
    p = S5_STATE
    rows = n_all * b
    cm = c * m
    kern = functools.partial(_s5_kernel, batch=b, n_ctx=n_ctx // c, n_all=n_all)
    per_g = lambda *tail: pl.BlockSpec((None,) + tail, lambda i: (i,) + (0,) * len(tail))
    y = pl.pallas_call(
        kern,
        grid=(g,),
        in_specs=[per_g(rows * m, c), per_g(2, p, LANES), per_g(2, 8, p), per_g(2, 8, 2 * p),
                  per_g(2, 2, p, LANES), per_g(2, 2, m, p), per_g(2, 2, p, LANES), per_g(m, LANES)],
        out_specs=per_g(rows * m, c),
        out_shape=jax.ShapeDtypeStruct((g, rows * m, c), F32),
        scratch_shapes=[pltpu.VMEM((cm + 4 * p, cm), BF16), pltpu.VMEM((rows, cm + 4 * p), BF16),
                        pltpu.VMEM((m, 2 * cm), F32)],
        compiler_params=_cparams("arbitrary"),
        name="s5_scan",
    )(*_s5_operands(u, lam_re, lam_im, log_dt, b_re, b_im, c_re, c_im, d_skip))
    return y.reshape(u.shape)


def kernel(x, c, ctx, c_ctx, ada_w, ada_b, norm_g, ffn_w_in, ffn_w_out, mla_w_in, mla_q_norm, mla_kv_norm,
           mla_w_uq, mla_w_ukv, mla_q_gain, mla_k_gain, mla_w_o, ret_w_in, ret_log1m_gamma, ret_gn_gain, ret_w_o,
           s5_lam_re, s5_lam_im, s5_log_dt, s5_b_re, s5_b_im, s5_c_re, s5_c_im, s5_d, s5_glu_w, s5_glu_b):
    b, s, d = x.shape
    n_ctx = ctx.shape[1]
    depth = ada_w.shape[0]
    assert n_ctx % TOKEN_TILE == 0 and s % TOKEN_TILE == 0 and s % GRID_W == 0
    n_ctx_tiles = n_ctx // TOKEN_TILE

    pad = (-(b + 1)) % 8
    cvec = jnp.concatenate([c, c_ctx[None, :], jnp.zeros((pad, d), F32)], axis=0)
    mod_all = _modulation(cvec, ada_w, ada_b)
    mod_lat = mod_all[:, :b].reshape(depth, b, 1, N_MOD, d)
    mod_ctx = jnp.broadcast_to(mod_all[:, b].reshape(depth, 1, 1, N_MOD, d), (depth, b, 1, N_MOD, d))
    mods = jnp.concatenate([mod_ctx, mod_lat], axis=2)

    mla_tabs = _mla_tables(n_ctx, s)
    ret_tabs = _ret_tables(n_ctx, s)

    r = None
    for i in range(depth):
        kind, j = i % N_MIXERS, i // N_MIXERS
        mod = mods[i]
        ffn0 = (norm_g[i, 0], ffn_w_in[i, 0].astype(BF16), ffn_w_out[i, 0].astype(BF16))
        ffn1 = (norm_g[i, 2], ffn_w_in[i, 1].astype(BF16), ffn_w_out[i, 1].astype(BF16))
        if i == 0:
            assert kind != 2
            r = _ffn_first(ctx, x, mod, *ffn0, n_ctx_tiles)
        elif kind == 2:
            r, u = _ffn_s5_in(r, mod, *ffn0, norm_g[i, 1], n_ctx_tiles)
        else:
            r = _ffn(r, mod, *ffn0, n_ctx_tiles)
        tile0 = n_ctx_tiles if i == depth - 1 else 0
        if kind == 0:
            weights = _mla_weights(mla_w_in[j], mla_q_norm[j], mla_kv_norm[j], mla_w_uq[j], mla_w_ukv[j],
                                   mla_q_gain[j], mla_k_gain[j])
            q, k, v = _mla_proj(r, *mla_tabs, mod, norm_g[i, 1], weights, n_ctx_tiles)
            y = _attention(q, k, v, _attn_score_bound(mla_q_gain[j], mla_k_gain[j]), n_ctx)
            mix = (y, _rows_spec(y.shape[-1], tile0), [mla_w_o[j].astype(BF16)], _ffn_proj_kernel, "mla_out_swiglu")
        elif kind == 1:
            q, k, v, sg = _ret_proj(r, *ret_tabs, mod, norm_g[i, 1], ret_w_in[j].astype(BF16), n_ctx_tiles)
            y = _ret_scan(q, k, v, sg, ret_log1m_gamma[j], ret_gn_gain[j], n_ctx)
            mix = (y, _rows_spec(y.shape[-1], tile0), [ret_w_o[j].astype(BF16)], _ffn_proj_kernel, "ret_out_swiglu")
        else:
            y = _s5_mixer(u, s5_lam_re[j], s5_lam_im[j], s5_log_dt[j], s5_b_re[j], s5_b_im[j],
                          s5_c_re[j], s5_c_im[j], s5_d[j], n_ctx)
            mix = (y, _chunk_major_spec(y.shape[0], tile0), [s5_glu_w[j].astype(BF16), s5_glu_b[j].reshape(1, -1)],
                   _ffn_glu_kernel, "s5_out_swiglu")
        r = _ffn_after_mixer(r, *mix, mod, *ffn1, n_ctx_tiles, tile0)
    return r
```

```python
import functools
import math

import jax
import jax.numpy as jnp
import numpy as np
from jax import lax
from jax.experimental import pallas as pl
from jax.experimental.pallas import tpu as pltpu

F32 = jnp.float32
BF16 = jnp.bfloat16

EPS = 1e-6
ROPE_BASE = 10000.0
GRID_W = 64
N_MIXERS = 3
N_MOD = 9

MLA_HEADS = 8
MLA_NOPE = 128
MLA_ROPE = 64
MLA_V = 128
MLA_Q_RANK = 384
MLA_KV_RANK = 256
MLA_HEAD_PAD = 256

RET_HEADS = 4
RET_DK = 256
RET_DV = 512
SCAN_CHUNK = 128

S5_GROUP = 16
S5_STATE = 64

LANES = 128
TOKEN_TILE = 256
ATTN_HEADS_PER_STEP = 4
FFN_BATCH_PER_STEP = 2
VMEM_LIMIT_BYTES = 56 * 1024 * 1024


def _cparams(*sem):
    return pltpu.CompilerParams(dimension_semantics=sem, vmem_limit_bytes=VMEM_LIMIT_BYTES)


def _resident(shape, single_buffer=False):
    zeros = (0,) * len(shape)
    if single_buffer:
        return pl.BlockSpec(shape, lambda *_: zeros, pipeline_mode=pl.Buffered(1))
    return pl.BlockSpec(shape, lambda *_: zeros)


def _silu(x):
    return x * jax.nn.sigmoid(x)


def _rms(x, n):
    return x * lax.rsqrt(jnp.sum(x * x, axis=-1, keepdims=True) * (1.0 / n) + EPS)


def _pre_norm(x, gain, mod_ref, k):
    y = _rms(x, x.shape[-1]) * gain
    return y * (1.0 + mod_ref[3 * k + 1:3 * k + 2, :]) + mod_ref[3 * k:3 * k + 1, :]


def _dot(a, b):
    return jnp.dot(a, b, preferred_element_type=F32)


def _dot_nt(a, b):
    return lax.dot_general(a, b, (((1,), (1,)), ((), ())), preferred_element_type=F32)


def _mod_kernel(cv_ref, w_ref, b_ref, o_ref):
    s = _silu(cv_ref[...]).astype(BF16)
    o_ref[...] = _dot(s, w_ref[...].astype(BF16)) + b_ref[...]


def _modulation(cvec, ada_w, ada_b):
    depth, d, n = ada_w.shape
    rows = cvec.shape[0]
    tn = n // 4
    return pl.pallas_call(
        _mod_kernel,
        grid=(depth, n // tn),
        in_specs=[
            _resident((rows, d)),
            pl.BlockSpec((None, d, tn), lambda i, j: (i, 0, j)),
            pl.BlockSpec((None, 1, tn), lambda i, j: (i, 0, j)),
        ],
        out_specs=pl.BlockSpec((None, rows, tn), lambda i, j: (i, 0, j)),
        out_shape=jax.ShapeDtypeStruct((depth, rows, n), F32),
        compiler_params=_cparams("arbitrary", "arbitrary"),
        name="adaln_modulation",
    )(cvec, ada_w, ada_b.reshape(depth, 1, n))


def _rows_spec(width, tile0=0):
    return pl.BlockSpec((FFN_BATCH_PER_STEP, TOKEN_TILE, width), lambda i, j: (i, j + tile0, 0))


def _mod_spec(d, n_ctx_tiles, tile0=0):
    return pl.BlockSpec((FFN_BATCH_PER_STEP, None, N_MOD, d),
                        lambda i, j: (i, (j + tile0 >= n_ctx_tiles).astype(jnp.int32), 0, 0))


def _row_call(kernel, name, b, n_tiles, inputs, consts, outs):
    in_specs = [spec for _, spec in inputs] + [_resident(a.shape, single_buffer=True) for a in consts]
    return pl.pallas_call(
        kernel,
        grid=(b // FFN_BATCH_PER_STEP, n_tiles),
        in_specs=in_specs,
        out_specs=[_rows_spec(w) for w, _ in outs],
        out_shape=[jax.ShapeDtypeStruct((b, n_tiles * TOKEN_TILE, w), dt) for w, dt in outs],
        compiler_params=_cparams("arbitrary", "arbitrary"),
        name=name,
    )(*[a for a, _ in inputs], *consts)


def _gelu_tanh(x):
    cdf = 0.5 * (1.0 + jnp.tanh(math.sqrt(2.0 / math.pi) * (x + 0.044715 * (x * x * x))))
    return x * cdf


def _swiglu_step(x, mod_ref, g_ref, win_ref, wout_ref, k):
    xn = _pre_norm(x, g_ref[...], mod_ref, k).astype(BF16)
    f = wout_ref.shape[0]
    a = _dot(xn, win_ref[:, :f])
    b = _dot(xn, win_ref[:, f:])
    hm = (_silu(a) * b).astype(BF16)
    return x + (0.5 * mod_ref[3 * k + 2:3 * k + 3, :]) * _dot(hm, wout_ref[...])


def _ffn_first_kernel(ctx_ref, x_ref, mod_ref, g_ref, win_ref, wout_ref, o_ref, *, n_ctx_tiles):
    for e in range(FFN_BATCH_PER_STEP):
        x = jnp.where(pl.program_id(1) < n_ctx_tiles, ctx_ref[e], x_ref[e])
        o_ref[e] = _swiglu_step(x, mod_ref.at[e], g_ref, win_ref, wout_ref, 0)


def _ffn_kernel(r_ref, mod_ref, g_ref, win_ref, wout_ref, o_ref):
    for e in range(FFN_BATCH_PER_STEP):
        o_ref[e] = _swiglu_step(r_ref[e], mod_ref.at[e], g_ref, win_ref, wout_ref, 0)


def _ffn_proj_kernel(r_ref, y_ref, mod_ref, pw_ref, g_ref, win_ref, wout_ref, o_ref):
    for e in range(FFN_BATCH_PER_STEP):
        mod = mod_ref.at[e]
        x = r_ref[e] + mod[5:6, :] * _dot(y_ref[e], pw_ref[...])
        o_ref[e] = _swiglu_step(x, mod, g_ref, win_ref, wout_ref, 2)


def _ffn_glu_kernel(r_ref, y_ref, mod_ref, pw_ref, pb_ref, g_ref, win_ref, wout_ref, o_ref):
    d = r_ref.shape[-1]
    c = SCAN_CHUNK
    for e in range(FFN_BATCH_PER_STEP):
        mod = mod_ref.at[e]
        y = jnp.concatenate([y_ref[:, ci, e].reshape(d, c).T for ci in range(TOKEN_TILE // c)], axis=0)
        ag = _dot(_gelu_tanh(y).astype(BF16), pw_ref[...]) + pb_ref[...]
        x = r_ref[e] + mod[5:6, :] * (ag[:, :d] * jax.nn.sigmoid(ag[:, d:]))
        o_ref[e] = _swiglu_step(x, mod, g_ref, win_ref, wout_ref, 2)


def _ffn_first(ctx, x, mod, gain, w_in, w_out, n_ctx_tiles):
    b, s, d = x.shape
    last_ctx = n_ctx_tiles - 1
    eb = FFN_BATCH_PER_STEP
    inputs = [(ctx, pl.BlockSpec((eb, TOKEN_TILE, d), lambda i, j: (i, jnp.minimum(j, last_ctx), 0))),
              (x, pl.BlockSpec((eb, TOKEN_TILE, d), lambda i, j: (i, jnp.maximum(j - n_ctx_tiles, 0), 0))),
              (mod, _mod_spec(d, n_ctx_tiles))]
    (out,) = _row_call(functools.partial(_ffn_first_kernel, n_ctx_tiles=n_ctx_tiles), "swiglu_first",
                       b, n_ctx_tiles + s // TOKEN_TILE, inputs, [gain.reshape(1, d), w_in, w_out], [(d, F32)])
    return out


def _ffn(r, mod, gain, w_in, w_out, n_ctx_tiles):
    b, t, d = r.shape
    inputs = [(r, _rows_spec(d)), (mod, _mod_spec(d, n_ctx_tiles))]
    (out,) = _row_call(_ffn_kernel, "swiglu_half_step", b, t // TOKEN_TILE, inputs,
                       [gain.reshape(1, d), w_in, w_out], [(d, F32)])
    return out


def _ffn_after_mixer(r, y, y_spec, mix_consts, kern, name, mod, gain, w_in, w_out, n_ctx_tiles, tile0):
    b, t, d = r.shape
    inputs = [(r, _rows_spec(d, tile0)), (y, y_spec), (mod, _mod_spec(d, n_ctx_tiles, tile0))]
    (out,) = _row_call(kern, name, b, t // TOKEN_TILE - tile0, inputs,
                       mix_consts + [gain.reshape(1, d), w_in, w_out], [(d, F32)])
    return out


def _rope_tile_perm():
    src = np.full((LANES,), -1, np.int64)
    src[0:16] = np.arange(0, 16)
    src[16:32] = np.arange(32, 48)
    src[64:80] = np.arange(16, 32)
    src[80:96] = np.arange(48, 64)
    return src


def _pad_rope_cols(w):
    src = _rope_tile_perm()
    cols = jnp.take(w, jnp.asarray(np.maximum(src, 0)), axis=-1)
    return jnp.where(jnp.asarray(src >= 0), cols, 0.0)


def _mla_tables(n_ctx, n_lat):
    half = MLA_ROPE // 2
    inv = ROPE_BASE ** (-jnp.arange(0, half, 2, dtype=F32) / half)
    pos = jnp.arange(n_lat)
    ang_r = (pos // GRID_W).astype(F32)[:, None] * inv[None, :]
    ang_c = (pos % GRID_W).astype(F32)[:, None] * inv[None, :]
    z = jnp.zeros((n_lat, 32), F32)
    cos = jnp.concatenate([jnp.cos(ang_r), jnp.cos(ang_c), z, jnp.cos(ang_r), jnp.cos(ang_c), z], axis=-1)
    sin = jnp.concatenate([-jnp.sin(ang_r), -jnp.sin(ang_c), z, jnp.sin(ang_r), jnp.sin(ang_c), z], axis=-1)
    cos = jnp.concatenate([jnp.ones((n_ctx, LANES), F32), cos], axis=0)
    sin = jnp.concatenate([jnp.zeros((n_ctx, LANES), F32), sin], axis=0)
    return cos, sin


def _mla_proj_kernel(r_ref, cos_ref, sin_ref, mod_ref, g_ref, win_ref, qn_ref, kvn_ref, wuq_ref, wukv_ref,
                     qg_ref, kg_ref, q_ref, k_ref, v_ref):
    xn = _pre_norm(r_ref[...], g_ref[...], mod_ref, 1).astype(BF16)
    p = _dot(xn, win_ref[...])
    cos = cos_ref[...]
    sin = sin_ref[...]

    def rotate(x):
        return x * cos + pltpu.roll(x, LANES // 2, 1) * sin

    cq = (_rms(p[:, :MLA_Q_RANK], MLA_Q_RANK) * qn_ref[...]).astype(BF16)
    ckv = (_rms(p[:, MLA_Q_RANK:MLA_Q_RANK + MLA_KV_RANK], MLA_KV_RANK) * kvn_ref[...]).astype(BF16)
    kr = rotate(_rms(p[:, MLA_Q_RANK + MLA_KV_RANK:], MLA_ROPE) * kg_ref[:, LANES:]).astype(BF16)
    q = _dot(cq, wuq_ref[...])
    kv = _dot(ckv, wukv_ref[...])
    scale = (MLA_NOPE + MLA_ROPE) ** -0.5 * math.log2(math.e)
    ii = lax.broadcasted_iota(jnp.int32, (2 * LANES, 2 * LANES), 0)
    jj = lax.broadcasted_iota(jnp.int32, (2 * LANES, 2 * LANES), 1)
    same = (ii // LANES) == (jj // LANES)
    mean_q = jnp.where(same, jnp.where(ii < LANES, 1.0 / MLA_NOPE, 1.0 / MLA_ROPE), 0.0).astype(BF16)
    mean_k = jnp.where(same, 1.0 / MLA_NOPE, 0.0).astype(BF16)
    q_gain = qg_ref[...] * scale
    for h in range(MLA_HEADS):
        a0 = h * MLA_HEAD_PAD
        qh = q[:, a0:a0 + 2 * LANES]
        qn = qh * lax.rsqrt(_dot((qh * qh).astype(BF16), mean_q) + EPS) * q_gain
        q_ref[:, a0:a0 + LANES] = qn[:, :LANES].astype(BF16)
        q_ref[:, a0 + LANES:a0 + 2 * LANES] = rotate(qn[:, LANES:]).astype(BF16)
        k_ref[:, a0 + LANES:a0 + 2 * LANES] = kr
    for hh in range(MLA_HEADS // 2):
        kh = kv[:, 2 * hh * LANES:(2 * hh + 2) * LANES]
        kn = kh * lax.rsqrt(_dot((kh * kh).astype(BF16), mean_k) + EPS)
        for e in range(2):
            a0 = (2 * hh + e) * MLA_HEAD_PAD
            k_ref[:, a0:a0 + LANES] = (kn[:, e * LANES:(e + 1) * LANES] * kg_ref[:, :LANES]).astype(BF16)
    v_ref[...] = kv[:, MLA_HEADS * LANES:].astype(BF16)


def _mla_weights(w_in, q_norm, kv_norm, w_uq, w_ukv, q_gain, k_gain):
    lat = MLA_Q_RANK + MLA_KV_RANK
    w_in_p = jnp.concatenate([w_in[:, :lat], _pad_rope_cols(w_in[:, lat:])], axis=-1).astype(BF16)
    uq = w_uq.reshape(MLA_Q_RANK, MLA_HEADS, MLA_NOPE + MLA_ROPE)
    uq = jnp.concatenate([uq[..., :MLA_NOPE], _pad_rope_cols(uq[..., MLA_NOPE:])], axis=-1)
    uq = uq.reshape(MLA_Q_RANK, MLA_HEADS * MLA_HEAD_PAD).astype(BF16)
    ukv = w_ukv.reshape(MLA_KV_RANK, MLA_HEADS, MLA_NOPE + MLA_V)
    ukv = jnp.concatenate([ukv[..., :MLA_NOPE].reshape(MLA_KV_RANK, -1),
                           ukv[..., MLA_NOPE:].reshape(MLA_KV_RANK, -1)], axis=-1).astype(BF16)
    qg = jnp.concatenate([q_gain[:MLA_NOPE], _pad_rope_cols(q_gain[MLA_NOPE:])]).reshape(1, -1)
    kg = jnp.concatenate([k_gain[:MLA_NOPE], _pad_rope_cols(k_gain[MLA_NOPE:])]).reshape(1, -1)
    return [w_in_p, q_norm.reshape(1, -1), kv_norm.reshape(1, -1), uq, ukv, qg, kg]


def _mla_proj(r, cos, sin, mod, gain, weights, n_ctx_tiles):
    b, t, d = r.shape
    cos_b = jnp.broadcast_to(cos[None], (1,) + cos.shape)
    sin_b = jnp.broadcast_to(sin[None], (1,) + sin.shape)
    tt = TOKEN_TILE
    hw = MLA_HEADS * MLA_HEAD_PAD
    consts = [gain.reshape(1, d)] + weights
    in_specs = [pl.BlockSpec((None, tt, d), lambda i, j: (i, j, 0)),
                pl.BlockSpec((None, tt, LANES), lambda i, j: (0, j, 0)),
                pl.BlockSpec((None, tt, LANES), lambda i, j: (0, j, 0)),
                pl.BlockSpec((None, None, N_MOD, d),
                             lambda i, j: (i, (j >= n_ctx_tiles).astype(jnp.int32), 0, 0))]
    in_specs += [_resident(a.shape) for a in consts]
    widths = [(hw, BF16), (hw, BF16), (MLA_HEADS * MLA_V, BF16)]
    return pl.pallas_call(
        _mla_proj_kernel,
        grid=(b, t // tt),
        in_specs=in_specs,
        out_specs=[pl.BlockSpec((None, tt, w), lambda i, j: (i, j, 0)) for w, _ in widths],
        out_shape=[jax.ShapeDtypeStruct((b, t, w), dt) for w, dt in widths],
        compiler_params=_cparams("arbitrary", "arbitrary"),
        name="mla_projections",
    )(r, cos_b, sin_b, mod, *consts)


ATTN_MIN_ROW_SUM = 2.0 ** -100


def _attn_kernel(bound_ref, q_ref, k_ref, v_ref, o_ref, *, n_ctx_tiles, n_ctx):
    def attend(n_keys, use_bound):
        low = None
        for h in range(ATTN_HEADS_PER_STEP):
            qk = slice(h * MLA_HEAD_PAD, (h + 1) * MLA_HEAD_PAD)
            vo = slice(h * MLA_V, (h + 1) * MLA_V)
            s = _dot_nt(q_ref[:, qk], k_ref[:n_keys, qk])
            shift = bound_ref[0:1, 0:1] if use_bound else jnp.max(s, axis=-1, keepdims=True)
            p = jnp.exp2(s - shift)
            l = jnp.sum(p, axis=-1, keepdims=True)
            o = _dot(p.astype(BF16), v_ref[:n_keys, vo])
            o_ref[:, vo] = (o / l).astype(o_ref.dtype)
            low = l if low is None else jnp.minimum(low, l)
        return low

    def attend_keys(n_keys):
        low = attend(n_keys, True)
        safe = jnp.min(low) >= ATTN_MIN_ROW_SUM

        @pl.when(jnp.logical_not(safe))
        def _():
            attend(n_keys, False)

    is_ctx = pl.program_id(2) < n_ctx_tiles
    pl.when(is_ctx)(lambda: attend_keys(n_ctx))
    pl.when(jnp.logical_not(is_ctx))(lambda: attend_keys(k_ref.shape[0]))


def _attn_score_bound(q_gain, k_gain):
    scale = (MLA_NOPE + MLA_ROPE) ** -0.5 * math.log2(math.e)
    gq, gk = jnp.abs(q_gain), jnp.abs(k_gain)
    dot_max = (MLA_NOPE * jnp.max(gq[:MLA_NOPE]) * jnp.max(gk[:MLA_NOPE])
               + MLA_ROPE * jnp.max(gq[MLA_NOPE:]) * jnp.max(gk[MLA_NOPE:]))
    return jnp.full((8, LANES), 1.01 * scale, F32) * dot_max


def _attention(q, k, v, bound, n_ctx):
    b, t, _ = q.shape
    tq = TOKEN_TILE
    hp = ATTN_HEADS_PER_STEP
    kern = functools.partial(_attn_kernel, n_ctx_tiles=n_ctx // tq, n_ctx=n_ctx)
    return pl.pallas_call(
        kern,
        grid=(b, MLA_HEADS // hp, t // tq),
        in_specs=[_resident(bound.shape),
                  pl.BlockSpec((None, tq, hp * MLA_HEAD_PAD), lambda i, h, j: (i, j, h)),
                  pl.BlockSpec((None, t, hp * MLA_HEAD_PAD), lambda i, h, j: (i, 0, h)),
                  pl.BlockSpec((None, t, hp * MLA_V), lambda i, h, j: (i, 0, h))],
        out_specs=pl.BlockSpec((None, tq, hp * MLA_V), lambda i, h, j: (i, j, h)),
        out_shape=jax.ShapeDtypeStruct((b, t, MLA_HEADS * MLA_V), BF16),
        compiler_params=_cparams("arbitrary", "arbitrary", "arbitrary"),
        name="mla_attention",
    )(bound, q, k, v)


def _ret_tables(n_ctx, n_lat):
    inv = ROPE_BASE ** (-jnp.arange(0, RET_DK, 2, dtype=F32) / RET_DK)
    ang = jnp.arange(n_lat, dtype=F32)[:, None] * inv[None, :]
    cos = jnp.concatenate([jnp.ones((n_ctx, LANES), F32), jnp.cos(ang)], axis=0)
    sin = jnp.concatenate([jnp.zeros((n_ctx, LANES), F32), jnp.sin(ang)], axis=0)
    return cos, sin


def _ret_proj_kernel(r_ref, cos_ref, sin_ref, mod_ref, g_ref, win_ref, q_ref, k_ref, v_ref, sg_ref):
    xn = _pre_norm(r_ref[...], g_ref[...], mod_ref, 1).astype(BF16)
    cos = cos_ref[...]
    sin = sin_ref[...]
    hdk = RET_HEADS * RET_DK
    hdv = RET_HEADS * RET_DV
    scale = RET_DK ** -0.5
    for out_ref, base, sc in ((q_ref, 0, None), (k_ref, hdk, scale)):
        qk = _dot(xn, win_ref[:, base:base + hdk])
        for h in range(RET_HEADS):
            c0 = h * RET_DK
            x1 = qk[:, c0:c0 + LANES]
            x2 = qk[:, c0 + LANES:c0 + 2 * LANES]
            o1 = x1 * cos - x2 * sin
            o2 = x2 * cos + x1 * sin
            if sc is not None:
                o1 = o1 * sc
                o2 = o2 * sc
            out_ref[:, c0:c0 + LANES] = o1.astype(BF16)
            out_ref[:, c0 + LANES:c0 + 2 * LANES] = o2.astype(BF16)
    v_ref[...] = _dot(xn, win_ref[:, 2 * hdk:2 * hdk + hdv]).astype(BF16)
    sg_ref[...] = _silu(_dot(xn, win_ref[:, 2 * hdk + hdv:])).astype(BF16)


def _ret_proj(r, cos, sin, mod, gain, w_in, n_ctx_tiles):
    b, t, d = r.shape
    tt = TOKEN_TILE
    consts = [gain.reshape(1, d), w_in]
    in_specs = [pl.BlockSpec((None, tt, d), lambda i, j: (i, j, 0)),
                pl.BlockSpec((tt, LANES), lambda i, j: (j, 0)),
                pl.BlockSpec((tt, LANES), lambda i, j: (j, 0)),
                pl.BlockSpec((None, None, N_MOD, d),
                             lambda i, j: (i, (j >= n_ctx_tiles).astype(jnp.int32), 0, 0))]
    in_specs += [_resident(a.shape) for a in consts]
    widths = [(RET_HEADS * RET_DK, BF16), (RET_HEADS * RET_DK, BF16),
              (RET_HEADS * RET_DV, BF16), (RET_HEADS * RET_DV, BF16)]
    return pl.pallas_call(
        _ret_proj_kernel,
        grid=(b, t // tt),
        in_specs=in_specs,
        out_specs=[pl.BlockSpec((None, tt, w), lambda i, j: (i, j, 0)) for w, _ in widths],
        out_shape=[jax.ShapeDtypeStruct((b, t, w), dt) for w, dt in widths],
        compiler_params=_cparams("arbitrary", "arbitrary"),
        name="retention_projections",
    )(r, cos, sin, mod, *consts)


def _ret_scan_kernel(l1g_ref, q_ref, k_ref, v_ref, sg_ref, gain_ref, y_ref, cross_scr, st_scr, *, n_ctx, n_all):
    c = SCAN_CHUNK
    log_g = jnp.log1p(-jnp.exp(l1g_ref[...]))
    lgf = log_g[0:1, 0:1]
    lgb = log_g[1:2, 0:1]
    ii = lax.broadcasted_iota(jnp.int32, (c, c), 0)
    jj = lax.broadcasted_iota(jnp.int32, (c, c), 1)
    diff = (ii - jj).astype(F32)
    mask = jnp.where(diff >= 0.0, jnp.exp(lgf * jnp.maximum(diff, 0.0)), jnp.exp(lgb * jnp.maximum(-diff, 0.0)))
    ic = lax.broadcasted_iota(jnp.int32, (c, 1), 0).astype(F32)
    qdec_f = jnp.exp(lgf * (ic + 1.0))
    qdec_b = jnp.exp(lgb * (c - ic))
    kdec_f = jnp.exp(lgf * (c - 1.0 - ic))
    kdec_b = jnp.exp(lgb * ic)
    cdec_f = jnp.exp(lgf * c)
    cdec_b = jnp.exp(lgb * c)

    def kv_outer(kc, dec, vc):
        return _dot((kc.astype(F32) * dec).T.astype(BF16), vc)

    sf_scr, sb_scr = st_scr.at[0], st_scr.at[1]
    sf_scr[...] = jnp.zeros_like(sf_scr)
    sb_scr[...] = jnp.zeros_like(sb_scr)
    order_b = list(range(n_ctx - 1, -1, -1)) + list(range(n_all - 1, n_ctx - 1, -1))
    seen = set()
    for nf, nb in zip(range(n_all), order_b):
        for n, s_ref, qdec, kdec, cdec in ((nf, sf_scr, qdec_f, kdec_f, cdec_f), (nb, sb_scr, qdec_b, kdec_b, cdec_b)):
            rows = slice(n * c, (n + 1) * c)
            state = s_ref[...]
            part = _dot((q_ref[rows, :].astype(F32) * qdec).astype(BF16), state.astype(BF16))
            if n in seen:
                cross_scr[rows, :] += part
            else:
                cross_scr[rows, :] = part
                seen.add(n)
            s_ref[...] = state * cdec + kv_outer(k_ref[rows, :], kdec, v_ref[rows, :])

    gain = gain_ref[...]
    for n in range(n_all):
        rows = slice(n * c, (n + 1) * c)
        scores = _dot_nt(q_ref[rows, :], k_ref[rows, :]) * mask
        o = _dot(scores.astype(BF16), v_ref[rows, :]) + cross_scr[rows, :]
        mu = jnp.mean(o, axis=-1, keepdims=True)
        dev = o - mu
        on = dev * lax.rsqrt(jnp.mean(dev * dev, axis=-1, keepdims=True) + EPS)
        y_ref[rows, :] = (on * gain * sg_ref[rows, :].astype(F32)).astype(y_ref.dtype)


def _ret_scan(q, k, v, sg, log1m_gamma, gn_gain, n_ctx):
    b, t, _ = q.shape
    l1g = jnp.broadcast_to(log1m_gamma.T[:, :, None], (RET_HEADS, 2, LANES))
    l1g = jnp.concatenate([l1g, jnp.zeros((RET_HEADS, 6, LANES), F32) - 1.0], axis=1)
    kern = functools.partial(_ret_scan_kernel, n_ctx=n_ctx // SCAN_CHUNK, n_all=t // SCAN_CHUNK)
    return pl.pallas_call(
        kern,
        grid=(b, RET_HEADS),
        in_specs=[pl.BlockSpec((None, 8, LANES), lambda i, h: (h, 0, 0)),
                  pl.BlockSpec((None, t, RET_DK), lambda i, h: (i, 0, h)),
                  pl.BlockSpec((None, t, RET_DK), lambda i, h: (i, 0, h)),
                  pl.BlockSpec((None, t, RET_DV), lambda i, h: (i, 0, h)),
                  pl.BlockSpec((None, t, RET_DV), lambda i, h: (i, 0, h)),
                  pl.BlockSpec((1, RET_DV), lambda i, h: (0, h))],
        out_specs=pl.BlockSpec((None, t, RET_DV), lambda i, h: (i, 0, h)),
        out_shape=jax.ShapeDtypeStruct((b, t, RET_HEADS * RET_DV), BF16),
        scratch_shapes=[pltpu.VMEM((t, RET_DV), F32), pltpu.VMEM((2, RET_DK, RET_DV), F32)],
        compiler_params=_cparams("arbitrary", "arbitrary"),
        name="retention_scan",
    )(l1g, q, k, v, sg, gn_gain.reshape(1, -1))


CHUNKS_PER_TILE = TOKEN_TILE // SCAN_CHUNK


def _chunk_major_spec(g, tile0=0):
    return pl.BlockSpec((g, CHUNKS_PER_TILE, FFN_BATCH_PER_STEP, S5_GROUP, SCAN_CHUNK),
                        lambda i, j: (0, j + tile0, i, 0, 0))


def _ffn_s5_in_kernel(r_ref, mod_ref, g_ref, win_ref, wout_ref, g1_ref, o_ref, u_ref):
    c = SCAN_CHUNK
    for e in range(FFN_BATCH_PER_STEP):
        mod = mod_ref.at[e]
        x = _swiglu_step(r_ref[e], mod, g_ref, win_ref, wout_ref, 0)
        o_ref[e] = x
        xn = _pre_norm(x, g1_ref[...], mod, 1)
        for ci in range(CHUNKS_PER_TILE):
            u_ref[:, ci, e] = xn[ci * c:(ci + 1) * c, :].T.reshape(u_ref.shape[0], S5_GROUP, c)


def _ffn_s5_in(r, mod, gain, w_in, w_out, mixer_gain, n_ctx_tiles):
    b, t, d = r.shape
    g = d // S5_GROUP
    consts = [gain.reshape(1, d), w_in, w_out, mixer_gain.reshape(1, d)]
    return pl.pallas_call(
        _ffn_s5_in_kernel,
        grid=(b // FFN_BATCH_PER_STEP, t // TOKEN_TILE),
        in_specs=[_rows_spec(d), _mod_spec(d, n_ctx_tiles)] + [_resident(a.shape, single_buffer=True) for a in consts],
        out_specs=[_rows_spec(d), _chunk_major_spec(g)],
        out_shape=[jax.ShapeDtypeStruct((b, t, d), F32),
                   jax.ShapeDtypeStruct((g, t // SCAN_CHUNK, b, S5_GROUP, SCAN_CHUNK), F32)],
        compiler_params=_cparams("arbitrary", "arbitrary"),
        name="swiglu_s5_in",
    )(r, mod, *consts)


def _s5_disc(lam_re, lam_im, log_dt):
    dt = jnp.exp(log_dt)
    a_re = lam_re * dt
    a_im = lam_im * dt
    mag = jnp.exp(a_re)
    e_re = mag * jnp.cos(a_im) - 1.0
    e_im = mag * jnp.sin(a_im)
    den = lam_re * lam_re + lam_im * lam_im
    return a_re, a_im, (e_re * lam_re + e_im * lam_im) / den, (e_im * lam_re - e_re * lam_im) / den


def _cmul(ar, ai, br, bi):
    return ar * br - ai * bi, ar * bi + ai * br


def _s5_kernel(u_ref, lamc_ref, lamr_ref, lamr2_ref, bc_ref, br_ref, ct_ref, d_ref,
               y_ref, t_scr, x_scr, w_scr, *, batch, n_ctx, n_all):
    c, m, p = SCAN_CHUNK, S5_GROUP, S5_STATE
    cm = c * m
    rows = n_all * batch
    for mi in range(m):
        x_scr[:, mi * c:(mi + 1) * c] = u_ref[pl.ds(mi, rows, stride=m), :].astype(BF16)

    kk = lax.broadcasted_iota(jnp.int32, (1, LANES), 1).astype(F32)

    def powers(a_re, a_im, expo):
        mag = jnp.exp(a_re * expo)
        return mag * jnp.cos(a_im * expo), mag * jnp.sin(a_im * expo)

    def lane_tiles(fn):
        parts = [fn(i) for i in range(m)]
        return (jnp.concatenate([q[0] for q in parts], axis=1), jnp.concatenate([q[1] for q in parts], axis=1))

    disc = [_s5_disc(lamc_ref[d][:, 0:1], lamc_ref[d][:, 1:2], lamc_ref[d][:, 2:3]) for d in range(2)]
    bbar = [_cmul(disc[d][2], disc[d][3], bc_ref[d, 0], bc_ref[d, 1]) for d in range(2)]
    ctr = [(ct_ref[d, 0], ct_ref[d, 1]) for d in range(2)]

    def col(pair, i):
        return pair[0][:, i:i + 1], pair[1][:, i:i + 1]

    f_up = powers(disc[0][0], disc[0][1], kk + 1.0)
    f_down = powers(disc[0][0], disc[0][1], (c - 1.0) - kk)
    b_up = powers(disc[1][0], disc[1][1], kk)
    b_down = powers(disc[1][0], disc[1][1], (c - 1.0) - kk)
    b_down1 = powers(disc[1][0], disc[1][1], c - kk)
    lag0 = jnp.where(kk == c - 1.0, 1.0, 0.0)

    q0f = lane_tiles(lambda mo: (col(ctr[0], mo)[0] * lag0, col(ctr[0], mo)[1] * lag0))
    q1f = lane_tiles(lambda mo: _cmul(*col(ctr[0], mo), *f_up))
    q0b = lane_tiles(lambda mo: _cmul(*col(ctr[1], mo), *b_down))
    qrb = lane_tiles(lambda mo: _cmul(*col(ctr[1], mo), *b_down1))

    hp = lax.Precision.HIGHEST
    btr = []
    for d in range(2):
        lam = lamr_ref[d]
        _, _, cf_re, cf_im = _s5_disc(lam[0:1, :], lam[1:2, :], lam[2:3, :])
        btr.append(_cmul(cf_re, cf_im, br_ref[d, 0], br_ref[d, 1]))

    def gen(bt, q):
        return (jnp.dot(bt[0], q[0], precision=hp, preferred_element_type=F32)
                - jnp.dot(bt[1], q[1], precision=hp, preferred_element_type=F32))

    w0 = gen(btr[0], q0f) + gen(btr[1], q0b)
    w1 = gen(btr[0], q1f)
    for mo in range(m):
        w_scr[:, 2 * mo * c:(2 * mo + 1) * c] = w0[:, mo * c:(mo + 1) * c]
        w_scr[:, (2 * mo + 1) * c:(2 * mo + 2) * c] = w1[:, mo * c:(mo + 1) * c]

    for mi in range(m):
        row = jnp.broadcast_to(w_scr[mi:mi + 1, :], (c, 2 * cm))
        win = pltpu.roll(row, 2 * cm - (c - 1), 1, stride=1, stride_axis=0)
        blk = jnp.concatenate([win[:, 2 * mo * c:(2 * mo + 1) * c] for mo in range(m)], axis=1)
        t_scr[mi * c:(mi + 1) * c, :] = blk.astype(BF16)

    for d, q in ((0, q1f), (1, qrb)):
        base = cm + 2 * p * d
        t_scr[base:base + p, :] = q[0].astype(BF16)
        t_scr[base + p:base + 2 * p, :] = (-q[1]).astype(BF16)

    wb_f = lane_tiles(lambda mi: _cmul(*col(bbar[0], mi), *f_down))
    wb_b = lane_tiles(lambda mi: _cmul(*col(bbar[1], mi), *b_up))
    wb = jnp.concatenate([wb_f[0], wb_f[1], wb_b[0], wb_b[1]], axis=0).astype(BF16)
    hloc = _dot_nt(x_scr[:, :cm], wb)

    sign = jnp.where(lax.broadcasted_iota(jnp.int32, (1, 2 * p), 1) < p, -1.0, 1.0)
    order_f = list(range(n_all))
    order_b = list(range(n_ctx - 1, -1, -1)) + list(range(n_all - 1, n_ctx - 1, -1))
    for d, order in ((0, order_f), (1, order_b)):
        lam = lamr2_ref[d]
        a_re, a_im, _, _ = _s5_disc(lam[0:1, :], lam[1:2, :], lam[2:3, :])
        mag = jnp.exp(a_re * c)
        ac_r = mag * jnp.cos(a_im * c)
        ac_i = mag * jnp.sin(a_im * c) * sign
        state = jnp.zeros((batch, 2 * p), F32)
        colx = cm + 2 * p * d
        for n in order:
            x_scr[n * batch:(n + 1) * batch, colx:colx + 2 * p] = state.astype(BF16)
            local = hloc[n * batch:(n + 1) * batch, 2 * p * d:2 * p * (d + 1)]
            state = state * ac_r + pltpu.roll(state, p, 1) * ac_i + local

    y = _dot(x_scr[...], t_scr[...])
    for mo in range(m):
        skip = d_ref[mo:mo + 1, :] * u_ref[pl.ds(mo, rows, stride=m), :]
        y_ref[pl.ds(mo, rows, stride=m), :] = y[:, mo * c:(mo + 1) * c] + skip


def _s5_operands(u, lam_re, lam_im, log_dt, b_re, b_im, c_re, c_im, d_skip):
    g, n_all, batch, m, c = u.shape
    dtb = jnp.broadcast_to(log_dt[:, :, None], lam_re.shape)
    lam3 = jnp.stack([lam_re, lam_im, dtb], axis=-1)
    lamc = jnp.pad(lam3, ((0, 0), (0, 0), (0, 0), (0, LANES - 3))).transpose(1, 0, 2, 3)
    lam3r = jnp.stack([lam_re, lam_im, dtb], axis=2)
    lamr = jnp.pad(lam3r, ((0, 0), (0, 0), (0, 5), (0, 0))).transpose(1, 0, 2, 3)
    lamr2 = jnp.concatenate([lamr, lamr], axis=-1)
    bcol = jnp.stack([b_re, b_im], axis=1)
    bc = jnp.pad(bcol, ((0, 0),) * 4 + ((0, LANES - m),)).transpose(2, 0, 1, 3, 4)
    brow = jnp.swapaxes(bcol, -1, -2).transpose(2, 0, 1, 3, 4)
    ctc = jnp.swapaxes(jnp.stack([c_re, c_im], axis=1), -1, -2)
    ctc = jnp.pad(ctc, ((0, 0),) * 4 + ((0, LANES - m),)).transpose(2, 0, 1, 3, 4)
    dvec = jnp.broadcast_to(d_skip.reshape(g, m, 1), (g, m, LANES))
    return (u.reshape(g, n_all * batch * m, c), lamc, lamr, lamr2, bc, brow, ctc, dvec)


def _s5_mixer(u, lam_re, lam_im, log_dt, b_re, b_im, c_re, c_im, d_skip, n_ctx):
    g, n_all, b, m, c = u.shape
    p = S5_STATE
    rows = n_all * b
    cm = c * m
    kern = functools.partial(_s5_kernel, batch=b, n_ctx=n_ctx // c, n_all=n_all)
    per_g = lambda *tail: pl.BlockSpec((None,) + tail, lambda i: (i,) + (0,) * len(tail))
    y = pl.pallas_call(
        kern,
        grid=(g,),
        in_specs=[per_g(rows * m, c), per_g(2, p, LANES), per_g(2, 8, p), per_g(2, 8, 2 * p),
                  per_g(2, 2, p, LANES), per_g(2, 2, m, p), per_g(2, 2, p, LANES), per_g(m, LANES)],
        out_specs=per_g(rows * m, c),
        out_shape=jax.ShapeDtypeStruct((g, rows * m, c), F32),
        scratch_shapes=[pltpu.VMEM((cm + 4 * p, cm), BF16), pltpu.VMEM((rows, cm + 4 * p), BF16),
                        pltpu.VMEM((m, 2 * cm), F32)],
        compiler_params=_cparams("arbitrary"),
        name="s5_scan",
    )(*_s5_operands(u, lam_re, lam_im, log_dt, b_re, b_im, c_re, c_im, d_skip))
    return y.reshape(u.shape)


def kernel(x, c, ctx, c_ctx, ada_w, ada_b, norm_g, ffn_w_in, ffn_w_out, mla_w_in, mla_q_norm, mla_kv_norm,
           mla_w_uq, mla_w_ukv, mla_q_gain, mla_k_gain, mla_w_o, ret_w_in, ret_log1m_gamma, ret_gn_gain, ret_w_o,
           s5_lam_re, s5_lam_im, s5_log_dt, s5_b_re, s5_b_im, s5_c_re, s5_c_im, s5_d, s5_glu_w, s5_glu_b):
    b, s, d = x.shape
    n_ctx = ctx.shape[1]
    depth = ada_w.shape[0]
    assert n_ctx % TOKEN_TILE == 0 and s % TOKEN_TILE == 0 and s % GRID_W == 0 and b % FFN_BATCH_PER_STEP == 0
    n_ctx_tiles = n_ctx // TOKEN_TILE

    pad = (-(b + 1)) % 8
    cvec = jnp.concatenate([c, c_ctx[None, :], jnp.zeros((pad, d), F32)], axis=0)
    mod_all = _modulation(cvec, ada_w, ada_b)
    mod_lat = mod_all[:, :b].reshape(depth, b, 1, N_MOD, d)
    mod_ctx = jnp.broadcast_to(mod_all[:, b].reshape(depth, 1, 1, N_MOD, d), (depth, b, 1, N_MOD, d))
    mods = jnp.concatenate([mod_ctx, mod_lat], axis=2)

    mla_tabs = _mla_tables(n_ctx, s)
    ret_tabs = _ret_tables(n_ctx, s)

    r = None
    for i in range(depth):
        kind, j = i % N_MIXERS, i // N_MIXERS
        mod = mods[i]
        ffn0 = (norm_g[i, 0], ffn_w_in[i, 0].astype(BF16), ffn_w_out[i, 0].astype(BF16))
        ffn1 = (norm_g[i, 2], ffn_w_in[i, 1].astype(BF16), ffn_w_out[i, 1].astype(BF16))
        if i == 0:
            assert kind != 2
            r = _ffn_first(ctx, x, mod, *ffn0, n_ctx_tiles)
        elif kind == 2:
            r, u = _ffn_s5_in(r, mod, *ffn0, norm_g[i, 1], n_ctx_tiles)
        else:
            r = _ffn(r, mod, *ffn0, n_ctx_tiles)
        tile0 = n_ctx_tiles if i == depth - 1 else 0
        if kind == 0:
            weights = _mla_weights(mla_w_in[j], mla_q_norm[j], mla_kv_norm[j], mla_w_uq[j], mla_w_ukv[j],
                                   mla_q_gain[j], mla_k_gain[j])
            q, k, v = _mla_proj(r, *mla_tabs, mod, norm_g[i, 1], weights, n_ctx_tiles)
            y = _attention(q, k, v, _attn_score_bound(mla_q_gain[j], mla_k_gain[j]), n_ctx)
            mix = (y, _rows_spec(y.shape[-1], tile0), [mla_w_o[j].astype(BF16)], _ffn_proj_kernel, "mla_out_swiglu")
        elif kind == 1:
            q, k, v, sg = _ret_proj(r, *ret_tabs, mod, norm_g[i, 1], ret_w_in[j].astype(BF16), n_ctx_tiles)
            y = _ret_scan(q, k, v, sg, ret_log1m_gamma[j], ret_gn_gain[j], n_ctx)
            mix = (y, _rows_spec(y.shape[-1], tile0), [ret_w_o[j].astype(BF16)], _ffn_proj_kernel, "ret_out_swiglu")
        else:
            y = _s5_mixer(u, s5_lam_re[j], s5_lam_im[j], s5_log_dt[j], s5_b_re[j], s5_b_im[j],
                          s5_c_re[j], s5_c_im[j], s5_d[j], n_ctx)
            mix = (y, _chunk_major_spec(y.shape[0], tile0), [s5_glu_w[j].astype(BF16), s5_glu_b[j].reshape(1, -1)],
                   _ffn_glu_kernel, "s5_out_swiglu")
        r = _ffn_after_mixer(r, *mix, mod, *ffn1, n_ctx_tiles, tile0)
    return r
```

```python
import functools
import math

import jax
import jax.numpy as jnp
import numpy as np
from jax import lax
from jax.experimental import pallas as pl
from jax.experimental.pallas import tpu as pltpu

F32 = jnp.float32
BF16 = jnp.bfloat16

EPS = 1e-6
ROPE_BASE = 10000.0
GRID_W = 64
N_MIXERS = 3
N_MOD = 9

MLA_HEADS = 8
MLA_NOPE = 128
MLA_ROPE = 64
MLA_V = 128
MLA_Q_RANK = 384
MLA_KV_RANK = 256
MLA_HEAD_PAD = 256

RET_HEADS = 4
RET_DK = 256
RET_DV = 512
SCAN_CHUNK = 128

S5_GROUP = 16
S5_STATE = 64

LANES = 128
TOKEN_TILE = 256
ATTN_HEADS_PER_STEP = 4
FFN_BATCH_PER_STEP = 2
VMEM_LIMIT_BYTES = 56 * 1024 * 1024


def _cparams(*sem):
    return pltpu.CompilerParams(dimension_semantics=sem, vmem_limit_bytes=VMEM_LIMIT_BYTES)


def _resident(shape, single_buffer=False):
    zeros = (0,) * len(shape)
    if single_buffer:
        return pl.BlockSpec(shape, lambda *_: zeros, pipeline_mode=pl.Buffered(1))
    return pl.BlockSpec(shape, lambda *_: zeros)


def _silu(x):
    return x * jax.nn.sigmoid(x)


def _rms(x, n):
    return x * lax.rsqrt(jnp.sum(x * x, axis=-1, keepdims=True) * (1.0 / n) + EPS)


def _pre_norm(x, gain, mod_ref, k):
    y = _rms(x, x.shape[-1]) * gain
    return y * (1.0 + mod_ref[3 * k + 1:3 * k + 2, :]) + mod_ref[3 * k:3 * k + 1, :]


def _dot(a, b):
    return jnp.dot(a, b, preferred_element_type=F32)


def _dot_nt(a, b):
    return lax.dot_general(a, b, (((1,), (1,)), ((), ())), preferred_element_type=F32)


def _mod_kernel(cv_ref, w_ref, b_ref, o_ref):
    s = _silu(cv_ref[...]).astype(BF16)
    o_ref[...] = _dot(s, w_ref[...].astype(BF16)) + b_ref[...]


def _modulation(cvec, ada_w, ada_b):
    depth, d, n = ada_w.shape
    rows = cvec.shape[0]
    tn = n // 4
    return pl.pallas_call(
        _mod_kernel,
        grid=(depth, n // tn),
        in_specs=[
            _resident((rows, d)),
            pl.BlockSpec((None, d, tn), lambda i, j: (i, 0, j)),
            pl.BlockSpec((None, 1, tn), lambda i, j: (i, 0, j)),
        ],
        out_specs=pl.BlockSpec((None, rows, tn), lambda i, j: (i, 0, j)),
        out_shape=jax.ShapeDtypeStruct((depth, rows, n), F32),
        compiler_params=_cparams("arbitrary", "arbitrary"),
        name="adaln_modulation",
    )(cvec, ada_w, ada_b.reshape(depth, 1, n))


def _rows_spec(width, tile0=0):
    return pl.BlockSpec((FFN_BATCH_PER_STEP, TOKEN_TILE, width), lambda i, j: (i, j + tile0, 0))


def _mod_spec(d, n_ctx_tiles, tile0=0):
    return pl.BlockSpec((FFN_BATCH_PER_STEP, None, N_MOD, d),
                        lambda i, j: (i, (j + tile0 >= n_ctx_tiles).astype(jnp.int32), 0, 0))


def _row_call(kernel, name, b, n_tiles, inputs, consts, outs):
    in_specs = [spec for _, spec in inputs] + [_resident(a.shape, single_buffer=True) for a in consts]
    return pl.pallas_call(
        kernel,
        grid=(b // FFN_BATCH_PER_STEP, n_tiles),
        in_specs=in_specs,
        out_specs=[_rows_spec(w) for w, _ in outs],
        out_shape=[jax.ShapeDtypeStruct((b, n_tiles * TOKEN_TILE, w), dt) for w, dt in outs],
        compiler_params=_cparams("arbitrary", "arbitrary"),
        name=name,
    )(*[a for a, _ in inputs], *consts)


def _gelu_tanh(x):
    cdf = 0.5 * (1.0 + jnp.tanh(math.sqrt(2.0 / math.pi) * (x + 0.044715 * (x * x * x))))
    return x * cdf


def _swiglu_step(x, mod_ref, g_ref, win_ref, wout_ref, k):
    xn = _pre_norm(x, g_ref[...], mod_ref, k).astype(BF16)
    f = wout_ref.shape[0]
    a = _dot(xn, win_ref[:, :f])
    b = _dot(xn, win_ref[:, f:])
    hm = (_silu(a) * b).astype(BF16)
    return x + (0.5 * mod_ref[3 * k + 2:3 * k + 3, :]) * _dot(hm, wout_ref[...])


def _ffn_first_kernel(ctx_ref, x_ref, mod_ref, g_ref, win_ref, wout_ref, o_ref, *, n_ctx_tiles):
    for e in range(FFN_BATCH_PER_STEP):
        x = jnp.where(pl.program_id(1) < n_ctx_tiles, ctx_ref[e], x_ref[e])
        o_ref[e] = _swiglu_step(x, mod_ref.at[e], g_ref, win_ref, wout_ref, 0)


def _ffn_kernel(r_ref, mod_ref, g_ref, win_ref, wout_ref, o_ref):
    for e in range(FFN_BATCH_PER_STEP):
        o_ref[e] = _swiglu_step(r_ref[e], mod_ref.at[e], g_ref, win_ref, wout_ref, 0)


def _ffn_proj_kernel(r_ref, y_ref, mod_ref, pw_ref, g_ref, win_ref, wout_ref, o_ref):
    for e in range(FFN_BATCH_PER_STEP):
        mod = mod_ref.at[e]
        x = r_ref[e] + mod[5:6, :] * _dot(y_ref[e], pw_ref[...])
        o_ref[e] = _swiglu_step(x, mod, g_ref, win_ref, wout_ref, 2)


def _ffn_glu_kernel(r_ref, y_ref, mod_ref, pw_ref, pb_ref, g_ref, win_ref, wout_ref, o_ref):
    d = r_ref.shape[-1]
    c = SCAN_CHUNK
    for e in range(FFN_BATCH_PER_STEP):
        mod = mod_ref.at[e]
        y = jnp.concatenate([y_ref[:, ci, e].reshape(d, c).T for ci in range(TOKEN_TILE // c)], axis=0)
        ag = _dot(_gelu_tanh(y).astype(BF16), pw_ref[...]) + pb_ref[...]
        x = r_ref[e] + mod[5:6, :] * (ag[:, :d] * jax.nn.sigmoid(ag[:, d:]))
        o_ref[e] = _swiglu_step(x, mod, g_ref, win_ref, wout_ref, 2)


def _ffn_first(ctx, x, mod, gain, w_in, w_out, n_ctx_tiles):
    b, s, d = x.shape
    last_ctx = n_ctx_tiles - 1
    eb = FFN_BATCH_PER_STEP
    inputs = [(ctx, pl.BlockSpec((eb, TOKEN_TILE, d), lambda i, j: (i, jnp.minimum(j, last_ctx), 0))),
              (x, pl.BlockSpec((eb, TOKEN_TILE, d), lambda i, j: (i, jnp.maximum(j - n_ctx_tiles, 0), 0))),
              (mod, _mod_spec(d, n_ctx_tiles))]
    (out,) = _row_call(functools.partial(_ffn_first_kernel, n_ctx_tiles=n_ctx_tiles), "swiglu_first",
                       b, n_ctx_tiles + s // TOKEN_TILE, inputs, [gain.reshape(1, d), w_in, w_out], [(d, F32)])
    return out


def _ffn(r, mod, gain, w_in, w_out, n_ctx_tiles):
    b, t, d = r.shape
    inputs = [(r, _rows_spec(d)), (mod, _mod_spec(d, n_ctx_tiles))]
    (out,) = _row_call(_ffn_kernel, "swiglu_half_step", b, t // TOKEN_TILE, inputs,
                       [gain.reshape(1, d), w_in, w_out], [(d, F32)])
    return out


def _ffn_after_mixer(r, y, y_spec, mix_consts, kern, name, mod, gain, w_in, w_out, n_ctx_tiles, tile0):
    b, t, d = r.shape
    inputs = [(r, _rows_spec(d, tile0)), (y, y_spec), (mod, _mod_spec(d, n_ctx_tiles, tile0))]
    (out,) = _row_call(kern, name, b, t // TOKEN_TILE - tile0, inputs,
                       mix_consts + [gain.reshape(1, d), w_in, w_out], [(d, F32)])
    return out


def _rope_tile_perm():
    src = np.full((LANES,), -1, np.int64)
    src[0:16] = np.arange(0, 16)
    src[16:32] = np.arange(32, 48)
    src[64:80] = np.arange(16, 32)
    src[80:96] = np.arange(48, 64)
    return src


def _pad_rope_cols(w):
    src = _rope_tile_perm()
    cols = jnp.take(w, jnp.asarray(np.maximum(src, 0)), axis=-1)
    return jnp.where(jnp.asarray(src >= 0), cols, 0.0)


def _mla_tables(n_ctx, n_lat):
    half = MLA_ROPE // 2
    inv = ROPE_BASE ** (-jnp.arange(0, half, 2, dtype=F32) / half)
    pos = jnp.arange(n_lat)
    ang_r = (pos // GRID_W).astype(F32)[:, None] * inv[None, :]
    ang_c = (pos % GRID_W).astype(F32)[:, None] * inv[None, :]
    z = jnp.zeros((n_lat, 32), F32)
    cos = jnp.concatenate([jnp.cos(ang_r), jnp.cos(ang_c), z, jnp.cos(ang_r), jnp.cos(ang_c), z], axis=-1)
    sin = jnp.concatenate([-jnp.sin(ang_r), -jnp.sin(ang_c), z, jnp.sin(ang_r), jnp.sin(ang_c), z], axis=-1)
    cos = jnp.concatenate([jnp.ones((n_ctx, LANES), F32), cos], axis=0)
    sin = jnp.concatenate([jnp.zeros((n_ctx, LANES), F32), sin], axis=0)
    return cos, sin


def _mla_proj_kernel(r_ref, cos_ref, sin_ref, mod_ref, *rest):
    q_ref, k_ref, v_ref = rest[-3:]
    for e in range(FFN_BATCH_PER_STEP):
        _mla_project(r_ref[e], cos_ref, sin_ref, mod_ref.at[e], *rest[:-3], q_ref.at[e], k_ref.at[e], v_ref.at[e])


def _mla_project(x, cos_ref, sin_ref, mod_ref, g_ref, win_ref, qn_ref, kvn_ref, wuq_ref, wukv_ref,
                 qg_ref, kg_ref, q_ref, k_ref, v_ref):
    xn = _pre_norm(x, g_ref[...], mod_ref, 1).astype(BF16)
    p = _dot(xn, win_ref[...])
    cos = cos_ref[...]
    sin = sin_ref[...]

    def rotate(x):
        return x * cos + pltpu.roll(x, LANES // 2, 1) * sin

    cq = (_rms(p[:, :MLA_Q_RANK], MLA_Q_RANK) * qn_ref[...]).astype(BF16)
    ckv = (_rms(p[:, MLA_Q_RANK:MLA_Q_RANK + MLA_KV_RANK], MLA_KV_RANK) * kvn_ref[...]).astype(BF16)
    kr = rotate(_rms(p[:, MLA_Q_RANK + MLA_KV_RANK:], MLA_ROPE) * kg_ref[:, LANES:]).astype(BF16)
    q = _dot(cq, wuq_ref[...])
    kv = _dot(ckv, wukv_ref[...])
    scale = (MLA_NOPE + MLA_ROPE) ** -0.5 * math.log2(math.e)
    ii = lax.broadcasted_iota(jnp.int32, (2 * LANES, 2 * LANES), 0)
    jj = lax.broadcasted_iota(jnp.int32, (2 * LANES, 2 * LANES), 1)
    same = (ii // LANES) == (jj // LANES)
    mean_q = jnp.where(same, jnp.where(ii < LANES, 1.0 / MLA_NOPE, 1.0 / MLA_ROPE), 0.0).astype(BF16)
    mean_k = jnp.where(same, 1.0 / MLA_NOPE, 0.0).astype(BF16)
    q_gain = qg_ref[...] * scale
    for h in range(MLA_HEADS):
        a0 = h * MLA_HEAD_PAD
        qh = q[:, a0:a0 + 2 * LANES]
        qn = qh * lax.rsqrt(_dot((qh * qh).astype(BF16), mean_q) + EPS) * q_gain
        q_ref[:, a0:a0 + LANES] = qn[:, :LANES].astype(BF16)
        q_ref[:, a0 + LANES:a0 + 2 * LANES] = rotate(qn[:, LANES:]).astype(BF16)
        k_ref[:, a0 + LANES:a0 + 2 * LANES] = kr
    for hh in range(MLA_HEADS // 2):
        kh = kv[:, 2 * hh * LANES:(2 * hh + 2) * LANES]
        kn = kh * lax.rsqrt(_dot((kh * kh).astype(BF16), mean_k) + EPS)
        for e in range(2):
            a0 = (2 * hh + e) * MLA_HEAD_PAD
            k_ref[:, a0:a0 + LANES] = (kn[:, e * LANES:(e + 1) * LANES] * kg_ref[:, :LANES]).astype(BF16)
    v_ref[...] = kv[:, MLA_HEADS * LANES:].astype(BF16)


def _mla_weights(w_in, q_norm, kv_norm, w_uq, w_ukv, q_gain, k_gain):
    lat = MLA_Q_RANK + MLA_KV_RANK
    w_in_p = jnp.concatenate([w_in[:, :lat], _pad_rope_cols(w_in[:, lat:])], axis=-1).astype(BF16)
    uq = w_uq.reshape(MLA_Q_RANK, MLA_HEADS, MLA_NOPE + MLA_ROPE)
    uq = jnp.concatenate([uq[..., :MLA_NOPE], _pad_rope_cols(uq[..., MLA_NOPE:])], axis=-1)
    uq = uq.reshape(MLA_Q_RANK, MLA_HEADS * MLA_HEAD_PAD).astype(BF16)
    ukv = w_ukv.reshape(MLA_KV_RANK, MLA_HEADS, MLA_NOPE + MLA_V)
    ukv = jnp.concatenate([ukv[..., :MLA_NOPE].reshape(MLA_KV_RANK, -1),
                           ukv[..., MLA_NOPE:].reshape(MLA_KV_RANK, -1)], axis=-1).astype(BF16)
    qg = jnp.concatenate([q_gain[:MLA_NOPE], _pad_rope_cols(q_gain[MLA_NOPE:])]).reshape(1, -1)
    kg = jnp.concatenate([k_gain[:MLA_NOPE], _pad_rope_cols(k_gain[MLA_NOPE:])]).reshape(1, -1)
    return [w_in_p, q_norm.reshape(1, -1), kv_norm.reshape(1, -1), uq, ukv, qg, kg]


def _mla_proj(r, cos, sin, mod, gain, weights, n_ctx_tiles):
    b, t, d = r.shape
    hw = MLA_HEADS * MLA_HEAD_PAD
    table = pl.BlockSpec((TOKEN_TILE, LANES), lambda i, j: (j, 0))
    inputs = [(r, _rows_spec(d)), (cos, table), (sin, table), (mod, _mod_spec(d, n_ctx_tiles))]
    return _row_call(_mla_proj_kernel, "mla_projections", b, t // TOKEN_TILE, inputs,
                     [gain.reshape(1, d)] + weights, [(hw, BF16), (hw, BF16), (MLA_HEADS * MLA_V, BF16)])


ATTN_MIN_ROW_SUM = 2.0 ** -100


def _attn_kernel(bound_ref, q_ref, k_ref, v_ref, o_ref, *, n_ctx_tiles, n_ctx):
    def attend(n_keys, use_bound):
        low = None
        for h in range(ATTN_HEADS_PER_STEP):
            qk = slice(h * MLA_HEAD_PAD, (h + 1) * MLA_HEAD_PAD)
            vo = slice(h * MLA_V, (h + 1) * MLA_V)
            s = _dot_nt(q_ref[:, qk], k_ref[:n_keys, qk])
            shift = bound_ref[0:1, 0:1] if use_bound else jnp.max(s, axis=-1, keepdims=True)
            p = jnp.exp2(s - shift)
            l = jnp.sum(p, axis=-1, keepdims=True)
            o = _dot(p.astype(BF16), v_ref[:n_keys, vo])
            o_ref[:, vo] = (o / l).astype(o_ref.dtype)
            low = l if low is None else jnp.minimum(low, l)
        return low

    def attend_keys(n_keys):
        low = attend(n_keys, True)
        safe = jnp.min(low) >= ATTN_MIN_ROW_SUM

        @pl.when(jnp.logical_not(safe))
        def _():
            attend(n_keys, False)

    is_ctx = pl.program_id(2) < n_ctx_tiles
    pl.when(is_ctx)(lambda: attend_keys(n_ctx))
    pl.when(jnp.logical_not(is_ctx))(lambda: attend_keys(k_ref.shape[0]))


def _attn_score_bound(q_gain, k_gain):
    scale = (MLA_NOPE + MLA_ROPE) ** -0.5 * math.log2(math.e)
    gq, gk = jnp.abs(q_gain), jnp.abs(k_gain)
    dot_max = (MLA_NOPE * jnp.max(gq[:MLA_NOPE]) * jnp.max(gk[:MLA_NOPE])
               + MLA_ROPE * jnp.max(gq[MLA_NOPE:]) * jnp.max(gk[MLA_NOPE:]))
    return jnp.full((8, LANES), 1.01 * scale, F32) * dot_max


def _attention(q, k, v, bound, n_ctx):
    b, t, _ = q.shape
    tq = TOKEN_TILE
    hp = ATTN_HEADS_PER_STEP
    kern = functools.partial(_attn_kernel, n_ctx_tiles=n_ctx // tq, n_ctx=n_ctx)
    return pl.pallas_call(
        kern,
        grid=(b, MLA_HEADS // hp, t // tq),
        in_specs=[_resident(bound.shape),
                  pl.BlockSpec((None, tq, hp * MLA_HEAD_PAD), lambda i, h, j: (i, j, h)),
                  pl.BlockSpec((None, t, hp * MLA_HEAD_PAD), lambda i, h, j: (i, 0, h)),
                  pl.BlockSpec((None, t, hp * MLA_V), lambda i, h, j: (i, 0, h))],
        out_specs=pl.BlockSpec((None, tq, hp * MLA_V), lambda i, h, j: (i, j, h)),
        out_shape=jax.ShapeDtypeStruct((b, t, MLA_HEADS * MLA_V), BF16),
        compiler_params=_cparams("arbitrary", "arbitrary", "arbitrary"),
        name="mla_attention",
    )(bound, q, k, v)


def _ret_tables(n_ctx, n_lat):
    inv = ROPE_BASE ** (-jnp.arange(0, RET_DK, 2, dtype=F32) / RET_DK)
    ang = jnp.arange(n_lat, dtype=F32)[:, None] * inv[None, :]
    cos = jnp.concatenate([jnp.ones((n_ctx, LANES), F32), jnp.cos(ang)], axis=0)
    sin = jnp.concatenate([jnp.zeros((n_ctx, LANES), F32), jnp.sin(ang)], axis=0)
    return cos, sin


def _ret_proj_kernel(r_ref, cos_ref, sin_ref, mod_ref, g_ref, win_ref, q_ref, k_ref, v_ref, sg_ref):
    for e in range(FFN_BATCH_PER_STEP):
        _ret_project(r_ref[e], cos_ref, sin_ref, mod_ref.at[e], g_ref, win_ref,
                     q_ref.at[e], k_ref.at[e], v_ref.at[e], sg_ref.at[e])


def _ret_project(x, cos_ref, sin_ref, mod_ref, g_ref, win_ref, q_ref, k_ref, v_ref, sg_ref):
    xn = _pre_norm(x, g_ref[...], mod_ref, 1).astype(BF16)
    cos = cos_ref[...]
    sin = sin_ref[...]
    hdk = RET_HEADS * RET_DK
    hdv = RET_HEADS * RET_DV
    scale = RET_DK ** -0.5
    for out_ref, base, sc in ((q_ref, 0, None), (k_ref, hdk, scale)):
        qk = _dot(xn, win_ref[:, base:base + hdk])
        for h in range(RET_HEADS):
            c0 = h * RET_DK
            x1 = qk[:, c0:c0 + LANES]
            x2 = qk[:, c0 + LANES:c0 + 2 * LANES]
            o1 = x1 * cos - x2 * sin
            o2 = x2 * cos + x1 * sin
            if sc is not None:
                o1 = o1 * sc
                o2 = o2 * sc
            out_ref[:, c0:c0 + LANES] = o1.astype(BF16)
            out_ref[:, c0 + LANES:c0 + 2 * LANES] = o2.astype(BF16)
    v_ref[...] = _dot(xn, win_ref[:, 2 * hdk:2 * hdk + hdv]).astype(BF16)
    sg_ref[...] = _silu(_dot(xn, win_ref[:, 2 * hdk + hdv:])).astype(BF16)


def _ret_proj(r, cos, sin, mod, gain, w_in, n_ctx_tiles):
    b, t, d = r.shape
    table = pl.BlockSpec((TOKEN_TILE, LANES), lambda i, j: (j, 0))
    inputs = [(r, _rows_spec(d)), (cos, table), (sin, table), (mod, _mod_spec(d, n_ctx_tiles))]
    widths = [(RET_HEADS * RET_DK, BF16), (RET_HEADS * RET_DK, BF16),
              (RET_HEADS * RET_DV, BF16), (RET_HEADS * RET_DV, BF16)]
    return _row_call(_ret_proj_kernel, "retention_projections", b, t // TOKEN_TILE, inputs,
                     [gain.reshape(1, d), w_in], widths)


def _ret_scan_kernel(l1g_ref, q_ref, k_ref, v_ref, sg_ref, gain_ref, y_ref, cross_scr, st_scr, *, n_ctx, n_all):
    c = SCAN_CHUNK
    log_g = jnp.log1p(-jnp.exp(l1g_ref[...]))
    lgf = log_g[0:1, 0:1]
    lgb = log_g[1:2, 0:1]
    ii = lax.broadcasted_iota(jnp.int32, (c, c), 0)
    jj = lax.broadcasted_iota(jnp.int32, (c, c), 1)
    diff = (ii - jj).astype(F32)
    mask = jnp.where(diff >= 0.0, jnp.exp(lgf * jnp.maximum(diff, 0.0)), jnp.exp(lgb * jnp.maximum(-diff, 0.0)))
    ic = lax.broadcasted_iota(jnp.int32, (c, 1), 0).astype(F32)
    qdec_f = jnp.exp(lgf * (ic + 1.0))
    qdec_b = jnp.exp(lgb * (c - ic))
    kdec_f = jnp.exp(lgf * (c - 1.0 - ic))
    kdec_b = jnp.exp(lgb * ic)
    cdec_f = jnp.exp(lgf * c)
    cdec_b = jnp.exp(lgb * c)

    def kv_outer(kc, dec, vc):
        return _dot((kc.astype(F32) * dec).T.astype(BF16), vc)

    sf_scr, sb_scr = st_scr.at[0], st_scr.at[1]
    sf_scr[...] = jnp.zeros_like(sf_scr)
    sb_scr[...] = jnp.zeros_like(sb_scr)
    order_b = list(range(n_ctx - 1, -1, -1)) + list(range(n_all - 1, n_ctx - 1, -1))
    seen = set()
    for nf, nb in zip(range(n_all), order_b):
        for n, s_ref, qdec, kdec, cdec in ((nf, sf_scr, qdec_f, kdec_f, cdec_f), (nb, sb_scr, qdec_b, kdec_b, cdec_b)):
            rows = slice(n * c, (n + 1) * c)
            state = s_ref[...]
            part = _dot((q_ref[rows, :].astype(F32) * qdec).astype(BF16), state.astype(BF16))
            if n in seen:
                cross_scr[rows, :] += part
            else:
                cross_scr[rows, :] = part
                seen.add(n)
            s_ref[...] = state * cdec + kv_outer(k_ref[rows, :], kdec, v_ref[rows, :])

    gain = gain_ref[...]
    for n in range(n_all):
        rows = slice(n * c, (n + 1) * c)
        scores = _dot_nt(q_ref[rows, :], k_ref[rows, :]) * mask
        o = _dot(scores.astype(BF16), v_ref[rows, :]) + cross_scr[rows, :]
        mu = jnp.mean(o, axis=-1, keepdims=True)
        dev = o - mu
        on = dev * lax.rsqrt(jnp.mean(dev * dev, axis=-1, keepdims=True) + EPS)
        y_ref[rows, :] = (on * gain * sg_ref[rows, :].astype(F32)).astype(y_ref.dtype)


def _ret_scan(q, k, v, sg, log1m_gamma, gn_gain, n_ctx):
    b, t, _ = q.shape
    l1g = jnp.broadcast_to(log1m_gamma.T[:, :, None], (RET_HEADS, 2, LANES))
    l1g = jnp.concatenate([l1g, jnp.zeros((RET_HEADS, 6, LANES), F32) - 1.0], axis=1)
    kern = functools.partial(_ret_scan_kernel, n_ctx=n_ctx // SCAN_CHUNK, n_all=t // SCAN_CHUNK)
    return pl.pallas_call(
        kern,
        grid=(b, RET_HEADS),
        in_specs=[pl.BlockSpec((None, 8, LANES), lambda i, h: (h, 0, 0)),
                  pl.BlockSpec((None, t, RET_DK), lambda i, h: (i, 0, h)),
                  pl.BlockSpec((None, t, RET_DK), lambda i, h: (i, 0, h)),
                  pl.BlockSpec((None, t, RET_DV), lambda i, h: (i, 0, h)),
                  pl.BlockSpec((None, t, RET_DV), lambda i, h: (i, 0, h)),
                  pl.BlockSpec((1, RET_DV), lambda i, h: (0, h))],
        out_specs=pl.BlockSpec((None, t, RET_DV), lambda i, h: (i, 0, h)),
        out_shape=jax.ShapeDtypeStruct((b, t, RET_HEADS * RET_DV), BF16),
        scratch_shapes=[pltpu.VMEM((t, RET_DV), F32), pltpu.VMEM((2, RET_DK, RET_DV), F32)],
        compiler_params=_cparams("arbitrary", "arbitrary"),
        name="retention_scan",
    )(l1g, q, k, v, sg, gn_gain.reshape(1, -1))


CHUNKS_PER_TILE = TOKEN_TILE // SCAN_CHUNK


def _chunk_major_spec(g, tile0=0):
    return pl.BlockSpec((g, CHUNKS_PER_TILE, FFN_BATCH_PER_STEP, S5_GROUP, SCAN_CHUNK),
                        lambda i, j: (0, j + tile0, i, 0, 0))


def _ffn_s5_in_kernel(r_ref, mod_ref, g_ref, win_ref, wout_ref, g1_ref, o_ref, u_ref):
    c = SCAN_CHUNK
    for e in range(FFN_BATCH_PER_STEP):
        mod = mod_ref.at[e]
        x = _swiglu_step(r_ref[e], mod, g_ref, win_ref, wout_ref, 0)
        o_ref[e] = x
        xn = _pre_norm(x, g1_ref[...], mod, 1)
        for ci in range(CHUNKS_PER_TILE):
            u_ref[:, ci, e] = xn[ci * c:(ci + 1) * c, :].T.reshape(u_ref.shape[0], S5_GROUP, c)


def _ffn_s5_in(r, mod, gain, w_in, w_out, mixer_gain, n_ctx_tiles):
    b, t, d = r.shape
    g = d // S5_GROUP
    consts = [gain.reshape(1, d), w_in, w_out, mixer_gain.reshape(1, d)]
    return pl.pallas_call(
        _ffn_s5_in_kernel,
        grid=(b // FFN_BATCH_PER_STEP, t // TOKEN_TILE),
        in_specs=[_rows_spec(d), _mod_spec(d, n_ctx_tiles)] + [_resident(a.shape, single_buffer=True) for a in consts],
        out_specs=[_rows_spec(d), _chunk_major_spec(g)],
        out_shape=[jax.ShapeDtypeStruct((b, t, d), F32),
                   jax.ShapeDtypeStruct((g, t // SCAN_CHUNK, b, S5_GROUP, SCAN_CHUNK), F32)],
        compiler_params=_cparams("arbitrary", "arbitrary"),
        name="swiglu_s5_in",
    )(r, mod, *consts)


def _s5_disc(lam_re, lam_im, log_dt):
    dt = jnp.exp(log_dt)
    a_re = lam_re * dt
    a_im = lam_im * dt
    mag = jnp.exp(a_re)
    e_re = mag * jnp.cos(a_im) - 1.0
    e_im = mag * jnp.sin(a_im)
    den = lam_re * lam_re + lam_im * lam_im
    return a_re, a_im, (e_re * lam_re + e_im * lam_im) / den, (e_im * lam_re - e_re * lam_im) / den


def _cmul(ar, ai, br, bi):
    return ar * br - ai * bi, ar * bi + ai * br


def _s5_kernel(u_ref, lamc_ref, lamr_ref, lamr2_ref, bc_ref, br_ref, ct_ref, d_ref,
               y_ref, t_scr, x_scr, w_scr, *, batch, n_ctx, n_all):
    c, m, p = SCAN_CHUNK, S5_GROUP, S5_STATE
    cm = c * m
    rows = n_all * batch
    for mi in range(m):
        x_scr[:, mi * c:(mi + 1) * c] = u_ref[pl.ds(mi, rows, stride=m), :].astype(BF16)

    kk = lax.broadcasted_iota(jnp.int32, (1, LANES), 1).astype(F32)

    def powers(a_re, a_im, expo):
        mag = jnp.exp(a_re * expo)
        return mag * jnp.cos(a_im * expo), mag * jnp.sin(a_im * expo)

    def lane_tiles(fn):
        parts = [fn(i) for i in range(m)]
        return (jnp.concatenate([q[0] for q in parts], axis=1), jnp.concatenate([q[1] for q in parts], axis=1))

    disc = [_s5_disc(lamc_ref[d][:, 0:1], lamc_ref[d][:, 1:2], lamc_ref[d][:, 2:3]) for d in range(2)]
    bbar = [_cmul(disc[d][2], disc[d][3], bc_ref[d, 0], bc_ref[d, 1]) for d in range(2)]
    ctr = [(ct_ref[d, 0], ct_ref[d, 1]) for d in range(2)]

    def col(pair, i):
        return pair[0][:, i:i + 1], pair[1][:, i:i + 1]

    f_up = powers(disc[0][0], disc[0][1], kk + 1.0)
    f_down = powers(disc[0][0], disc[0][1], (c - 1.0) - kk)
    b_up = powers(disc[1][0], disc[1][1], kk)
    b_down = powers(disc[1][0], disc[1][1], (c - 1.0) - kk)
    b_down1 = powers(disc[1][0], disc[1][1], c - kk)
    lag0 = jnp.where(kk == c - 1.0, 1.0, 0.0)

    q0f = lane_tiles(lambda mo: (col(ctr[0], mo)[0] * lag0, col(ctr[0], mo)[1] * lag0))
    q1f = lane_tiles(lambda mo: _cmul(*col(ctr[0], mo), *f_up))
    q0b = lane_tiles(lambda mo: _cmul(*col(ctr[1], mo), *b_down))
    qrb = lane_tiles(lambda mo: _cmul(*col(ctr[1], mo), *b_down1))

    hp = lax.Precision.HIGHEST
    btr = []
    for d in range(2):
        lam = lamr_ref[d]
        _, _, cf_re, cf_im = _s5_disc(lam[0:1, :], lam[1:2, :], lam[2:3, :])
        btr.append(_cmul(cf_re, cf_im, br_ref[d, 0], br_ref[d, 1]))

    def gen(bt, q):
        return (jnp.dot(bt[0], q[0], precision=hp, preferred_element_type=F32)
                - jnp.dot(bt[1], q[1], precision=hp, preferred_element_type=F32))

    w0 = gen(btr[0], q0f) + gen(btr[1], q0b)
    w1 = gen(btr[0], q1f)
    for mo in range(m):
        w_scr[:, 2 * mo * c:(2 * mo + 1) * c] = w0[:, mo * c:(mo + 1) * c]
        w_scr[:, (2 * mo + 1) * c:(2 * mo + 2) * c] = w1[:, mo * c:(mo + 1) * c]

    for mi in range(m):
        row = jnp.broadcast_to(w_scr[mi:mi + 1, :], (c, 2 * cm))
        win = pltpu.roll(row, 2 * cm - (c - 1), 1, stride=1, stride_axis=0)
        blk = jnp.concatenate([win[:, 2 * mo * c:(2 * mo + 1) * c] for mo in range(m)], axis=1)
        t_scr[mi * c:(mi + 1) * c, :] = blk.astype(BF16)

    for d, q in ((0, q1f), (1, qrb)):
        base = cm + 2 * p * d
        t_scr[base:base + p, :] = q[0].astype(BF16)
        t_scr[base + p:base + 2 * p, :] = (-q[1]).astype(BF16)

    wb_f = lane_tiles(lambda mi: _cmul(*col(bbar[0], mi), *f_down))
    wb_b = lane_tiles(lambda mi: _cmul(*col(bbar[1], mi), *b_up))
    wb = jnp.concatenate([wb_f[0], wb_f[1], wb_b[0], wb_b[1]], axis=0).astype(BF16)
    hloc = _dot_nt(x_scr[:, :cm], wb)

    sign = jnp.where(lax.broadcasted_iota(jnp.int32, (1, 2 * p), 1) < p, -1.0, 1.0)
    order_f = list(range(n_all))
    order_b = list(range(n_ctx - 1, -1, -1)) + list(range(n_all - 1, n_ctx - 1, -1))
    for d, order in ((0, order_f), (1, order_b)):
        lam = lamr2_ref[d]
        a_re, a_im, _, _ = _s5_disc(lam[0:1, :], lam[1:2, :], lam[2:3, :])
        mag = jnp.exp(a_re * c)
        ac_r = mag * jnp.cos(a_im * c)
        ac_i = mag * jnp.sin(a_im * c) * sign
        state = jnp.zeros((batch, 2 * p), F32)
        colx = cm + 2 * p * d
        for n in order:
            x_scr[n * batch:(n + 1) * batch, colx:colx + 2 * p] = state.astype(BF16)
            local = hloc[n * batch:(n + 1) * batch, 2 * p * d:2 * p * (d + 1)]
            state = state * ac_r + pltpu.roll(state, p, 1) * ac_i + local

    y = _dot(x_scr[...], t_scr[...])
    for mo in range(m):
        skip = d_ref[mo:mo + 1, :] * u_ref[pl.ds(mo, rows, stride=m), :]
        y_ref[pl.ds(mo, rows, stride=m), :] = y[:, mo * c:(mo + 1) * c] + skip


def _s5_operands(u, lam_re, lam_im, log_dt, b_re, b_im, c_re, c_im, d_skip):
    g, n_all, batch, m, c = u.shape
    dtb = jnp.broadcast_to(log_dt[:, :, None], lam_re.shape)
    lam3 = jnp.stack([lam_re, lam_im, dtb], axis=-1)
    lamc = jnp.pad(lam3, ((0, 0), (0, 0), (0, 0), (0, LANES - 3))).transpose(1, 0, 2, 3)
    lam3r = jnp.stack([lam_re, lam_im, dtb], axis=2)
    lamr = jnp.pad(lam3r, ((0, 0), (0, 0), (0, 5), (0, 0))).transpose(1, 0, 2, 3)
    lamr2 = jnp.concatenate([lamr, lamr], axis=-1)
    bcol = jnp.stack([b_re, b_im], axis=1)
    bc = jnp.pad(bcol, ((0, 0),) * 4 + ((0, LANES - m),)).transpose(2, 0, 1, 3, 4)
    brow = jnp.swapaxes(bcol, -1, -2).transpose(2, 0, 1, 3, 4)
    ctc = jnp.swapaxes(jnp.stack([c_re, c_im], axis=1), -1, -2)
    ctc = jnp.pad(ctc, ((0, 0),) * 4 + ((0, LANES - m),)).transpose(2, 0, 1, 3, 4)
    dvec = jnp.broadcast_to(d_skip.reshape(g, m, 1), (g, m, LANES))
    return (u.reshape(g, n_all * batch * m, c), lamc, lamr, lamr2, bc, brow, ctc, dvec)


def _s5_mixer(u, lam_re, lam_im, log_dt, b_re, b_im, c_re, c_im, d_skip, n_ctx):
    g, n_all, b, m, c = u.shape
    p = S5_STATE
    rows = n_all * b
    cm = c * m
    kern = functools.partial(_s5_kernel, batch=b, n_ctx=n_ctx // c, n_all=n_all)
    per_g = lambda *tail: pl.BlockSpec((None,) + tail, lambda i: (i,) + (0,) * len(tail))
    y = pl.pallas_call(
        kern,
        grid=(g,),
        in_specs=[per_g(rows * m, c), per_g(2, p, LANES), per_g(2, 8, p), per_g(2, 8, 2 * p),
                  per_g(2, 2, p, LANES), per_g(2, 2, m, p), per_g(2, 2, p, LANES), per_g(m, LANES)],
        out_specs=per_g(rows * m, c),
        out_shape=jax.ShapeDtypeStruct((g, rows * m, c), F32),
        scratch_shapes=[pltpu.VMEM((cm + 4 * p, cm), BF16), pltpu.VMEM((rows, cm + 4 * p), BF16),
                        pltpu.VMEM((m, 2 * cm), F32)],
        compiler_params=_cparams("arbitrary"),
        name="s5_scan",
    )(*_s5_operands(u, lam_re, lam_im, log_dt, b_re, b_im, c_re, c_im, d_skip))
    return y.reshape(u.shape)


def kernel(x, c, ctx, c_ctx, ada_w, ada_b, norm_g, ffn_w_in, ffn_w_out, mla_w_in, mla_q_norm, mla_kv_norm,
           mla_w_uq, mla_w_ukv, mla_q_gain, mla_k_gain, mla_w_o, ret_w_in, ret_log1m_gamma, ret_gn_gain, ret_w_o,
           s5_lam_re, s5_lam_im, s5_log_dt, s5_b_re, s5_b_im, s5_c_re, s5_c_im, s5_d, s5_glu_w, s5_glu_b):
    b, s, d = x.shape
    n_ctx = ctx.shape[1]
    depth = ada_w.shape[0]
    assert n_ctx % TOKEN_TILE == 0 and s % TOKEN_TILE == 0 and s % GRID_W == 0 and b % FFN_BATCH_PER_STEP == 0
    n_ctx_tiles = n_ctx // TOKEN_TILE

    pad = (-(b + 1)) % 8
    cvec = jnp.concatenate([c, c_ctx[None, :], jnp.zeros((pad, d), F32)], axis=0)
    mod_all = _modulation(cvec, ada_w, ada_b)
    mod_lat = mod_all[:, :b].reshape(depth, b, 1, N_MOD, d)
    mod_ctx = jnp.broadcast_to(mod_all[:, b].reshape(depth, 1, 1, N_MOD, d), (depth, b, 1, N_MOD, d))
    mods = jnp.concatenate([mod_ctx, mod_lat], axis=2)

    mla_tabs = _mla_tables(n_ctx, s)
    ret_tabs = _ret_tables(n_ctx, s)

    r = None
    for i in range(depth):
        kind, j = i % N_MIXERS, i // N_MIXERS
        mod = mods[i]
        ffn0 = (norm_g[i, 0], ffn_w_in[i, 0].astype(BF16), ffn_w_out[i, 0].astype(BF16))
        ffn1 = (norm_g[i, 2], ffn_w_in[i, 1].astype(BF16), ffn_w_out[i, 1].astype(BF16))
        if i == 0:
            assert kind != 2
            r = _ffn_first(ctx, x, mod, *ffn0, n_ctx_tiles)
        elif kind == 2:
            r, u = _ffn_s5_in(r, mod, *ffn0, norm_g[i, 1], n_ctx_tiles)
        else:
            r = _ffn(r, mod, *ffn0, n_ctx_tiles)
        tile0 = n_ctx_tiles if i == depth - 1 else 0
        if kind == 0:
            weights = _mla_weights(mla_w_in[j], mla_q_norm[j], mla_kv_norm[j], mla_w_uq[j], mla_w_ukv[j],
                                   mla_q_gain[j], mla_k_gain[j])
            q, k, v = _mla_proj(r, *mla_tabs, mod, norm_g[i, 1], weights, n_ctx_tiles)
            y = _attention(q, k, v, _attn_score_bound(mla_q_gain[j], mla_k_gain[j]), n_ctx)
            mix = (y, _rows_spec(y.shape[-1], tile0), [mla_w_o[j].astype(BF16)], _ffn_proj_kernel, "mla_out_swiglu")
        elif kind == 1:
            q, k, v, sg = _ret_proj(r, *ret_tabs, mod, norm_g[i, 1], ret_w_in[j].astype(BF16), n_ctx_tiles)
            y = _ret_scan(q, k, v, sg, ret_log1m_gamma[j], ret_gn_gain[j], n_ctx)
            mix = (y, _rows_spec(y.shape[-1], tile0), [ret_w_o[j].astype(BF16)], _ffn_proj_kernel, "ret_out_swiglu")
        else:
            y = _s5_mixer(u, s5_lam_re[j], s5_lam_im[j], s5_log_dt[j], s5_b_re[j], s5_b_im[j],
                          s5_c_re[j], s5_c_im[j], s5_d[j], n_ctx)
            mix = (y, _chunk_major_spec(y.shape[0], tile0), [s5_glu_w[j].astype(BF16), s5_glu_b[j].reshape(1, -1)],
                   _ffn_glu_kernel, "s5_out_swiglu")
        r = _ffn_after_mixer(r, *mix, mod, *ffn1, n_ctx_tiles, tile0)
    return r
```

```python
import functools
import math

import jax
import jax.numpy as jnp
import numpy as np
from jax import lax
from jax.experimental import pallas as pl
from jax.experimental.pallas import tpu as pltpu

F32 = jnp.float32
BF16 = jnp.bfloat16

EPS = 1e-6
ROPE_BASE = 10000.0
GRID_W = 64
N_MIXERS = 3
N_MOD = 9

MLA_HEADS = 8
MLA_NOPE = 128
MLA_ROPE = 64
MLA_V = 128
MLA_Q_RANK = 384
MLA_KV_RANK = 256
MLA_HEAD_PAD = 256

RET_HEADS = 4
RET_DK = 256
RET_DV = 512
SCAN_CHUNK = 128

S5_GROUP = 16
S5_STATE = 64

LANES = 128
TOKEN_TILE = 256
ATTN_HEADS_PER_STEP = 4
FFN_BATCH_PER_STEP = 2
VMEM_LIMIT_BYTES = 56 * 1024 * 1024


def _cparams(*sem):
    return pltpu.CompilerParams(dimension_semantics=sem, vmem_limit_bytes=VMEM_LIMIT_BYTES)


def _resident(shape, single_buffer=False):
    zeros = (0,) * len(shape)
    if single_buffer:
        return pl.BlockSpec(shape, lambda *_: zeros, pipeline_mode=pl.Buffered(1))
    return pl.BlockSpec(shape, lambda *_: zeros)


def _silu(x):
    return x * jax.nn.sigmoid(x)


def _rms(x, n):
    return x * lax.rsqrt(jnp.sum(x * x, axis=-1, keepdims=True) * (1.0 / n) + EPS)


def _pre_norm(x, gain, mod_ref, k):
    y = _rms(x, x.shape[-1]) * gain
    return y * (1.0 + mod_ref[3 * k + 1:3 * k + 2, :]) + mod_ref[3 * k:3 * k + 1, :]


def _dot(a, b):
    return jnp.dot(a, b, preferred_element_type=F32)


def _dot_nt(a, b):
    return lax.dot_general(a, b, (((1,), (1,)), ((), ())), preferred_element_type=F32)


def _mod_kernel(cv_ref, w_ref, b_ref, o_ref):
    s = _silu(cv_ref[...]).astype(BF16)
    o_ref[...] = _dot(s, w_ref[...].astype(BF16)) + b_ref[...]


def _modulation(cvec, ada_w, ada_b):
    depth, d, n = ada_w.shape
    rows = cvec.shape[0]
    tn = n // 4
    return pl.pallas_call(
        _mod_kernel,
        grid=(depth, n // tn),
        in_specs=[
            _resident((rows, d)),
            pl.BlockSpec((None, d, tn), lambda i, j: (i, 0, j)),
            pl.BlockSpec((None, 1, tn), lambda i, j: (i, 0, j)),
        ],
        out_specs=pl.BlockSpec((None, rows, tn), lambda i, j: (i, 0, j)),
        out_shape=jax.ShapeDtypeStruct((depth, rows, n), F32),
        compiler_params=_cparams("arbitrary", "arbitrary"),
        name="adaln_modulation",
    )(cvec, ada_w, ada_b.reshape(depth, 1, n))


def _rows_spec(width, tile0=0):
    return pl.BlockSpec((FFN_BATCH_PER_STEP, TOKEN_TILE, width), lambda i, j: (i, j + tile0, 0))


def _mod_spec(d, n_ctx_tiles, tile0=0):
    return pl.BlockSpec((FFN_BATCH_PER_STEP, None, N_MOD, d),
                        lambda i, j: (i, (j + tile0 >= n_ctx_tiles).astype(jnp.int32), 0, 0))


def _const_spec(c):
    if not isinstance(c, tuple):
        return _resident(c.shape, single_buffer=True)
    stacked, lead = c
    tail = stacked.shape[len(lead):]
    return pl.BlockSpec((None,) * len(lead) + tail, lambda *_: lead + (0,) * len(tail), pipeline_mode=pl.Buffered(1))


def _const_array(c):
    return c[0] if isinstance(c, tuple) else c


def _row_call(kernel, name, b, n_tiles, inputs, consts, outs):
    in_specs = [spec for _, spec in inputs] + [_const_spec(c) for c in consts]
    return pl.pallas_call(
        kernel,
        grid=(b // FFN_BATCH_PER_STEP, n_tiles),
        in_specs=in_specs,
        out_specs=[_rows_spec(w) for w, _ in outs],
        out_shape=[jax.ShapeDtypeStruct((b, n_tiles * TOKEN_TILE, w), dt) for w, dt in outs],
        compiler_params=_cparams("arbitrary", "arbitrary"),
        name=name,
    )(*[a for a, _ in inputs], *[_const_array(c) for c in consts])


def _gelu_tanh(x):
    cdf = 0.5 * (1.0 + jnp.tanh(math.sqrt(2.0 / math.pi) * (x + 0.044715 * (x * x * x))))
    return x * cdf


def _swiglu_step(x, mod_ref, g_ref, win_ref, wout_ref, k):
    xn = _pre_norm(x, g_ref[...], mod_ref, k).astype(BF16)
    f = wout_ref.shape[0]
    a = _dot(xn, win_ref[:, :f])
    b = _dot(xn, win_ref[:, f:])
    hm = (_silu(a) * b).astype(BF16)
    return x + (0.5 * mod_ref[3 * k + 2:3 * k + 3, :]) * _dot(hm, wout_ref[...])


def _ffn_first_kernel(ctx_ref, x_ref, mod_ref, g_ref, win_ref, wout_ref, o_ref, *, n_ctx_tiles):
    for e in range(FFN_BATCH_PER_STEP):
        x = jnp.where(pl.program_id(1) < n_ctx_tiles, ctx_ref[e], x_ref[e])
        o_ref[e] = _swiglu_step(x, mod_ref.at[e], g_ref, win_ref, wout_ref, 0)


def _ffn_kernel(r_ref, mod_ref, g_ref, win_ref, wout_ref, o_ref):
    for e in range(FFN_BATCH_PER_STEP):
        o_ref[e] = _swiglu_step(r_ref[e], mod_ref.at[e], g_ref, win_ref, wout_ref, 0)


def _ffn_proj_kernel(r_ref, y_ref, mod_ref, pw_ref, g_ref, win_ref, wout_ref, o_ref):
    for e in range(FFN_BATCH_PER_STEP):
        mod = mod_ref.at[e]
        x = r_ref[e] + mod[5:6, :] * _dot(y_ref[e], pw_ref[...])
        o_ref[e] = _swiglu_step(x, mod, g_ref, win_ref, wout_ref, 2)


def _ffn_glu_kernel(r_ref, y_ref, mod_ref, pw_ref, pb_ref, g_ref, win_ref, wout_ref, o_ref):
    d = r_ref.shape[-1]
    c = SCAN_CHUNK
    for e in range(FFN_BATCH_PER_STEP):
        mod = mod_ref.at[e]
        y = jnp.concatenate([y_ref[:, ci, e].reshape(d, c).T for ci in range(TOKEN_TILE // c)], axis=0)
        ag = _dot(_gelu_tanh(y).astype(BF16), pw_ref[...]) + pb_ref[...]
        x = r_ref[e] + mod[5:6, :] * (ag[:, :d] * jax.nn.sigmoid(ag[:, d:]))
        o_ref[e] = _swiglu_step(x, mod, g_ref, win_ref, wout_ref, 2)


def _ffn_first(ctx, x, mod, gain, w_in, w_out, n_ctx_tiles):
    b, s, d = x.shape
    last_ctx = n_ctx_tiles - 1
    eb = FFN_BATCH_PER_STEP
    inputs = [(ctx, pl.BlockSpec((eb, TOKEN_TILE, d), lambda i, j: (i, jnp.minimum(j, last_ctx), 0))),
              (x, pl.BlockSpec((eb, TOKEN_TILE, d), lambda i, j: (i, jnp.maximum(j - n_ctx_tiles, 0), 0))),
              (mod, _mod_spec(d, n_ctx_tiles))]
    (out,) = _row_call(functools.partial(_ffn_first_kernel, n_ctx_tiles=n_ctx_tiles), "swiglu_first",
                       b, n_ctx_tiles + s // TOKEN_TILE, inputs, [gain.reshape(1, d), w_in, w_out], [(d, F32)])
    return out


def _ffn(r, mod, gain, w_in, w_out, n_ctx_tiles):
    b, t, d = r.shape
    inputs = [(r, _rows_spec(d)), (mod, _mod_spec(d, n_ctx_tiles))]
    (out,) = _row_call(_ffn_kernel, "swiglu_half_step", b, t // TOKEN_TILE, inputs,
                       [gain.reshape(1, d), w_in, w_out], [(d, F32)])
    return out


def _ffn_after_mixer(r, y, y_spec, mix_consts, kern, name, mod, gain, w_in, w_out, n_ctx_tiles, tile0):
    b, t, d = r.shape
    inputs = [(r, _rows_spec(d, tile0)), (y, y_spec), (mod, _mod_spec(d, n_ctx_tiles, tile0))]
    (out,) = _row_call(kern, name, b, t // TOKEN_TILE - tile0, inputs,
                       mix_consts + [gain.reshape(1, d), w_in, w_out], [(d, F32)])
    return out


def _rope_tile_perm():
    src = np.full((LANES,), -1, np.int64)
    src[0:16] = np.arange(0, 16)
    src[16:32] = np.arange(32, 48)
    src[64:80] = np.arange(16, 32)
    src[80:96] = np.arange(48, 64)
    return src


def _pad_rope_cols(w):
    src = _rope_tile_perm()
    cols = jnp.take(w, jnp.asarray(np.maximum(src, 0)), axis=-1)
    return jnp.where(jnp.asarray(src >= 0), cols, 0.0)


def _mla_tables(n_ctx, n_lat):
    half = MLA_ROPE // 2
    inv = ROPE_BASE ** (-jnp.arange(0, half, 2, dtype=F32) / half)
    pos = jnp.arange(n_lat)
    ang_r = (pos // GRID_W).astype(F32)[:, None] * inv[None, :]
    ang_c = (pos % GRID_W).astype(F32)[:, None] * inv[None, :]
    z = jnp.zeros((n_lat, 32), F32)
    cos = jnp.concatenate([jnp.cos(ang_r), jnp.cos(ang_c), z, jnp.cos(ang_r), jnp.cos(ang_c), z], axis=-1)
    sin = jnp.concatenate([-jnp.sin(ang_r), -jnp.sin(ang_c), z, jnp.sin(ang_r), jnp.sin(ang_c), z], axis=-1)
    cos = jnp.concatenate([jnp.ones((n_ctx, LANES), F32), cos], axis=0)
    sin = jnp.concatenate([jnp.zeros((n_ctx, LANES), F32), sin], axis=0)
    return cos, sin


def _mla_proj_kernel(r_ref, cos_ref, sin_ref, mod_ref, *rest):
    q_ref, k_ref, v_ref = rest[-3:]
    for e in range(FFN_BATCH_PER_STEP):
        _mla_project(r_ref[e], cos_ref, sin_ref, mod_ref.at[e], *rest[:-3], q_ref.at[e], k_ref.at[e], v_ref.at[e])


def _mla_project(x, cos_ref, sin_ref, mod_ref, g_ref, win_ref, qn_ref, kvn_ref, wuq_ref, wukv_ref,
                 qg_ref, kg_ref, q_ref, k_ref, v_ref):
    xn = _pre_norm(x, g_ref[...], mod_ref, 1).astype(BF16)
    p = _dot(xn, win_ref[...])
    cos = cos_ref[...]
    sin = sin_ref[...]

    def rotate(x):
        return x * cos + pltpu.roll(x, LANES // 2, 1) * sin

    cq = (_rms(p[:, :MLA_Q_RANK], MLA_Q_RANK) * qn_ref[...]).astype(BF16)
    ckv = (_rms(p[:, MLA_Q_RANK:MLA_Q_RANK + MLA_KV_RANK], MLA_KV_RANK) * kvn_ref[...]).astype(BF16)
    kr = rotate(_rms(p[:, MLA_Q_RANK + MLA_KV_RANK:], MLA_ROPE) * kg_ref[:, LANES:]).astype(BF16)
    q = _dot(cq, wuq_ref[...])
    kv = _dot(ckv, wukv_ref[...])
    scale = (MLA_NOPE + MLA_ROPE) ** -0.5 * math.log2(math.e)
    ii = lax.broadcasted_iota(jnp.int32, (2 * LANES, 2 * LANES), 0)
    jj = lax.broadcasted_iota(jnp.int32, (2 * LANES, 2 * LANES), 1)
    same = (ii // LANES) == (jj // LANES)
    mean_q = jnp.where(same, jnp.where(ii < LANES, 1.0 / MLA_NOPE, 1.0 / MLA_ROPE), 0.0).astype(BF16)
    mean_k = jnp.where(same, 1.0 / MLA_NOPE, 0.0).astype(BF16)
    q_gain = qg_ref[...] * scale
    for h in range(MLA_HEADS):
        a0 = h * MLA_HEAD_PAD
        qh = q[:, a0:a0 + 2 * LANES]
        qn = qh * lax.rsqrt(_dot((qh * qh).astype(BF16), mean_q) + EPS) * q_gain
        q_ref[:, a0:a0 + LANES] = qn[:, :LANES].astype(BF16)
        q_ref[:, a0 + LANES:a0 + 2 * LANES] = rotate(qn[:, LANES:]).astype(BF16)
        k_ref[:, a0 + LANES:a0 + 2 * LANES] = kr
    for hh in range(MLA_HEADS // 2):
        kh = kv[:, 2 * hh * LANES:(2 * hh + 2) * LANES]
        kn = kh * lax.rsqrt(_dot((kh * kh).astype(BF16), mean_k) + EPS)
        for e in range(2):
            a0 = (2 * hh + e) * MLA_HEAD_PAD
            k_ref[:, a0:a0 + LANES] = (kn[:, e * LANES:(e + 1) * LANES] * kg_ref[:, :LANES]).astype(BF16)
    v_ref[...] = kv[:, MLA_HEADS * LANES:].astype(BF16)


def _mla_weights(w_in, q_norm, kv_norm, w_uq, w_ukv, q_gain, k_gain):
    lat = MLA_Q_RANK + MLA_KV_RANK
    w_in_p = jnp.concatenate([w_in[:, :lat], _pad_rope_cols(w_in[:, lat:])], axis=-1).astype(BF16)
    uq = w_uq.reshape(MLA_Q_RANK, MLA_HEADS, MLA_NOPE + MLA_ROPE)
    uq = jnp.concatenate([uq[..., :MLA_NOPE], _pad_rope_cols(uq[..., MLA_NOPE:])], axis=-1)
    uq = uq.reshape(MLA_Q_RANK, MLA_HEADS * MLA_HEAD_PAD).astype(BF16)
    ukv = w_ukv.reshape(MLA_KV_RANK, MLA_HEADS, MLA_NOPE + MLA_V)
    ukv = jnp.concatenate([ukv[..., :MLA_NOPE].reshape(MLA_KV_RANK, -1),
                           ukv[..., MLA_NOPE:].reshape(MLA_KV_RANK, -1)], axis=-1).astype(BF16)
    qg = jnp.concatenate([q_gain[:MLA_NOPE], _pad_rope_cols(q_gain[MLA_NOPE:])]).reshape(1, -1)
    kg = jnp.concatenate([k_gain[:MLA_NOPE], _pad_rope_cols(k_gain[MLA_NOPE:])]).reshape(1, -1)
    return [w_in_p, q_norm.reshape(1, -1), kv_norm.reshape(1, -1), uq, ukv, qg, kg]


def _mla_proj(r, cos, sin, mod, gain, weights, n_ctx_tiles):
    b, t, d = r.shape
    hw = MLA_HEADS * MLA_HEAD_PAD
    table = pl.BlockSpec((TOKEN_TILE, LANES), lambda i, j: (j, 0))
    inputs = [(r, _rows_spec(d)), (cos, table), (sin, table), (mod, _mod_spec(d, n_ctx_tiles))]
    return _row_call(_mla_proj_kernel, "mla_projections", b, t // TOKEN_TILE, inputs,
                     [gain.reshape(1, d)] + weights, [(hw, BF16), (hw, BF16), (MLA_HEADS * MLA_V, BF16)])


ATTN_MIN_ROW_SUM = 2.0 ** -100


def _attn_kernel(bound_ref, q_ref, k_ref, v_ref, o_ref, *, n_ctx_tiles, n_ctx):
    def attend(n_keys, use_bound):
        low = None
        for h in range(ATTN_HEADS_PER_STEP):
            qk = slice(h * MLA_HEAD_PAD, (h + 1) * MLA_HEAD_PAD)
            vo = slice(h * MLA_V, (h + 1) * MLA_V)
            s = _dot_nt(q_ref[:, qk], k_ref[:n_keys, qk])
            shift = bound_ref[0:1, 0:1] if use_bound else jnp.max(s, axis=-1, keepdims=True)
            p = jnp.exp2(s - shift)
            l = jnp.sum(p, axis=-1, keepdims=True)
            o = _dot(p.astype(BF16), v_ref[:n_keys, vo])
            o_ref[:, vo] = (o / l).astype(o_ref.dtype)
            low = l if low is None else jnp.minimum(low, l)
        return low

    def attend_keys(n_keys):
        low = attend(n_keys, True)
        safe = jnp.min(low) >= ATTN_MIN_ROW_SUM

        @pl.when(jnp.logical_not(safe))
        def _():
            attend(n_keys, False)

    is_ctx = pl.program_id(2) < n_ctx_tiles
    pl.when(is_ctx)(lambda: attend_keys(n_ctx))
    pl.when(jnp.logical_not(is_ctx))(lambda: attend_keys(k_ref.shape[0]))


def _attn_score_bound(q_gain, k_gain):
    scale = (MLA_NOPE + MLA_ROPE) ** -0.5 * math.log2(math.e)
    gq, gk = jnp.abs(q_gain), jnp.abs(k_gain)
    dot_max = (MLA_NOPE * jnp.max(gq[:MLA_NOPE]) * jnp.max(gk[:MLA_NOPE])
               + MLA_ROPE * jnp.max(gq[MLA_NOPE:]) * jnp.max(gk[MLA_NOPE:]))
    return jnp.full((8, LANES), 1.01 * scale, F32) * dot_max


def _attention(q, k, v, bound, n_ctx):
    b, t, _ = q.shape
    tq = TOKEN_TILE
    hp = ATTN_HEADS_PER_STEP
    kern = functools.partial(_attn_kernel, n_ctx_tiles=n_ctx // tq, n_ctx=n_ctx)
    return pl.pallas_call(
        kern,
        grid=(b, MLA_HEADS // hp, t // tq),
        in_specs=[_resident(bound.shape),
                  pl.BlockSpec((None, tq, hp * MLA_HEAD_PAD), lambda i, h, j: (i, j, h)),
                  pl.BlockSpec((None, t, hp * MLA_HEAD_PAD), lambda i, h, j: (i, 0, h)),
                  pl.BlockSpec((None, t, hp * MLA_V), lambda i, h, j: (i, 0, h))],
        out_specs=pl.BlockSpec((None, tq, hp * MLA_V), lambda i, h, j: (i, j, h)),
        out_shape=jax.ShapeDtypeStruct((b, t, MLA_HEADS * MLA_V), BF16),
        compiler_params=_cparams("arbitrary", "arbitrary", "arbitrary"),
        name="mla_attention",
    )(bound, q, k, v)


def _ret_tables(n_ctx, n_lat):
    inv = ROPE_BASE ** (-jnp.arange(0, RET_DK, 2, dtype=F32) / RET_DK)
    ang = jnp.arange(n_lat, dtype=F32)[:, None] * inv[None, :]
    cos = jnp.concatenate([jnp.ones((n_ctx, LANES), F32), jnp.cos(ang)], axis=0)
    sin = jnp.concatenate([jnp.zeros((n_ctx, LANES), F32), jnp.sin(ang)], axis=0)
    return cos, sin


def _ret_proj_kernel(r_ref, cos_ref, sin_ref, mod_ref, g_ref, win_ref, q_ref, k_ref, v_ref, sg_ref):
    for e in range(FFN_BATCH_PER_STEP):
        _ret_project(r_ref[e], cos_ref, sin_ref, mod_ref.at[e], g_ref, win_ref,
                     q_ref.at[e], k_ref.at[e], v_ref.at[e], sg_ref.at[e])


def _ret_project(x, cos_ref, sin_ref, mod_ref, g_ref, win_ref, q_ref, k_ref, v_ref, sg_ref):
    xn = _pre_norm(x, g_ref[...], mod_ref, 1).astype(BF16)
    cos = cos_ref[...]
    sin = sin_ref[...]
    hdk = RET_HEADS * RET_DK
    hdv = RET_HEADS * RET_DV
    scale = RET_DK ** -0.5
    for out_ref, base, sc in ((q_ref, 0, None), (k_ref, hdk, scale)):
        qk = _dot(xn, win_ref[:, base:base + hdk])
        for h in range(RET_HEADS):
            c0 = h * RET_DK
            x1 = qk[:, c0:c0 + LANES]
            x2 = qk[:, c0 + LANES:c0 + 2 * LANES]
            o1 = x1 * cos - x2 * sin
            o2 = x2 * cos + x1 * sin
            if sc is not None:
                o1 = o1 * sc
                o2 = o2 * sc
            out_ref[:, c0:c0 + LANES] = o1.astype(BF16)
            out_ref[:, c0 + LANES:c0 + 2 * LANES] = o2.astype(BF16)
    v_ref[...] = _dot(xn, win_ref[:, 2 * hdk:2 * hdk + hdv]).astype(BF16)
    sg_ref[...] = _silu(_dot(xn, win_ref[:, 2 * hdk + hdv:])).astype(BF16)


def _ret_proj(r, cos, sin, mod, gain, w_in, n_ctx_tiles):
    b, t, d = r.shape
    table = pl.BlockSpec((TOKEN_TILE, LANES), lambda i, j: (j, 0))
    inputs = [(r, _rows_spec(d)), (cos, table), (sin, table), (mod, _mod_spec(d, n_ctx_tiles))]
    widths = [(RET_HEADS * RET_DK, BF16), (RET_HEADS * RET_DK, BF16),
              (RET_HEADS * RET_DV, BF16), (RET_HEADS * RET_DV, BF16)]
    return _row_call(_ret_proj_kernel, "retention_projections", b, t // TOKEN_TILE, inputs,
                     [gain.reshape(1, d), w_in], widths)


def _ret_scan_kernel(l1g_ref, q_ref, k_ref, v_ref, sg_ref, gain_ref, y_ref, cross_scr, st_scr, *, n_ctx, n_all):
    c = SCAN_CHUNK
    log_g = jnp.log1p(-jnp.exp(l1g_ref[...]))
    lgf = log_g[0:1, 0:1]
    lgb = log_g[1:2, 0:1]
    ii = lax.broadcasted_iota(jnp.int32, (c, c), 0)
    jj = lax.broadcasted_iota(jnp.int32, (c, c), 1)
    diff = (ii - jj).astype(F32)
    mask = jnp.where(diff >= 0.0, jnp.exp(lgf * jnp.maximum(diff, 0.0)), jnp.exp(lgb * jnp.maximum(-diff, 0.0)))
    ic = lax.broadcasted_iota(jnp.int32, (c, 1), 0).astype(F32)
    qdec_f = jnp.exp(lgf * (ic + 1.0))
    qdec_b = jnp.exp(lgb * (c - ic))
    kdec_f = jnp.exp(lgf * (c - 1.0 - ic))
    kdec_b = jnp.exp(lgb * ic)
    cdec_f = jnp.exp(lgf * c)
    cdec_b = jnp.exp(lgb * c)

    def kv_outer(kc, dec, vc):
        return _dot((kc.astype(F32) * dec).T.astype(BF16), vc)

    sf_scr, sb_scr = st_scr.at[0], st_scr.at[1]
    sf_scr[...] = jnp.zeros_like(sf_scr)
    sb_scr[...] = jnp.zeros_like(sb_scr)
    order_b = list(range(n_ctx - 1, -1, -1)) + list(range(n_all - 1, n_ctx - 1, -1))
    seen = set()
    for nf, nb in zip(range(n_all), order_b):
        for n, s_ref, qdec, kdec, cdec in ((nf, sf_scr, qdec_f, kdec_f, cdec_f), (nb, sb_scr, qdec_b, kdec_b, cdec_b)):
            rows = slice(n * c, (n + 1) * c)
            state = s_ref[...]
            part = _dot((q_ref[rows, :].astype(F32) * qdec).astype(BF16), state.astype(BF16))
            if n in seen:
                cross_scr[rows, :] += part
            else:
                cross_scr[rows, :] = part
                seen.add(n)
            s_ref[...] = state * cdec + kv_outer(k_ref[rows, :], kdec, v_ref[rows, :])

    gain = gain_ref[...]
    for n in range(n_all):
        rows = slice(n * c, (n + 1) * c)
        scores = _dot_nt(q_ref[rows, :], k_ref[rows, :]) * mask
        o = _dot(scores.astype(BF16), v_ref[rows, :]) + cross_scr[rows, :]
        mu = jnp.mean(o, axis=-1, keepdims=True)
        dev = o - mu
        on = dev * lax.rsqrt(jnp.mean(dev * dev, axis=-1, keepdims=True) + EPS)
        y_ref[rows, :] = (on * gain * sg_ref[rows, :].astype(F32)).astype(y_ref.dtype)


def _ret_scan(q, k, v, sg, log1m_gamma, gn_gain, n_ctx):
    b, t, _ = q.shape
    l1g = jnp.broadcast_to(log1m_gamma.T[:, :, None], (RET_HEADS, 2, LANES))
    l1g = jnp.concatenate([l1g, jnp.zeros((RET_HEADS, 6, LANES), F32) - 1.0], axis=1)
    kern = functools.partial(_ret_scan_kernel, n_ctx=n_ctx // SCAN_CHUNK, n_all=t // SCAN_CHUNK)
    return pl.pallas_call(
        kern,
        grid=(b, RET_HEADS),
        in_specs=[pl.BlockSpec((None, 8, LANES), lambda i, h: (h, 0, 0)),
                  pl.BlockSpec((None, t, RET_DK), lambda i, h: (i, 0, h)),
                  pl.BlockSpec((None, t, RET_DK), lambda i, h: (i, 0, h)),
                  pl.BlockSpec((None, t, RET_DV), lambda i, h: (i, 0, h)),
                  pl.BlockSpec((None, t, RET_DV), lambda i, h: (i, 0, h)),
                  pl.BlockSpec((1, RET_DV), lambda i, h: (0, h))],
        out_specs=pl.BlockSpec((None, t, RET_DV), lambda i, h: (i, 0, h)),
        out_shape=jax.ShapeDtypeStruct((b, t, RET_HEADS * RET_DV), BF16),
        scratch_shapes=[pltpu.VMEM((t, RET_DV), F32), pltpu.VMEM((2, RET_DK, RET_DV), F32)],
        compiler_params=_cparams("arbitrary", "arbitrary"),
        name="retention_scan",
    )(l1g, q, k, v, sg, gn_gain.reshape(1, -1))


CHUNKS_PER_TILE = TOKEN_TILE // SCAN_CHUNK


def _chunk_major_spec(g, tile0=0):
    return pl.BlockSpec((g, CHUNKS_PER_TILE, FFN_BATCH_PER_STEP, S5_GROUP, SCAN_CHUNK),
                        lambda i, j: (0, j + tile0, i, 0, 0))


def _ffn_s5_in_kernel(r_ref, mod_ref, g_ref, win_ref, wout_ref, g1_ref, o_ref, u_ref):
    c = SCAN_CHUNK
    for e in range(FFN_BATCH_PER_STEP):
        mod = mod_ref.at[e]
        x = _swiglu_step(r_ref[e], mod, g_ref, win_ref, wout_ref, 0)
        o_ref[e] = x
        xn = _pre_norm(x, g1_ref[...], mod, 1)
        for ci in range(CHUNKS_PER_TILE):
            u_ref[:, ci, e] = xn[ci * c:(ci + 1) * c, :].T.reshape(u_ref.shape[0], S5_GROUP, c)


def _ffn_s5_in(r, mod, gain, w_in, w_out, mixer_gain, n_ctx_tiles):
    b, t, d = r.shape
    g = d // S5_GROUP
    consts = [gain.reshape(1, d), w_in, w_out, mixer_gain.reshape(1, d)]
    return pl.pallas_call(
        _ffn_s5_in_kernel,
        grid=(b // FFN_BATCH_PER_STEP, t // TOKEN_TILE),
        in_specs=[_rows_spec(d), _mod_spec(d, n_ctx_tiles)] + [_const_spec(c) for c in consts],
        out_specs=[_rows_spec(d), _chunk_major_spec(g)],
        out_shape=[jax.ShapeDtypeStruct((b, t, d), F32),
                   jax.ShapeDtypeStruct((g, t // SCAN_CHUNK, b, S5_GROUP, SCAN_CHUNK), F32)],
        compiler_params=_cparams("arbitrary", "arbitrary"),
        name="swiglu_s5_in",
    )(r, mod, *[_const_array(c) for c in consts])


def _s5_disc(lam_re, lam_im, log_dt):
    dt = jnp.exp(log_dt)
    a_re = lam_re * dt
    a_im = lam_im * dt
    mag = jnp.exp(a_re)
    e_re = mag * jnp.cos(a_im) - 1.0
    e_im = mag * jnp.sin(a_im)
    den = lam_re * lam_re + lam_im * lam_im
    return a_re, a_im, (e_re * lam_re + e_im * lam_im) / den, (e_im * lam_re - e_re * lam_im) / den


def _cmul(ar, ai, br, bi):
    return ar * br - ai * bi, ar * bi + ai * br


def _s5_kernel(u_ref, lamc_ref, lamr_ref, lamr2_ref, bc_ref, br_ref, ct_ref, d_ref,
               y_ref, t_scr, x_scr, w_scr, *, batch, n_ctx, n_all):
    c, m, p = SCAN_CHUNK, S5_GROUP, S5_STATE
    cm = c * m
    rows = n_all * batch
    for mi in range(m):
        x_scr[:, mi * c:(mi + 1) * c] = u_ref[pl.ds(mi, rows, stride=m), :].astype(BF16)

    kk = lax.broadcasted_iota(jnp.int32, (1, LANES), 1).astype(F32)

    def powers(a_re, a_im, expo):
        mag = jnp.exp(a_re * expo)
        return mag * jnp.cos(a_im * expo), mag * jnp.sin(a_im * expo)

    def lane_tiles(fn):
        parts = [fn(i) for i in range(m)]
        return (jnp.concatenate([q[0] for q in parts], axis=1), jnp.concatenate([q[1] for q in parts], axis=1))

    disc = [_s5_disc(lamc_ref[d][:, 0:1], lamc_ref[d][:, 1:2], lamc_ref[d][:, 2:3]) for d in range(2)]
    bbar = [_cmul(disc[d][2], disc[d][3], bc_ref[d, 0], bc_ref[d, 1]) for d in range(2)]
    ctr = [(ct_ref[d, 0], ct_ref[d, 1]) for d in range(2)]

    def col(pair, i):
        return pair[0][:, i:i + 1], pair[1][:, i:i + 1]

    f_up = powers(disc[0][0], disc[0][1], kk + 1.0)
    f_down = powers(disc[0][0], disc[0][1], (c - 1.0) - kk)
    b_up = powers(disc[1][0], disc[1][1], kk)
    b_down = powers(disc[1][0], disc[1][1], (c - 1.0) - kk)
    b_down1 = powers(disc[1][0], disc[1][1], c - kk)
    lag0 = jnp.where(kk == c - 1.0, 1.0, 0.0)

    q0f = lane_tiles(lambda mo: (col(ctr[0], mo)[0] * lag0, col(ctr[0], mo)[1] * lag0))
    q1f = lane_tiles(lambda mo: _cmul(*col(ctr[0], mo), *f_up))
    q0b = lane_tiles(lambda mo: _cmul(*col(ctr[1], mo), *b_down))
    qrb = lane_tiles(lambda mo: _cmul(*col(ctr[1], mo), *b_down1))

    hp = lax.Precision.HIGHEST
    btr = []
    for d in range(2):
        lam = lamr_ref[d]
        _, _, cf_re, cf_im = _s5_disc(lam[0:1, :], lam[1:2, :], lam[2:3, :])
        btr.append(_cmul(cf_re, cf_im, br_ref[d, 0], br_ref[d, 1]))

    def gen(bt, q):
        return (jnp.dot(bt[0], q[0], precision=hp, preferred_element_type=F32)
                - jnp.dot(bt[1], q[1], precision=hp, preferred_element_type=F32))

    w0 = gen(btr[0], q0f) + gen(btr[1], q0b)
    w1 = gen(btr[0], q1f)
    for mo in range(m):
        w_scr[:, 2 * mo * c:(2 * mo + 1) * c] = w0[:, mo * c:(mo + 1) * c]
        w_scr[:, (2 * mo + 1) * c:(2 * mo + 2) * c] = w1[:, mo * c:(mo + 1) * c]

    for mi in range(m):
        row = jnp.broadcast_to(w_scr[mi:mi + 1, :], (c, 2 * cm))
        win = pltpu.roll(row, 2 * cm - (c - 1), 1, stride=1, stride_axis=0)
        blk = jnp.concatenate([win[:, 2 * mo * c:(2 * mo + 1) * c] for mo in range(m)], axis=1)
        t_scr[mi * c:(mi + 1) * c, :] = blk.astype(BF16)

    for d, q in ((0, q1f), (1, qrb)):
        base = cm + 2 * p * d
        t_scr[base:base + p, :] = q[0].astype(BF16)
        t_scr[base + p:base + 2 * p, :] = (-q[1]).astype(BF16)

    wb_f = lane_tiles(lambda mi: _cmul(*col(bbar[0], mi), *f_down))
    wb_b = lane_tiles(lambda mi: _cmul(*col(bbar[1], mi), *b_up))
    wb = jnp.concatenate([wb_f[0], wb_f[1], wb_b[0], wb_b[1]], axis=0).astype(BF16)
    hloc = _dot_nt(x_scr[:, :cm], wb)

    sign = jnp.where(lax.broadcasted_iota(jnp.int32, (1, 2 * p), 1) < p, -1.0, 1.0)
    order_f = list(range(n_all))
    order_b = list(range(n_ctx - 1, -1, -1)) + list(range(n_all - 1, n_ctx - 1, -1))
    for d, order in ((0, order_f), (1, order_b)):
        lam = lamr2_ref[d]
        a_re, a_im, _, _ = _s5_disc(lam[0:1, :], lam[1:2, :], lam[2:3, :])
        mag = jnp.exp(a_re * c)
        ac_r = mag * jnp.cos(a_im * c)
        ac_i = mag * jnp.sin(a_im * c) * sign
        state = jnp.zeros((batch, 2 * p), F32)
        colx = cm + 2 * p * d
        for n in order:
            x_scr[n * batch:(n + 1) * batch, colx:colx + 2 * p] = state.astype(BF16)
            local = hloc[n * batch:(n + 1) * batch, 2 * p * d:2 * p * (d + 1)]
            state = state * ac_r + pltpu.roll(state, p, 1) * ac_i + local

    y = _dot(x_scr[...], t_scr[...])
    for mo in range(m):
        skip = d_ref[mo:mo + 1, :] * u_ref[pl.ds(mo, rows, stride=m), :]
        y_ref[pl.ds(mo, rows, stride=m), :] = y[:, mo * c:(mo + 1) * c] + skip


def _s5_operands(u, lam_re, lam_im, log_dt, b_re, b_im, c_re, c_im, d_skip):
    g, n_all, batch, m, c = u.shape
    dtb = jnp.broadcast_to(log_dt[:, :, None], lam_re.shape)
    lam3 = jnp.stack([lam_re, lam_im, dtb], axis=-1)
    lamc = jnp.pad(lam3, ((0, 0), (0, 0), (0, 0), (0, LANES - 3))).transpose(1, 0, 2, 3)
    lam3r = jnp.stack([lam_re, lam_im, dtb], axis=2)
    lamr = jnp.pad(lam3r, ((0, 0), (0, 0), (0, 5), (0, 0))).transpose(1, 0, 2, 3)
    lamr2 = jnp.concatenate([lamr, lamr], axis=-1)
    bcol = jnp.stack([b_re, b_im], axis=1)
    bc = jnp.pad(bcol, ((0, 0),) * 4 + ((0, LANES - m),)).transpose(2, 0, 1, 3, 4)
    brow = jnp.swapaxes(bcol, -1, -2).transpose(2, 0, 1, 3, 4)
    ctc = jnp.swapaxes(jnp.stack([c_re, c_im], axis=1), -1, -2)
    ctc = jnp.pad(ctc, ((0, 0),) * 4 + ((0, LANES - m),)).transpose(2, 0, 1, 3, 4)
    dvec = jnp.broadcast_to(d_skip.reshape(g, m, 1), (g, m, LANES))
    return (u.reshape(g, n_all * batch * m, c), lamc, lamr, lamr2, bc, brow, ctc, dvec)


def _s5_mixer(u, lam_re, lam_im, log_dt, b_re, b_im, c_re, c_im, d_skip, n_ctx):
    g, n_all, b, m, c = u.shape
    p = S5_STATE
    rows = n_all * b
    cm = c * m
    kern = functools.partial(_s5_kernel, batch=b, n_ctx=n_ctx // c, n_all=n_all)
    per_g = lambda *tail: pl.BlockSpec((None,) + tail, lambda i: (i,) + (0,) * len(tail))
    y = pl.pallas_call(
        kern,
        grid=(g,),
        in_specs=[per_g(rows * m, c), per_g(2, p, LANES), per_g(2, 8, p), per_g(2, 8, 2 * p),
                  per_g(2, 2, p, LANES), per_g(2, 2, m, p), per_g(2, 2, p, LANES), per_g(m, LANES)],
        out_specs=per_g(rows * m, c),
        out_shape=jax.ShapeDtypeStruct((g, rows * m, c), F32),
        scratch_shapes=[pltpu.VMEM((cm + 4 * p, cm), BF16), pltpu.VMEM((rows, cm + 4 * p), BF16),
                        pltpu.VMEM((m, 2 * cm), F32)],
        compiler_params=_cparams("arbitrary"),
        name="s5_scan",
    )(*_s5_operands(u, lam_re, lam_im, log_dt, b_re, b_im, c_re, c_im, d_skip))
    return y.reshape(u.shape)


def kernel(x, c, ctx, c_ctx, ada_w, ada_b, norm_g, ffn_w_in, ffn_w_out, mla_w_in, mla_q_norm, mla_kv_norm,
           mla_w_uq, mla_w_ukv, mla_q_gain, mla_k_gain, mla_w_o, ret_w_in, ret_log1m_gamma, ret_gn_gain, ret_w_o,
           s5_lam_re, s5_lam_im, s5_log_dt, s5_b_re, s5_b_im, s5_c_re, s5_c_im, s5_d, s5_glu_w, s5_glu_b):
    b, s, d = x.shape
    n_ctx = ctx.shape[1]
    depth = ada_w.shape[0]
    assert n_ctx % TOKEN_TILE == 0 and s % TOKEN_TILE == 0 and s % GRID_W == 0 and b % FFN_BATCH_PER_STEP == 0
    n_ctx_tiles = n_ctx // TOKEN_TILE

    pad = (-(b + 1)) % 8
    cvec = jnp.concatenate([c, c_ctx[None, :], jnp.zeros((pad, d), F32)], axis=0)
    mod_all = _modulation(cvec, ada_w, ada_b)
    mod_lat = mod_all[:, :b].reshape(depth, b, 1, N_MOD, d)
    mod_ctx = jnp.broadcast_to(mod_all[:, b].reshape(depth, 1, 1, N_MOD, d), (depth, b, 1, N_MOD, d))
    mods = jnp.concatenate([mod_ctx, mod_lat], axis=2)

    mla_tabs = _mla_tables(n_ctx, s)
    ret_tabs = _ret_tables(n_ctx, s)

    ffn_w_in_bf = ffn_w_in.astype(BF16)
    ffn_w_out_bf = ffn_w_out.astype(BF16)
    r = None
    for i in range(depth):
        kind, j = i % N_MIXERS, i // N_MIXERS
        mod = mods[i]
        ffn0 = (norm_g[i, 0], (ffn_w_in_bf, (i, 0)), (ffn_w_out_bf, (i, 0)))
        ffn1 = (norm_g[i, 2], (ffn_w_in_bf, (i, 1)), (ffn_w_out_bf, (i, 1)))
        if i == 0:
            assert kind != 2
            r = _ffn_first(ctx, x, mod, *ffn0, n_ctx_tiles)
        elif kind == 2:
            r, u = _ffn_s5_in(r, mod, *ffn0, norm_g[i, 1], n_ctx_tiles)
        else:
            r = _ffn(r, mod, *ffn0, n_ctx_tiles)
        tile0 = n_ctx_tiles if i == depth - 1 else 0
        if kind == 0:
            weights = _mla_weights(mla_w_in[j], mla_q_norm[j], mla_kv_norm[j], mla_w_uq[j], mla_w_ukv[j],
                                   mla_q_gain[j], mla_k_gain[j])
            q, k, v = _mla_proj(r, *mla_tabs, mod, norm_g[i, 1], weights, n_ctx_tiles)
            y = _attention(q, k, v, _attn_score_bound(mla_q_gain[j], mla_k_gain[j]), n_ctx)
            mix = (y, _rows_spec(y.shape[-1], tile0), [mla_w_o[j].astype(BF16)], _ffn_proj_kernel, "mla_out_swiglu")
        elif kind == 1:
            q, k, v, sg = _ret_proj(r, *ret_tabs, mod, norm_g[i, 1], ret_w_in[j].astype(BF16), n_ctx_tiles)
            y = _ret_scan(q, k, v, sg, ret_log1m_gamma[j], ret_gn_gain[j], n_ctx)
            mix = (y, _rows_spec(y.shape[-1], tile0), [ret_w_o[j].astype(BF16)], _ffn_proj_kernel, "ret_out_swiglu")
        else:
            y = _s5_mixer(u, s5_lam_re[j], s5_lam_im[j], s5_log_dt[j], s5_b_re[j], s5_b_im[j],
                          s5_c_re[j], s5_c_im[j], s5_d[j], n_ctx)
            mix = (y, _chunk_major_spec(y.shape[0], tile0), [s5_glu_w[j].astype(BF16), s5_glu_b[j].reshape(1, -1)],
                   _ffn_glu_kernel, "s5_out_swiglu")
        r = _ffn_after_mixer(r, *mix, mod, *ffn1, n_ctx_tiles, tile0)
    return r
```

```python
import functools
import math

import jax
import jax.numpy as jnp
import numpy as np
from jax import lax
from jax.experimental import pallas as pl
from jax.experimental.pallas import tpu as pltpu

F32 = jnp.float32
BF16 = jnp.bfloat16

EPS = 1e-6
ROPE_BASE = 10000.0
GRID_W = 64
N_MIXERS = 3
N_MOD = 9

MLA_HEADS = 8
MLA_NOPE = 128
MLA_ROPE = 64
MLA_V = 128
MLA_Q_RANK = 384
MLA_KV_RANK = 256
MLA_HEAD_PAD = 256

RET_HEADS = 4
RET_DK = 256
RET_DV = 512
SCAN_CHUNK = 128

S5_GROUP = 16
S5_STATE = 64

LANES = 128
MXU_WIDTH = 256
TOKEN_TILE = 256
ATTN_HEADS_PER_STEP = 4
FFN_BATCH_PER_STEP = 4
VMEM_LIMIT_BYTES = 56 * 1024 * 1024


def _cparams(*sem):
    return pltpu.CompilerParams(dimension_semantics=sem, vmem_limit_bytes=VMEM_LIMIT_BYTES)


def _resident(shape, single_buffer=False):
    zeros = (0,) * len(shape)
    if single_buffer:
        return pl.BlockSpec(shape, lambda *_: zeros, pipeline_mode=pl.Buffered(1))
    return pl.BlockSpec(shape, lambda *_: zeros)


def _silu(x):
    return x * jax.nn.sigmoid(x)


def _rms(x, n):
    return x * lax.rsqrt(jnp.sum(x * x, axis=-1, keepdims=True) * (1.0 / n) + EPS)


def _pre_norm(x, gain, mod_ref, k):
    y = _rms(x, x.shape[-1]) * gain
    return y * (1.0 + mod_ref[3 * k + 1:3 * k + 2, :]) + mod_ref[3 * k:3 * k + 1, :]


def _dot(a, b):
    return jnp.dot(a, b, preferred_element_type=F32)


def _dot_nt(a, b):
    return lax.dot_general(a, b, (((1,), (1,)), ((), ())), preferred_element_type=F32)


def _mod_kernel(cv_ref, w_ref, b_ref, o_ref):
    s = _silu(cv_ref[...]).astype(BF16)
    o_ref[...] = _dot(s, w_ref[...].astype(BF16)) + b_ref[...]


def _modulation(cvec, ada_w, ada_b):
    depth, d, n = ada_w.shape
    rows = cvec.shape[0]
    tn = n // 4
    return pl.pallas_call(
        _mod_kernel,
        grid=(depth, n // tn),
        in_specs=[
            _resident((rows, d)),
            pl.BlockSpec((None, d, tn), lambda i, j: (i, 0, j)),
            pl.BlockSpec((None, 1, tn), lambda i, j: (i, 0, j)),
        ],
        out_specs=pl.BlockSpec((None, rows, tn), lambda i, j: (i, 0, j)),
        out_shape=jax.ShapeDtypeStruct((depth, rows, n), F32),
        compiler_params=_cparams("arbitrary", "arbitrary"),
        name="adaln_modulation",
    )(cvec, ada_w, ada_b.reshape(depth, 1, n))


def _rows_spec(width, tile0=0):
    return pl.BlockSpec((FFN_BATCH_PER_STEP, TOKEN_TILE, width), lambda i, j: (i, j + tile0, 0))


def _mod_spec(d, n_ctx_tiles, tile0=0):
    return pl.BlockSpec((FFN_BATCH_PER_STEP, None, N_MOD, d),
                        lambda i, j: (i, (j + tile0 >= n_ctx_tiles).astype(jnp.int32), 0, 0))


def _const_spec(c):
    if not isinstance(c, tuple):
        return _resident(c.shape, single_buffer=True)
    stacked, lead = c
    tail = stacked.shape[len(lead):]
    return pl.BlockSpec((None,) * len(lead) + tail, lambda *_: lead + (0,) * len(tail), pipeline_mode=pl.Buffered(1))


def _const_array(c):
    return c[0] if isinstance(c, tuple) else c


def _row_call(kernel, name, b, n_tiles, inputs, consts, outs):
    in_specs = [spec for _, spec in inputs] + [_const_spec(c) for c in consts]
    return pl.pallas_call(
        kernel,
        grid=(b // FFN_BATCH_PER_STEP, n_tiles),
        in_specs=in_specs,
        out_specs=[_rows_spec(w) for w, _ in outs],
        out_shape=[jax.ShapeDtypeStruct((b, n_tiles * TOKEN_TILE, w), dt) for w, dt in outs],
        compiler_params=_cparams("arbitrary", "arbitrary"),
        name=name,
    )(*[a for a, _ in inputs], *[_const_array(c) for c in consts])


def _gelu_tanh(x):
    cdf = 0.5 * (1.0 + jnp.tanh(math.sqrt(2.0 / math.pi) * (x + 0.044715 * (x * x * x))))
    return x * cdf


def _swiglu_step(x, mod_ref, g_ref, win_ref, wout_ref, k):
    xn = _pre_norm(x, g_ref[...], mod_ref, k).astype(BF16)
    f = wout_ref.shape[0]
    half = (f // (2 * MXU_WIDTH) + 1) * MXU_WIDTH
    y = None
    for c0, c1 in ((0, half), (half, f)):
        a = _dot(xn, win_ref[:, c0:c1])
        b = _dot(xn, win_ref[:, f + c0:f + c1])
        part = _dot((_silu(a) * b).astype(BF16), wout_ref[c0:c1, :])
        y = part if y is None else y + part
    return x + (0.5 * mod_ref[3 * k + 2:3 * k + 3, :]) * y


def _ffn_first_kernel(ctx_ref, x_ref, mod_ref, g_ref, win_ref, wout_ref, o_ref, *, n_ctx_tiles):
    for e in range(FFN_BATCH_PER_STEP):
        x = jnp.where(pl.program_id(1) < n_ctx_tiles, ctx_ref[e], x_ref[e])
        o_ref[e] = _swiglu_step(x, mod_ref.at[e], g_ref, win_ref, wout_ref, 0)


def _ffn_kernel(r_ref, mod_ref, g_ref, win_ref, wout_ref, o_ref):
    for e in range(FFN_BATCH_PER_STEP):
        o_ref[e] = _swiglu_step(r_ref[e], mod_ref.at[e], g_ref, win_ref, wout_ref, 0)


def _ffn_proj_kernel(r_ref, y_ref, mod_ref, pw_ref, g_ref, win_ref, wout_ref, o_ref):
    for e in range(FFN_BATCH_PER_STEP):
        mod = mod_ref.at[e]
        x = r_ref[e] + mod[5:6, :] * _dot(y_ref[e], pw_ref[...])
        o_ref[e] = _swiglu_step(x, mod, g_ref, win_ref, wout_ref, 2)


def _ffn_glu_kernel(r_ref, y_ref, mod_ref, pw_ref, pb_ref, g_ref, win_ref, wout_ref, o_ref):
    d = r_ref.shape[-1]
    c = SCAN_CHUNK
    for e in range(FFN_BATCH_PER_STEP):
        mod = mod_ref.at[e]
        y = jnp.concatenate([y_ref[:, ci, e].reshape(d, c).T for ci in range(TOKEN_TILE // c)], axis=0)
        ag = _dot(_gelu_tanh(y).astype(BF16), pw_ref[...]) + pb_ref[...]
        x = r_ref[e] + mod[5:6, :] * (ag[:, :d] * jax.nn.sigmoid(ag[:, d:]))
        o_ref[e] = _swiglu_step(x, mod, g_ref, win_ref, wout_ref, 2)


def _ffn_first(ctx, x, mod, gain, w_in, w_out, n_ctx_tiles):
    b, s, d = x.shape
    last_ctx = n_ctx_tiles - 1
    eb = FFN_BATCH_PER_STEP
    inputs = [(ctx, pl.BlockSpec((eb, TOKEN_TILE, d), lambda i, j: (i, jnp.minimum(j, last_ctx), 0))),
              (x, pl.BlockSpec((eb, TOKEN_TILE, d), lambda i, j: (i, jnp.maximum(j - n_ctx_tiles, 0), 0))),
              (mod, _mod_spec(d, n_ctx_tiles))]
    (out,) = _row_call(functools.partial(_ffn_first_kernel, n_ctx_tiles=n_ctx_tiles), "swiglu_first",
                       b, n_ctx_tiles + s // TOKEN_TILE, inputs, [gain.reshape(1, d), w_in, w_out], [(d, F32)])
    return out


def _ffn(r, mod, gain, w_in, w_out, n_ctx_tiles):
    b, t, d = r.shape
    inputs = [(r, _rows_spec(d)), (mod, _mod_spec(d, n_ctx_tiles))]
    (out,) = _row_call(_ffn_kernel, "swiglu_half_step", b, t // TOKEN_TILE, inputs,
                       [gain.reshape(1, d), w_in, w_out], [(d, F32)])
    return out


def _ffn_after_mixer(r, y, y_spec, mix_consts, kern, name, mod, gain, w_in, w_out, n_ctx_tiles, tile0):
    b, t, d = r.shape
    inputs = [(r, _rows_spec(d, tile0)), (y, y_spec), (mod, _mod_spec(d, n_ctx_tiles, tile0))]
    (out,) = _row_call(kern, name, b, t // TOKEN_TILE - tile0, inputs,
                       mix_consts + [gain.reshape(1, d), w_in, w_out], [(d, F32)])
    return out


def _rope_tile_perm():
    src = np.full((LANES,), -1, np.int64)
    src[0:16] = np.arange(0, 16)
    src[16:32] = np.arange(32, 48)
    src[64:80] = np.arange(16, 32)
    src[80:96] = np.arange(48, 64)
    return src


def _pad_rope_cols(w):
    src = _rope_tile_perm()
    cols = jnp.take(w, jnp.asarray(np.maximum(src, 0)), axis=-1)
    return jnp.where(jnp.asarray(src >= 0), cols, 0.0)


def _mla_tables(n_ctx, n_lat):
    half = MLA_ROPE // 2
    inv = ROPE_BASE ** (-jnp.arange(0, half, 2, dtype=F32) / half)
    pos = jnp.arange(n_lat)
    ang_r = (pos // GRID_W).astype(F32)[:, None] * inv[None, :]
    ang_c = (pos % GRID_W).astype(F32)[:, None] * inv[None, :]
    z = jnp.zeros((n_lat, 32), F32)
    cos = jnp.concatenate([jnp.cos(ang_r), jnp.cos(ang_c), z, jnp.cos(ang_r), jnp.cos(ang_c), z], axis=-1)
    sin = jnp.concatenate([-jnp.sin(ang_r), -jnp.sin(ang_c), z, jnp.sin(ang_r), jnp.sin(ang_c), z], axis=-1)
    cos = jnp.concatenate([jnp.ones((n_ctx, LANES), F32), cos], axis=0)
    sin = jnp.concatenate([jnp.zeros((n_ctx, LANES), F32), sin], axis=0)
    return cos, sin


def _mla_proj_kernel(r_ref, cos_ref, sin_ref, mod_ref, *rest):
    q_ref, k_ref, v_ref = rest[-3:]
    for e in range(FFN_BATCH_PER_STEP):
        _mla_project(r_ref[e], cos_ref, sin_ref, mod_ref.at[e], *rest[:-3], q_ref.at[e], k_ref.at[e], v_ref.at[e])


def _mla_project(x, cos_ref, sin_ref, mod_ref, g_ref, win_ref, qn_ref, kvn_ref, wuq_ref, wukv_ref,
                 qg_ref, kg_ref, q_ref, k_ref, v_ref):
    xn = _pre_norm(x, g_ref[...], mod_ref, 1).astype(BF16)
    p = _dot(xn, win_ref[...])
    cos = cos_ref[...]
    sin = sin_ref[...]

    def rotate(x):
        return x * cos + pltpu.roll(x, LANES // 2, 1) * sin

    cq = (_rms(p[:, :MLA_Q_RANK], MLA_Q_RANK) * qn_ref[...]).astype(BF16)
    ckv = (_rms(p[:, MLA_Q_RANK:MLA_Q_RANK + MLA_KV_RANK], MLA_KV_RANK) * kvn_ref[...]).astype(BF16)
    kr = rotate(_rms(p[:, MLA_Q_RANK + MLA_KV_RANK:], MLA_ROPE) * kg_ref[:, LANES:]).astype(BF16)
    q = _dot(cq, wuq_ref[...])
    kv = _dot(ckv, wukv_ref[...])
    scale = (MLA_NOPE + MLA_ROPE) ** -0.5 * math.log2(math.e)
    ii = lax.broadcasted_iota(jnp.int32, (2 * LANES, 2 * LANES), 0)
    jj = lax.broadcasted_iota(jnp.int32, (2 * LANES, 2 * LANES), 1)
    same = (ii // LANES) == (jj // LANES)
    mean_q = jnp.where(same, jnp.where(ii < LANES, 1.0 / MLA_NOPE, 1.0 / MLA_ROPE), 0.0).astype(BF16)
    mean_k = jnp.where(same, 1.0 / MLA_NOPE, 0.0).astype(BF16)
    q_gain = qg_ref[...] * scale
    for h in range(MLA_HEADS):
        a0 = h * MLA_HEAD_PAD
        qh = q[:, a0:a0 + 2 * LANES]
        qn = qh * lax.rsqrt(_dot((qh * qh).astype(BF16), mean_q) + EPS) * q_gain
        q_ref[:, a0:a0 + LANES] = qn[:, :LANES].astype(BF16)
        q_ref[:, a0 + LANES:a0 + 2 * LANES] = rotate(qn[:, LANES:]).astype(BF16)
        k_ref[:, a0 + LANES:a0 + 2 * LANES] = kr
    for hh in range(MLA_HEADS // 2):
        kh = kv[:, 2 * hh * LANES:(2 * hh + 2) * LANES]
        kn = kh * lax.rsqrt(_dot((kh * kh).astype(BF16), mean_k) + EPS)
        for e in range(2):
            a0 = (2 * hh + e) * MLA_HEAD_PAD
            k_ref[:, a0:a0 + LANES] = (kn[:, e * LANES:(e + 1) * LANES] * kg_ref[:, :LANES]).astype(BF16)
    v_ref[...] = kv[:, MLA_HEADS * LANES:].astype(BF16)


def _mla_weights(w_in, q_norm, kv_norm, w_uq, w_ukv, q_gain, k_gain):
    lat = MLA_Q_RANK + MLA_KV_RANK
    w_in_p = jnp.concatenate([w_in[:, :lat], _pad_rope_cols(w_in[:, lat:])], axis=-1).astype(BF16)
    uq = w_uq.reshape(MLA_Q_RANK, MLA_HEADS, MLA_NOPE + MLA_ROPE)
    uq = jnp.concatenate([uq[..., :MLA_NOPE], _pad_rope_cols(uq[..., MLA_NOPE:])], axis=-1)
    uq = uq.reshape(MLA_Q_RANK, MLA_HEADS * MLA_HEAD_PAD).astype(BF16)
    ukv = w_ukv.reshape(MLA_KV_RANK, MLA_HEADS, MLA_NOPE + MLA_V)
    ukv = jnp.concatenate([ukv[..., :MLA_NOPE].reshape(MLA_KV_RANK, -1),
                           ukv[..., MLA_NOPE:].reshape(MLA_KV_RANK, -1)], axis=-1).astype(BF16)
    qg = jnp.concatenate([q_gain[:MLA_NOPE], _pad_rope_cols(q_gain[MLA_NOPE:])]).reshape(1, -1)
    kg = jnp.concatenate([k_gain[:MLA_NOPE], _pad_rope_cols(k_gain[MLA_NOPE:])]).reshape(1, -1)
    return [w_in_p, q_norm.reshape(1, -1), kv_norm.reshape(1, -1), uq, ukv, qg, kg]


def _mla_proj(r, cos, sin, mod, gain, weights, n_ctx_tiles):
    b, t, d = r.shape
    hw = MLA_HEADS * MLA_HEAD_PAD
    table = pl.BlockSpec((TOKEN_TILE, LANES), lambda i, j: (j, 0))
    inputs = [(r, _rows_spec(d)), (cos, table), (sin, table), (mod, _mod_spec(d, n_ctx_tiles))]
    return _row_call(_mla_proj_kernel, "mla_projections", b, t // TOKEN_TILE, inputs,
                     [gain.reshape(1, d)] + weights, [(hw, BF16), (hw, BF16), (MLA_HEADS * MLA_V, BF16)])


ATTN_MIN_ROW_SUM = 2.0 ** -100


def _attn_kernel(bound_ref, q_ref, k_ref, v_ref, o_ref, *, n_ctx_tiles, n_ctx):
    def attend(n_keys, use_bound):
        low = None
        for h in range(ATTN_HEADS_PER_STEP):
            qk = slice(h * MLA_HEAD_PAD, (h + 1) * MLA_HEAD_PAD)
            vo = slice(h * MLA_V, (h + 1) * MLA_V)
            s = _dot_nt(q_ref[:, qk], k_ref[:n_keys, qk])
            shift = bound_ref[0:1, 0:1] if use_bound else jnp.max(s, axis=-1, keepdims=True)
            p = jnp.exp2(s - shift)
            l = jnp.sum(p, axis=-1, keepdims=True)
            o = _dot(p.astype(BF16), v_ref[:n_keys, vo])
            o_ref[:, vo] = (o / l).astype(o_ref.dtype)
            low = l if low is None else jnp.minimum(low, l)
        return low

    def attend_keys(n_keys):
        low = attend(n_keys, True)
        safe = jnp.min(low) >= ATTN_MIN_ROW_SUM

        @pl.when(jnp.logical_not(safe))
        def _():
            attend(n_keys, False)

    is_ctx = pl.program_id(2) < n_ctx_tiles
    pl.when(is_ctx)(lambda: attend_keys(n_ctx))
    pl.when(jnp.logical_not(is_ctx))(lambda: attend_keys(k_ref.shape[0]))


def _attn_score_bound(q_gain, k_gain):
    scale = (MLA_NOPE + MLA_ROPE) ** -0.5 * math.log2(math.e)
    gq, gk = jnp.abs(q_gain), jnp.abs(k_gain)
    dot_max = (MLA_NOPE * jnp.max(gq[:MLA_NOPE]) * jnp.max(gk[:MLA_NOPE])
               + MLA_ROPE * jnp.max(gq[MLA_NOPE:]) * jnp.max(gk[MLA_NOPE:]))
    return jnp.full((8, LANES), 1.01 * scale, F32) * dot_max


def _attention(q, k, v, bound, n_ctx):
    b, t, _ = q.shape
    tq = TOKEN_TILE
    hp = ATTN_HEADS_PER_STEP
    kern = functools.partial(_attn_kernel, n_ctx_tiles=n_ctx // tq, n_ctx=n_ctx)
    return pl.pallas_call(
        kern,
        grid=(b, MLA_HEADS // hp, t // tq),
        in_specs=[_resident(bound.shape),
                  pl.BlockSpec((None, tq, hp * MLA_HEAD_PAD), lambda i, h, j: (i, j, h)),
                  pl.BlockSpec((None, t, hp * MLA_HEAD_PAD), lambda i, h, j: (i, 0, h)),
                  pl.BlockSpec((None, t, hp * MLA_V), lambda i, h, j: (i, 0, h))],
        out_specs=pl.BlockSpec((None, tq, hp * MLA_V), lambda i, h, j: (i, j, h)),
        out_shape=jax.ShapeDtypeStruct((b, t, MLA_HEADS * MLA_V), BF16),
        compiler_params=_cparams("arbitrary", "arbitrary", "arbitrary"),
        name="mla_attention",
    )(bound, q, k, v)


def _ret_tables(n_ctx, n_lat):
    inv = ROPE_BASE ** (-jnp.arange(0, RET_DK, 2, dtype=F32) / RET_DK)
    ang = jnp.arange(n_lat, dtype=F32)[:, None] * inv[None, :]
    cos = jnp.concatenate([jnp.ones((n_ctx, LANES), F32), jnp.cos(ang)], axis=0)
    sin = jnp.concatenate([jnp.zeros((n_ctx, LANES), F32), jnp.sin(ang)], axis=0)
    return cos, sin


def _ret_proj_kernel(r_ref, cos_ref, sin_ref, mod_ref, g_ref, win_ref, q_ref, k_ref, v_ref, sg_ref):
    for e in range(FFN_BATCH_PER_STEP):
        _ret_project(r_ref[e], cos_ref, sin_ref, mod_ref.at[e], g_ref, win_ref,
                     q_ref.at[e], k_ref.at[e], v_ref.at[e], sg_ref.at[e])


def _ret_project(x, cos_ref, sin_ref, mod_ref, g_ref, win_ref, q_ref, k_ref, v_ref, sg_ref):
    xn = _pre_norm(x, g_ref[...], mod_ref, 1).astype(BF16)
    cos = cos_ref[...]
    sin = sin_ref[...]
    hdk = RET_HEADS * RET_DK
    hdv = RET_HEADS * RET_DV
    scale = RET_DK ** -0.5
    for out_ref, base, sc in ((q_ref, 0, None), (k_ref, hdk, scale)):
        qk = _dot(xn, win_ref[:, base:base + hdk])
        for h in range(RET_HEADS):
            c0 = h * RET_DK
            x1 = qk[:, c0:c0 + LANES]
            x2 = qk[:, c0 + LANES:c0 + 2 * LANES]
            o1 = x1 * cos - x2 * sin
            o2 = x2 * cos + x1 * sin
            if sc is not None:
                o1 = o1 * sc
                o2 = o2 * sc
            out_ref[:, c0:c0 + LANES] = o1.astype(BF16)
            out_ref[:, c0 + LANES:c0 + 2 * LANES] = o2.astype(BF16)
    v_ref[...] = _dot(xn, win_ref[:, 2 * hdk:2 * hdk + hdv]).astype(BF16)
    sg_ref[...] = _silu(_dot(xn, win_ref[:, 2 * hdk + hdv:])).astype(BF16)


def _ret_proj(r, cos, sin, mod, gain, w_in, n_ctx_tiles):
    b, t, d = r.shape
    table = pl.BlockSpec((TOKEN_TILE, LANES), lambda i, j: (j, 0))
    inputs = [(r, _rows_spec(d)), (cos, table), (sin, table), (mod, _mod_spec(d, n_ctx_tiles))]
    widths = [(RET_HEADS * RET_DK, BF16), (RET_HEADS * RET_DK, BF16),
              (RET_HEADS * RET_DV, BF16), (RET_HEADS * RET_DV, BF16)]
    return _row_call(_ret_proj_kernel, "retention_projections", b, t // TOKEN_TILE, inputs,
                     [gain.reshape(1, d), w_in], widths)


def _ret_scan_kernel(l1g_ref, q_ref, k_ref, v_ref, sg_ref, gain_ref, y_ref, cross_scr, st_scr, *, n_ctx, n_all):
    c = SCAN_CHUNK
    log_g = jnp.log1p(-jnp.exp(l1g_ref[...]))
    lgf = log_g[0:1, 0:1]
    lgb = log_g[1:2, 0:1]
    ii = lax.broadcasted_iota(jnp.int32, (c, c), 0)
    jj = lax.broadcasted_iota(jnp.int32, (c, c), 1)
    diff = (ii - jj).astype(F32)
    mask = jnp.where(diff >= 0.0, jnp.exp(lgf * jnp.maximum(diff, 0.0)), jnp.exp(lgb * jnp.maximum(-diff, 0.0)))
    ic = lax.broadcasted_iota(jnp.int32, (c, 1), 0).astype(F32)
    qdec_f = jnp.exp(lgf * (ic + 1.0))
    qdec_b = jnp.exp(lgb * (c - ic))
    kdec_f = jnp.exp(lgf * (c - 1.0 - ic))
    kdec_b = jnp.exp(lgb * ic)
    cdec_f = jnp.exp(lgf * c)
    cdec_b = jnp.exp(lgb * c)

    def kv_outer(kc, dec, vc):
        return _dot((kc.astype(F32) * dec).T.astype(BF16), vc)

    sf_scr, sb_scr = st_scr.at[0], st_scr.at[1]
    sf_scr[...] = jnp.zeros_like(sf_scr)
    sb_scr[...] = jnp.zeros_like(sb_scr)
    order_b = list(range(n_ctx - 1, -1, -1)) + list(range(n_all - 1, n_ctx - 1, -1))
    seen = set()
    for nf, nb in zip(range(n_all), order_b):
        for n, s_ref, qdec, kdec, cdec in ((nf, sf_scr, qdec_f, kdec_f, cdec_f), (nb, sb_scr, qdec_b, kdec_b, cdec_b)):
            rows = slice(n * c, (n + 1) * c)
            state = s_ref[...]
            part = _dot((q_ref[rows, :].astype(F32) * qdec).astype(BF16), state.astype(BF16))
            if n in seen:
                cross_scr[rows, :] += part
            else:
                cross_scr[rows, :] = part
                seen.add(n)
            s_ref[...] = state * cdec + kv_outer(k_ref[rows, :], kdec, v_ref[rows, :])

    gain = gain_ref[...]
    for n in range(n_all):
        rows = slice(n * c, (n + 1) * c)
        scores = _dot_nt(q_ref[rows, :], k_ref[rows, :]) * mask
        o = _dot(scores.astype(BF16), v_ref[rows, :]) + cross_scr[rows, :]
        mu = jnp.mean(o, axis=-1, keepdims=True)
        dev = o - mu
        on = dev * lax.rsqrt(jnp.mean(dev * dev, axis=-1, keepdims=True) + EPS)
        y_ref[rows, :] = (on * gain * sg_ref[rows, :].astype(F32)).astype(y_ref.dtype)


def _ret_scan(q, k, v, sg, log1m_gamma, gn_gain, n_ctx):
    b, t, _ = q.shape
    l1g = jnp.broadcast_to(log1m_gamma.T[:, :, None], (RET_HEADS, 2, LANES))
    l1g = jnp.concatenate([l1g, jnp.zeros((RET_HEADS, 6, LANES), F32) - 1.0], axis=1)
    kern = functools.partial(_ret_scan_kernel, n_ctx=n_ctx // SCAN_CHUNK, n_all=t // SCAN_CHUNK)
    return pl.pallas_call(
        kern,
        grid=(b, RET_HEADS),
        in_specs=[pl.BlockSpec((None, 8, LANES), lambda i, h: (h, 0, 0)),
                  pl.BlockSpec((None, t, RET_DK), lambda i, h: (i, 0, h)),
                  pl.BlockSpec((None, t, RET_DK), lambda i, h: (i, 0, h)),
                  pl.BlockSpec((None, t, RET_DV), lambda i, h: (i, 0, h)),
                  pl.BlockSpec((None, t, RET_DV), lambda i, h: (i, 0, h)),
                  pl.BlockSpec((1, RET_DV), lambda i, h: (0, h))],
        out_specs=pl.BlockSpec((None, t, RET_DV), lambda i, h: (i, 0, h)),
        out_shape=jax.ShapeDtypeStruct((b, t, RET_HEADS * RET_DV), BF16),
        scratch_shapes=[pltpu.VMEM((t, RET_DV), F32), pltpu.VMEM((2, RET_DK, RET_DV), F32)],
        compiler_params=_cparams("arbitrary", "arbitrary"),
        name="retention_scan",
    )(l1g, q, k, v, sg, gn_gain.reshape(1, -1))


CHUNKS_PER_TILE = TOKEN_TILE // SCAN_CHUNK


def _chunk_major_spec(g, tile0=0):
    return pl.BlockSpec((g, CHUNKS_PER_TILE, FFN_BATCH_PER_STEP, S5_GROUP, SCAN_CHUNK),
                        lambda i, j: (0, j + tile0, i, 0, 0))


def _ffn_s5_in_kernel(r_ref, mod_ref, g_ref, win_ref, wout_ref, g1_ref, o_ref, u_ref):
    c = SCAN_CHUNK
    for e in range(FFN_BATCH_PER_STEP):
        mod = mod_ref.at[e]
        x = _swiglu_step(r_ref[e], mod, g_ref, win_ref, wout_ref, 0)
        o_ref[e] = x
        xn = _pre_norm(x, g1_ref[...], mod, 1)
        for ci in range(CHUNKS_PER_TILE):
            u_ref[:, ci, e] = xn[ci * c:(ci + 1) * c, :].T.reshape(u_ref.shape[0], S5_GROUP, c)


def _ffn_s5_in(r, mod, gain, w_in, w_out, mixer_gain, n_ctx_tiles):
    b, t, d = r.shape
    g = d // S5_GROUP
    consts = [gain.reshape(1, d), w_in, w_out, mixer_gain.reshape(1, d)]
    return pl.pallas_call(
        _ffn_s5_in_kernel,
        grid=(b // FFN_BATCH_PER_STEP, t // TOKEN_TILE),
        in_specs=[_rows_spec(d), _mod_spec(d, n_ctx_tiles)] + [_const_spec(c) for c in consts],
        out_specs=[_rows_spec(d), _chunk_major_spec(g)],
        out_shape=[jax.ShapeDtypeStruct((b, t, d), F32),
                   jax.ShapeDtypeStruct((g, t // SCAN_CHUNK, b, S5_GROUP, SCAN_CHUNK), F32)],
        compiler_params=_cparams("arbitrary", "arbitrary"),
        name="swiglu_s5_in",
    )(r, mod, *[_const_array(c) for c in consts])


def _s5_disc(lam_re, lam_im, log_dt):
    dt = jnp.exp(log_dt)
    a_re = lam_re * dt
    a_im = lam_im * dt
    mag = jnp.exp(a_re)
    e_re = mag * jnp.cos(a_im) - 1.0
    e_im = mag * jnp.sin(a_im)
    den = lam_re * lam_re + lam_im * lam_im
    return a_re, a_im, (e_re * lam_re + e_im * lam_im) / den, (e_im * lam_re - e_re * lam_im) / den


def _cmul(ar, ai, br, bi):
    return ar * br - ai * bi, ar * bi + ai * br


def _s5_kernel(u_ref, lamc_ref, lamr_ref, lamr2_ref, bc_ref, br_ref, ct_ref, d_ref,
               y_ref, t_scr, x_scr, w_scr, *, batch, n_ctx, n_all):
    c, m, p = SCAN_CHUNK, S5_GROUP, S5_STATE
    cm = c * m
    rows = n_all * batch
    for mi in range(m):
        x_scr[:, mi * c:(mi + 1) * c] = u_ref[pl.ds(mi, rows, stride=m), :].astype(BF16)

    kk = lax.broadcasted_iota(jnp.int32, (1, LANES), 1).astype(F32)

    def powers(a_re, a_im, expo):
        mag = jnp.exp(a_re * expo)
        return mag * jnp.cos(a_im * expo), mag * jnp.sin(a_im * expo)

    def lane_tiles(fn):
        parts = [fn(i) for i in range(m)]
        return (jnp.concatenate([q[0] for q in parts], axis=1), jnp.concatenate([q[1] for q in parts], axis=1))

    disc = [_s5_disc(lamc_ref[d][:, 0:1], lamc_ref[d][:, 1:2], lamc_ref[d][:, 2:3]) for d in range(2)]
    bbar = [_cmul(disc[d][2], disc[d][3], bc_ref[d, 0], bc_ref[d, 1]) for d in range(2)]
    ctr = [(ct_ref[d, 0], ct_ref[d, 1]) for d in range(2)]

    def col(pair, i):
        return pair[0][:, i:i + 1], pair[1][:, i:i + 1]

    f_up = powers(disc[0][0], disc[0][1], kk + 1.0)
    f_down = powers(disc[0][0], disc[0][1], (c - 1.0) - kk)
    b_up = powers(disc[1][0], disc[1][1], kk)
    b_down = powers(disc[1][0], disc[1][1], (c - 1.0) - kk)
    b_down1 = powers(disc[1][0], disc[1][1], c - kk)
    lag0 = jnp.where(kk == c - 1.0, 1.0, 0.0)

    q0f = lane_tiles(lambda mo: (col(ctr[0], mo)[0] * lag0, col(ctr[0], mo)[1] * lag0))
    q1f = lane_tiles(lambda mo: _cmul(*col(ctr[0], mo), *f_up))
    q0b = lane_tiles(lambda mo: _cmul(*col(ctr[1], mo), *b_down))
    qrb = lane_tiles(lambda mo: _cmul(*col(ctr[1], mo), *b_down1))

    hp = lax.Precision.HIGHEST
    btr = []
    for d in range(2):
        lam = lamr_ref[d]
        _, _, cf_re, cf_im = _s5_disc(lam[0:1, :], lam[1:2, :], lam[2:3, :])
        btr.append(_cmul(cf_re, cf_im, br_ref[d, 0], br_ref[d, 1]))

    def gen(bt, q):
        return (jnp.dot(bt[0], q[0], precision=hp, preferred_element_type=F32)
                - jnp.dot(bt[1], q[1], precision=hp, preferred_element_type=F32))

    w0 = gen(btr[0], q0f) + gen(btr[1], q0b)
    w1 = gen(btr[0], q1f)
    for mo in range(m):
        w_scr[:, 2 * mo * c:(2 * mo + 1) * c] = w0[:, mo * c:(mo + 1) * c]
        w_scr[:, (2 * mo + 1) * c:(2 * mo + 2) * c] = w1[:, mo * c:(mo + 1) * c]

    for mi in range(m):
        row = jnp.broadcast_to(w_scr[mi:mi + 1, :], (c, 2 * cm))
        win = pltpu.roll(row, 2 * cm - (c - 1), 1, stride=1, stride_axis=0)
        blk = jnp.concatenate([win[:, 2 * mo * c:(2 * mo + 1) * c] for mo in range(m)], axis=1)
        t_scr[mi * c:(mi + 1) * c, :] = blk.astype(BF16)

    for d, q in ((0, q1f), (1, qrb)):
        base = cm + 2 * p * d
        t_scr[base:base + p, :] = q[0].astype(BF16)
        t_scr[base + p:base + 2 * p, :] = (-q[1]).astype(BF16)

    wb_f = lane_tiles(lambda mi: _cmul(*col(bbar[0], mi), *f_down))
    wb_b = lane_tiles(lambda mi: _cmul(*col(bbar[1], mi), *b_up))
    wb = jnp.concatenate([wb_f[0], wb_f[1], wb_b[0], wb_b[1]], axis=0).astype(BF16)
    hloc = _dot_nt(x_scr[:, :cm], wb)

    sign = jnp.where(lax.broadcasted_iota(jnp.int32, (1, 2 * p), 1) < p, -1.0, 1.0)
    order_f = list(range(n_all))
    order_b = list(range(n_ctx - 1, -1, -1)) + list(range(n_all - 1, n_ctx - 1, -1))
    for d, order in ((0, order_f), (1, order_b)):
        lam = lamr2_ref[d]
        a_re, a_im, _, _ = _s5_disc(lam[0:1, :], lam[1:2, :], lam[2:3, :])
        mag = jnp.exp(a_re * c)
        ac_r = mag * jnp.cos(a_im * c)
        ac_i = mag * jnp.sin(a_im * c) * sign
        state = jnp.zeros((batch, 2 * p), F32)
        colx = cm + 2 * p * d
        for n in order:
            x_scr[n * batch:(n + 1) * batch, colx:colx + 2 * p] = state.astype(BF16)
            local = hloc[n * batch:(n + 1) * batch, 2 * p * d:2 * p * (d + 1)]
            state = state * ac_r + pltpu.roll(state, p, 1) * ac_i + local

    y = _dot(x_scr[...], t_scr[...])
    for mo in range(m):
        skip = d_ref[mo:mo + 1, :] * u_ref[pl.ds(mo, rows, stride=m), :]
        y_ref[pl.ds(mo, rows, stride=m), :] = y[:, mo * c:(mo + 1) * c] + skip


def _s5_operands(u, lam_re, lam_im, log_dt, b_re, b_im, c_re, c_im, d_skip):
    g, n_all, batch, m, c = u.shape
    dtb = jnp.broadcast_to(log_dt[:, :, None], lam_re.shape)
    lam3 = jnp.stack([lam_re, lam_im, dtb], axis=-1)
    lamc = jnp.pad(lam3, ((0, 0), (0, 0), (0, 0), (0, LANES - 3))).transpose(1, 0, 2, 3)
    lam3r = jnp.stack([lam_re, lam_im, dtb], axis=2)
    lamr = jnp.pad(lam3r, ((0, 0), (0, 0), (0, 5), (0, 0))).transpose(1, 0, 2, 3)
    lamr2 = jnp.concatenate([lamr, lamr], axis=-1)
    bcol = jnp.stack([b_re, b_im], axis=1)
    bc = jnp.pad(bcol, ((0, 0),) * 4 + ((0, LANES - m),)).transpose(2, 0, 1, 3, 4)
    brow = jnp.swapaxes(bcol, -1, -2).transpose(2, 0, 1, 3, 4)
    ctc = jnp.swapaxes(jnp.stack([c_re, c_im], axis=1), -1, -2)
    ctc = jnp.pad(ctc, ((0, 0),) * 4 + ((0, LANES - m),)).transpose(2, 0, 1, 3, 4)
    dvec = jnp.broadcast_to(d_skip.reshape(g, m, 1), (g, m, LANES))
    return (u.reshape(g, n_all * batch * m, c), lamc, lamr, lamr2, bc, brow, ctc, dvec)


def _s5_mixer(u, lam_re, lam_im, log_dt, b_re, b_im, c_re, c_im, d_skip, n_ctx):
    g, n_all, b, m, c = u.shape
    p = S5_STATE
    rows = n_all * b
    cm = c * m
    kern = functools.partial(_s5_kernel, batch=b, n_ctx=n_ctx // c, n_all=n_all)
    per_g = lambda *tail: pl.BlockSpec((None,) + tail, lambda i: (i,) + (0,) * len(tail))
    y = pl.pallas_call(
        kern,
        grid=(g,),
        in_specs=[per_g(rows * m, c), per_g(2, p, LANES), per_g(2, 8, p), per_g(2, 8, 2 * p),
                  per_g(2, 2, p, LANES), per_g(2, 2, m, p), per_g(2, 2, p, LANES), per_g(m, LANES)],
        out_specs=per_g(rows * m, c),
        out_shape=jax.ShapeDtypeStruct((g, rows * m, c), F32),
        scratch_shapes=[pltpu.VMEM((cm + 4 * p, cm), BF16), pltpu.VMEM((rows, cm + 4 * p), BF16),
                        pltpu.VMEM((m, 2 * cm), F32)],
        compiler_params=_cparams("arbitrary"),
        name="s5_scan",
    )(*_s5_operands(u, lam_re, lam_im, log_dt, b_re, b_im, c_re, c_im, d_skip))
    return y.reshape(u.shape)


def kernel(x, c, ctx, c_ctx, ada_w, ada_b, norm_g, ffn_w_in, ffn_w_out, mla_w_in, mla_q_norm, mla_kv_norm,
           mla_w_uq, mla_w_ukv, mla_q_gain, mla_k_gain, mla_w_o, ret_w_in, ret_log1m_gamma, ret_gn_gain, ret_w_o,
           s5_lam_re, s5_lam_im, s5_log_dt, s5_b_re, s5_b_im, s5_c_re, s5_c_im, s5_d, s5_glu_w, s5_glu_b):
    b, s, d = x.shape
    n_ctx = ctx.shape[1]
    depth = ada_w.shape[0]
    assert n_ctx % TOKEN_TILE == 0 and s % TOKEN_TILE == 0 and s % GRID_W == 0 and b % FFN_BATCH_PER_STEP == 0
    n_ctx_tiles = n_ctx // TOKEN_TILE

    pad = (-(b + 1)) % 8
    cvec = jnp.concatenate([c, c_ctx[None, :], jnp.zeros((pad, d), F32)], axis=0)
    mod_all = _modulation(cvec, ada_w, ada_b)
    mod_lat = mod_all[:, :b].reshape(depth, b, 1, N_MOD, d)
    mod_ctx = jnp.broadcast_to(mod_all[:, b].reshape(depth, 1, 1, N_MOD, d), (depth, b, 1, N_MOD, d))
    mods = jnp.concatenate([mod_ctx, mod_lat], axis=2)

    mla_tabs = _mla_tables(n_ctx, s)
    ret_tabs = _ret_tables(n_ctx, s)

    ffn_w_in_bf = ffn_w_in.astype(BF16)
    ffn_w_out_bf = ffn_w_out.astype(BF16)
    r = None
    for i in range(depth):
        kind, j = i % N_MIXERS, i // N_MIXERS
        mod = mods[i]
        ffn0 = (norm_g[i, 0], (ffn_w_in_bf, (i, 0)), (ffn_w_out_bf, (i, 0)))
        ffn1 = (norm_g[i, 2], (ffn_w_in_bf, (i, 1)), (ffn_w_out_bf, (i, 1)))
        if i == 0:
            assert kind != 2
            r = _ffn_first(ctx, x, mod, *ffn0, n_ctx_tiles)
        elif kind == 2:
            r, u = _ffn_s5_in(r, mod, *ffn0, norm_g[i, 1], n_ctx_tiles)
        else:
            r = _ffn(r, mod, *ffn0, n_ctx_tiles)
        tile0 = n_ctx_tiles if i == depth - 1 else 0
        if kind == 0:
            weights = _mla_weights(mla_w_in[j], mla_q_norm[j], mla_kv_norm[j], mla_w_uq[j], mla_w_ukv[j],
                                   mla_q_gain[j], mla_k_gain[j])
            q, k, v = _mla_proj(r, *mla_tabs, mod, norm_g[i, 1], weights, n_ctx_tiles)
            y = _attention(q, k, v, _attn_score_bound(mla_q_gain[j], mla_k_gain[j]), n_ctx)
            mix = (y, _rows_spec(y.shape[-1], tile0), [mla_w_o[j].astype(BF16)], _ffn_proj_kernel, "mla_out_swiglu")
        elif kind == 1:
            q, k, v, sg = _ret_proj(r, *ret_tabs, mod, norm_g[i, 1], ret_w_in[j].astype(BF16), n_ctx_tiles)
            y = _ret_scan(q, k, v, sg, ret_log1m_gamma[j], ret_gn_gain[j], n_ctx)
            mix = (y, _rows_spec(y.shape[-1], tile0), [ret_w_o[j].astype(BF16)], _ffn_proj_kernel, "ret_out_swiglu")
        else:
            y = _s5_mixer(u, s5_lam_re[j], s5_lam_im[j], s5_log_dt[j], s5_b_re[j], s5_b_im[j],
                          s5_c_re[j], s5_c_im[j], s5_d[j], n_ctx)
            mix = (y, _chunk_major_spec(y.shape[0], tile0), [s5_glu_w[j].astype(BF16), s5_glu_b[j].reshape(1, -1)],
                   _ffn_glu_kernel, "s5_out_swiglu")
        r = _ffn_after_mixer(r, *mix, mod, *ffn1, n_ctx_tiles, tile0)
    return r
```

```python
import functools
import math

import jax
import jax.numpy as jnp
import numpy as np
from jax import lax
from jax.experimental import pallas as pl
from jax.experimental.pallas import tpu as pltpu

F32 = jnp.float32
BF16 = jnp.bfloat16

EPS = 1e-6
ROPE_BASE = 10000.0
GRID_W = 64
N_MIXERS = 3
N_MOD = 9

MLA_HEADS = 8
MLA_NOPE = 128
MLA_ROPE = 64
MLA_V = 128
MLA_Q_RANK = 384
MLA_KV_RANK = 256
MLA_HEAD_PAD = 256

RET_HEADS = 4
RET_DK = 256
RET_DV = 512
SCAN_CHUNK = 128

S5_GROUP = 16
S5_ROW_PITCH = 24
S5_STATE = 64

LANES = 128
MXU_WIDTH = 256
TOKEN_TILE = 256
ATTN_HEADS_PER_STEP = 4
FFN_BATCH_PER_STEP = 4
VMEM_LIMIT_BYTES = 56 * 1024 * 1024


def _cparams(*sem):
    return pltpu.CompilerParams(dimension_semantics=sem, vmem_limit_bytes=VMEM_LIMIT_BYTES)


def _resident(shape, single_buffer=False):
    zeros = (0,) * len(shape)
    if single_buffer:
        return pl.BlockSpec(shape, lambda *_: zeros, pipeline_mode=pl.Buffered(1))
    return pl.BlockSpec(shape, lambda *_: zeros)


def _silu(x):
    return x * jax.nn.sigmoid(x)


def _rms(x, n):
    return x * lax.rsqrt(jnp.sum(x * x, axis=-1, keepdims=True) * (1.0 / n) + EPS)


def _pre_norm(x, gain, mod_ref, k):
    y = _rms(x, x.shape[-1]) * gain
    return y * (1.0 + mod_ref[3 * k + 1:3 * k + 2, :]) + mod_ref[3 * k:3 * k + 1, :]


def _dot(a, b):
    return jnp.dot(a, b, preferred_element_type=F32)


def _dot_nt(a, b):
    return lax.dot_general(a, b, (((1,), (1,)), ((), ())), preferred_element_type=F32)


def _mod_kernel(cv_ref, w_ref, b_ref, o_ref):
    s = _silu(cv_ref[...]).astype(BF16)
    o_ref[...] = _dot(s, w_ref[...].astype(BF16)) + b_ref[...]


def _modulation(cvec, ada_w, ada_b):
    depth, d, n = ada_w.shape
    rows = cvec.shape[0]
    tn = n // 4
    return pl.pallas_call(
        _mod_kernel,
        grid=(depth, n // tn),
        in_specs=[
            _resident((rows, d)),
            pl.BlockSpec((None, d, tn), lambda i, j: (i, 0, j)),
            pl.BlockSpec((None, 1, tn), lambda i, j: (i, 0, j)),
        ],
        out_specs=pl.BlockSpec((None, rows, tn), lambda i, j: (i, 0, j)),
        out_shape=jax.ShapeDtypeStruct((depth, rows, n), F32),
        compiler_params=_cparams("arbitrary", "arbitrary"),
        name="adaln_modulation",
    )(cvec, ada_w, ada_b.reshape(depth, 1, n))


def _rows_spec(width, tile0=0):
    return pl.BlockSpec((FFN_BATCH_PER_STEP, TOKEN_TILE, width), lambda i, j: (i, j + tile0, 0))


def _mod_spec(d, n_ctx_tiles, tile0=0):
    return pl.BlockSpec((FFN_BATCH_PER_STEP, None, N_MOD, d),
                        lambda i, j: (i, (j + tile0 >= n_ctx_tiles).astype(jnp.int32), 0, 0))


def _const_spec(c):
    if not isinstance(c, tuple):
        return _resident(c.shape, single_buffer=True)
    stacked, lead = c
    tail = stacked.shape[len(lead):]
    return pl.BlockSpec((None,) * len(lead) + tail, lambda *_: lead + (0,) * len(tail), pipeline_mode=pl.Buffered(1))


def _const_array(c):
    return c[0] if isinstance(c, tuple) else c


def _row_call(kernel, name, b, n_tiles, inputs, consts, outs):
    in_specs = [spec for _, spec in inputs] + [_const_spec(c) for c in consts]
    return pl.pallas_call(
        kernel,
        grid=(b // FFN_BATCH_PER_STEP, n_tiles),
        in_specs=in_specs,
        out_specs=[_rows_spec(w) for w, _ in outs],
        out_shape=[jax.ShapeDtypeStruct((b, n_tiles * TOKEN_TILE, w), dt) for w, dt in outs],
        compiler_params=_cparams("arbitrary", "arbitrary"),
        name=name,
    )(*[a for a, _ in inputs], *[_const_array(c) for c in consts])


def _gelu_tanh(x):
    cdf = 0.5 * (1.0 + jnp.tanh(math.sqrt(2.0 / math.pi) * (x + 0.044715 * (x * x * x))))
    return x * cdf


def _swiglu_step(x, mod_ref, g_ref, win_ref, wout_ref, k):
    xn = _pre_norm(x, g_ref[...], mod_ref, k).astype(BF16)
    f = wout_ref.shape[0]
    half = (f // (2 * MXU_WIDTH) + 1) * MXU_WIDTH
    y = None
    for c0, c1 in ((0, half), (half, f)):
        a = _dot(xn, win_ref[:, c0:c1])
        b = _dot(xn, win_ref[:, f + c0:f + c1])
        part = _dot((_silu(a) * b).astype(BF16), wout_ref[c0:c1, :])
        y = part if y is None else y + part
    return x + (0.5 * mod_ref[3 * k + 2:3 * k + 3, :]) * y


def _ffn_first_kernel(ctx_ref, x_ref, mod_ref, g_ref, win_ref, wout_ref, o_ref, *, n_ctx_tiles):
    for e in range(FFN_BATCH_PER_STEP):
        x = jnp.where(pl.program_id(1) < n_ctx_tiles, ctx_ref[e], x_ref[e])
        o_ref[e] = _swiglu_step(x, mod_ref.at[e], g_ref, win_ref, wout_ref, 0)


def _ffn_kernel(r_ref, mod_ref, g_ref, win_ref, wout_ref, o_ref):
    for e in range(FFN_BATCH_PER_STEP):
        o_ref[e] = _swiglu_step(r_ref[e], mod_ref.at[e], g_ref, win_ref, wout_ref, 0)


def _ffn_proj_kernel(r_ref, y_ref, mod_ref, pw_ref, g_ref, win_ref, wout_ref, o_ref):
    for e in range(FFN_BATCH_PER_STEP):
        mod = mod_ref.at[e]
        x = r_ref[e] + mod[5:6, :] * _dot(y_ref[e], pw_ref[...])
        o_ref[e] = _swiglu_step(x, mod, g_ref, win_ref, wout_ref, 2)


def _ffn_glu_kernel(r_ref, y_ref, mod_ref, pw_ref, pb_ref, g_ref, win_ref, wout_ref, o_ref):
    d = r_ref.shape[-1]
    c = SCAN_CHUNK
    for e in range(FFN_BATCH_PER_STEP):
        mod = mod_ref.at[e]
        y = jnp.concatenate([y_ref[:, ci, e, :S5_GROUP, :].reshape(d, c).T for ci in range(TOKEN_TILE // c)], axis=0)
        ag = _dot(_gelu_tanh(y).astype(BF16), pw_ref[...]) + pb_ref[...]
        x = r_ref[e] + mod[5:6, :] * (ag[:, :d] * jax.nn.sigmoid(ag[:, d:]))
        o_ref[e] = _swiglu_step(x, mod, g_ref, win_ref, wout_ref, 2)


def _ffn_first(ctx, x, mod, gain, w_in, w_out, n_ctx_tiles):
    b, s, d = x.shape
    last_ctx = n_ctx_tiles - 1
    eb = FFN_BATCH_PER_STEP
    inputs = [(ctx, pl.BlockSpec((eb, TOKEN_TILE, d), lambda i, j: (i, jnp.minimum(j, last_ctx), 0))),
              (x, pl.BlockSpec((eb, TOKEN_TILE, d), lambda i, j: (i, jnp.maximum(j - n_ctx_tiles, 0), 0))),
              (mod, _mod_spec(d, n_ctx_tiles))]
    (out,) = _row_call(functools.partial(_ffn_first_kernel, n_ctx_tiles=n_ctx_tiles), "swiglu_first",
                       b, n_ctx_tiles + s // TOKEN_TILE, inputs, [gain.reshape(1, d), w_in, w_out], [(d, F32)])
    return out


def _ffn(r, mod, gain, w_in, w_out, n_ctx_tiles):
    b, t, d = r.shape
    inputs = [(r, _rows_spec(d)), (mod, _mod_spec(d, n_ctx_tiles))]
    (out,) = _row_call(_ffn_kernel, "swiglu_half_step", b, t // TOKEN_TILE, inputs,
                       [gain.reshape(1, d), w_in, w_out], [(d, F32)])
    return out


def _ffn_after_mixer(r, y, y_spec, mix_consts, kern, name, mod, gain, w_in, w_out, n_ctx_tiles, tile0):
    b, t, d = r.shape
    inputs = [(r, _rows_spec(d, tile0)), (y, y_spec), (mod, _mod_spec(d, n_ctx_tiles, tile0))]
    (out,) = _row_call(kern, name, b, t // TOKEN_TILE - tile0, inputs,
                       mix_consts + [gain.reshape(1, d), w_in, w_out], [(d, F32)])
    return out


def _rope_tile_perm():
    src = np.full((LANES,), -1, np.int64)
    src[0:16] = np.arange(0, 16)
    src[16:32] = np.arange(32, 48)
    src[64:80] = np.arange(16, 32)
    src[80:96] = np.arange(48, 64)
    return src


def _pad_rope_cols(w):
    src = _rope_tile_perm()
    cols = jnp.take(w, jnp.asarray(np.maximum(src, 0)), axis=-1)
    return jnp.where(jnp.asarray(src >= 0), cols, 0.0)


def _mla_tables(n_ctx, n_lat):
    half = MLA_ROPE // 2
    inv = ROPE_BASE ** (-jnp.arange(0, half, 2, dtype=F32) / half)
    pos = jnp.arange(n_lat)
    ang_r = (pos // GRID_W).astype(F32)[:, None] * inv[None, :]
    ang_c = (pos % GRID_W).astype(F32)[:, None] * inv[None, :]
    z = jnp.zeros((n_lat, 32), F32)
    cos = jnp.concatenate([jnp.cos(ang_r), jnp.cos(ang_c), z, jnp.cos(ang_r), jnp.cos(ang_c), z], axis=-1)
    sin = jnp.concatenate([-jnp.sin(ang_r), -jnp.sin(ang_c), z, jnp.sin(ang_r), jnp.sin(ang_c), z], axis=-1)
    cos = jnp.concatenate([jnp.ones((n_ctx, LANES), F32), cos], axis=0)
    sin = jnp.concatenate([jnp.zeros((n_ctx, LANES), F32), sin], axis=0)
    return cos, sin


def _mla_proj_kernel(r_ref, cos_ref, sin_ref, mod_ref, *rest):
    q_ref, k_ref, v_ref = rest[-3:]
    for e in range(FFN_BATCH_PER_STEP):
        _mla_project(r_ref[e], cos_ref, sin_ref, mod_ref.at[e], *rest[:-3], q_ref.at[e], k_ref.at[e], v_ref.at[e])


def _mla_project(x, cos_ref, sin_ref, mod_ref, g_ref, win_ref, qn_ref, kvn_ref, wuq_ref, wukv_ref,
                 qg_ref, kg_ref, q_ref, k_ref, v_ref):
    xn = _pre_norm(x, g_ref[...], mod_ref, 1).astype(BF16)
    p = _dot(xn, win_ref[...])
    cos = cos_ref[...]
    sin = sin_ref[...]

    def rotate(x):
        return x * cos + pltpu.roll(x, LANES // 2, 1) * sin

    cq = (_rms(p[:, :MLA_Q_RANK], MLA_Q_RANK) * qn_ref[...]).astype(BF16)
    ckv = (_rms(p[:, MLA_Q_RANK:MLA_Q_RANK + MLA_KV_RANK], MLA_KV_RANK) * kvn_ref[...]).astype(BF16)
    kr = rotate(_rms(p[:, MLA_Q_RANK + MLA_KV_RANK:], MLA_ROPE) * kg_ref[:, LANES:]).astype(BF16)
    q = _dot(cq, wuq_ref[...])
    kv = _dot(ckv, wukv_ref[...])
    scale = (MLA_NOPE + MLA_ROPE) ** -0.5 * math.log2(math.e)
    ii = lax.broadcasted_iota(jnp.int32, (2 * LANES, 2 * LANES), 0)
    jj = lax.broadcasted_iota(jnp.int32, (2 * LANES, 2 * LANES), 1)
    same = (ii // LANES) == (jj // LANES)
    mean_q = jnp.where(same, jnp.where(ii < LANES, 1.0 / MLA_NOPE, 1.0 / MLA_ROPE), 0.0).astype(BF16)
    mean_k = jnp.where(same, 1.0 / MLA_NOPE, 0.0).astype(BF16)
    q_gain = qg_ref[...] * scale
    for h in range(MLA_HEADS):
        a0 = h * MLA_HEAD_PAD
        qh = q[:, a0:a0 + 2 * LANES]
        qn = qh * lax.rsqrt(_dot((qh * qh).astype(BF16), mean_q) + EPS) * q_gain
        q_ref[:, a0:a0 + LANES] = qn[:, :LANES].astype(BF16)
        q_ref[:, a0 + LANES:a0 + 2 * LANES] = rotate(qn[:, LANES:]).astype(BF16)
        k_ref[:, a0 + LANES:a0 + 2 * LANES] = kr
    for hh in range(MLA_HEADS // 2):
        kh = kv[:, 2 * hh * LANES:(2 * hh + 2) * LANES]
        kn = kh * lax.rsqrt(_dot((kh * kh).astype(BF16), mean_k) + EPS)
        for e in range(2):
            a0 = (2 * hh + e) * MLA_HEAD_PAD
            k_ref[:, a0:a0 + LANES] = (kn[:, e * LANES:(e + 1) * LANES] * kg_ref[:, :LANES]).astype(BF16)
    v_ref[...] = kv[:, MLA_HEADS * LANES:].astype(BF16)


def _mla_weights(w_in, q_norm, kv_norm, w_uq, w_ukv, q_gain, k_gain):
    lat = MLA_Q_RANK + MLA_KV_RANK
    w_in_p = jnp.concatenate([w_in[:, :lat], _pad_rope_cols(w_in[:, lat:])], axis=-1).astype(BF16)
    uq = w_uq.reshape(MLA_Q_RANK, MLA_HEADS, MLA_NOPE + MLA_ROPE)
    uq = jnp.concatenate([uq[..., :MLA_NOPE], _pad_rope_cols(uq[..., MLA_NOPE:])], axis=-1)
    uq = uq.reshape(MLA_Q_RANK, MLA_HEADS * MLA_HEAD_PAD).astype(BF16)
    ukv = w_ukv.reshape(MLA_KV_RANK, MLA_HEADS, MLA_NOPE + MLA_V)
    ukv = jnp.concatenate([ukv[..., :MLA_NOPE].reshape(MLA_KV_RANK, -1),
                           ukv[..., MLA_NOPE:].reshape(MLA_KV_RANK, -1)], axis=-1).astype(BF16)
    qg = jnp.concatenate([q_gain[:MLA_NOPE], _pad_rope_cols(q_gain[MLA_NOPE:])]).reshape(1, -1)
    kg = jnp.concatenate([k_gain[:MLA_NOPE], _pad_rope_cols(k_gain[MLA_NOPE:])]).reshape(1, -1)
    return [w_in_p, q_norm.reshape(1, -1), kv_norm.reshape(1, -1), uq, ukv, qg, kg]


def _mla_proj(r, cos, sin, mod, gain, weights, n_ctx_tiles):
    b, t, d = r.shape
    hw = MLA_HEADS * MLA_HEAD_PAD
    table = pl.BlockSpec((TOKEN_TILE, LANES), lambda i, j: (j, 0))
    inputs = [(r, _rows_spec(d)), (cos, table), (sin, table), (mod, _mod_spec(d, n_ctx_tiles))]
    return _row_call(_mla_proj_kernel, "mla_projections", b, t // TOKEN_TILE, inputs,
                     [gain.reshape(1, d)] + weights, [(hw, BF16), (hw, BF16), (MLA_HEADS * MLA_V, BF16)])


ATTN_MIN_ROW_SUM = 2.0 ** -100


def _attn_kernel(bound_ref, q_ref, k_ref, v_ref, o_ref, *, n_ctx_tiles, n_ctx):
    def attend(n_keys, use_bound):
        low = None
        for h in range(ATTN_HEADS_PER_STEP):
            qk = slice(h * MLA_HEAD_PAD, (h + 1) * MLA_HEAD_PAD)
            vo = slice(h * MLA_V, (h + 1) * MLA_V)
            s = _dot_nt(q_ref[:, qk], k_ref[:n_keys, qk])
            shift = bound_ref[0:1, 0:1] if use_bound else jnp.max(s, axis=-1, keepdims=True)
            p = jnp.exp2(s - shift)
            l = jnp.sum(p, axis=-1, keepdims=True)
            o = _dot(p.astype(BF16), v_ref[:n_keys, vo])
            o_ref[:, vo] = (o / l).astype(o_ref.dtype)
            low = l if low is None else jnp.minimum(low, l)
        return low

    def attend_keys(n_keys):
        low = attend(n_keys, True)
        safe = jnp.min(low) >= ATTN_MIN_ROW_SUM

        @pl.when(jnp.logical_not(safe))
        def _():
            attend(n_keys, False)

    is_ctx = pl.program_id(2) < n_ctx_tiles
    pl.when(is_ctx)(lambda: attend_keys(n_ctx))
    pl.when(jnp.logical_not(is_ctx))(lambda: attend_keys(k_ref.shape[0]))


def _attn_score_bound(q_gain, k_gain):
    scale = (MLA_NOPE + MLA_ROPE) ** -0.5 * math.log2(math.e)
    gq, gk = jnp.abs(q_gain), jnp.abs(k_gain)
    dot_max = (MLA_NOPE * jnp.max(gq[:MLA_NOPE]) * jnp.max(gk[:MLA_NOPE])
               + MLA_ROPE * jnp.max(gq[MLA_NOPE:]) * jnp.max(gk[MLA_NOPE:]))
    return jnp.full((8, LANES), 1.01 * scale, F32) * dot_max


def _attention(q, k, v, bound, n_ctx):
    b, t, _ = q.shape
    tq = TOKEN_TILE
    hp = ATTN_HEADS_PER_STEP
    kern = functools.partial(_attn_kernel, n_ctx_tiles=n_ctx // tq, n_ctx=n_ctx)
    return pl.pallas_call(
        kern,
        grid=(b, MLA_HEADS // hp, t // tq),
        in_specs=[_resident(bound.shape),
                  pl.BlockSpec((None, tq, hp * MLA_HEAD_PAD), lambda i, h, j: (i, j, h)),
                  pl.BlockSpec((None, t, hp * MLA_HEAD_PAD), lambda i, h, j: (i, 0, h)),
                  pl.BlockSpec((None, t, hp * MLA_V), lambda i, h, j: (i, 0, h))],
        out_specs=pl.BlockSpec((None, tq, hp * MLA_V), lambda i, h, j: (i, j, h)),
        out_shape=jax.ShapeDtypeStruct((b, t, MLA_HEADS * MLA_V), BF16),
        compiler_params=_cparams("arbitrary", "arbitrary", "arbitrary"),
        name="mla_attention",
    )(bound, q, k, v)


def _ret_tables(n_ctx, n_lat):
    inv = ROPE_BASE ** (-jnp.arange(0, RET_DK, 2, dtype=F32) / RET_DK)
    ang = jnp.arange(n_lat, dtype=F32)[:, None] * inv[None, :]
    cos = jnp.concatenate([jnp.ones((n_ctx, LANES), F32), jnp.cos(ang)], axis=0)
    sin = jnp.concatenate([jnp.zeros((n_ctx, LANES), F32), jnp.sin(ang)], axis=0)
    return cos, sin


def _ret_proj_kernel(r_ref, cos_ref, sin_ref, mod_ref, g_ref, win_ref, q_ref, k_ref, v_ref, sg_ref):
    for e in range(FFN_BATCH_PER_STEP):
        _ret_project(r_ref[e], cos_ref, sin_ref, mod_ref.at[e], g_ref, win_ref,
                     q_ref.at[e], k_ref.at[e], v_ref.at[e], sg_ref.at[e])


def _ret_project(x, cos_ref, sin_ref, mod_ref, g_ref, win_ref, q_ref, k_ref, v_ref, sg_ref):
    xn = _pre_norm(x, g_ref[...], mod_ref, 1).astype(BF16)
    cos = cos_ref[...]
    sin = sin_ref[...]
    hdk = RET_HEADS * RET_DK
    hdv = RET_HEADS * RET_DV
    scale = RET_DK ** -0.5
    for out_ref, base, sc in ((q_ref, 0, None), (k_ref, hdk, scale)):
        qk = _dot(xn, win_ref[:, base:base + hdk])
        for h in range(RET_HEADS):
            c0 = h * RET_DK
            x1 = qk[:, c0:c0 + LANES]
            x2 = qk[:, c0 + LANES:c0 + 2 * LANES]
            o1 = x1 * cos - x2 * sin
            o2 = x2 * cos + x1 * sin
            if sc is not None:
                o1 = o1 * sc
                o2 = o2 * sc
            out_ref[:, c0:c0 + LANES] = o1.astype(BF16)
            out_ref[:, c0 + LANES:c0 + 2 * LANES] = o2.astype(BF16)
    v_ref[...] = _dot(xn, win_ref[:, 2 * hdk:2 * hdk + hdv]).astype(BF16)
    sg_ref[...] = _silu(_dot(xn, win_ref[:, 2 * hdk + hdv:])).astype(BF16)


def _ret_proj(r, cos, sin, mod, gain, w_in, n_ctx_tiles):
    b, t, d = r.shape
    table = pl.BlockSpec((TOKEN_TILE, LANES), lambda i, j: (j, 0))
    inputs = [(r, _rows_spec(d)), (cos, table), (sin, table), (mod, _mod_spec(d, n_ctx_tiles))]
    widths = [(RET_HEADS * RET_DK, BF16), (RET_HEADS * RET_DK, BF16),
              (RET_HEADS * RET_DV, BF16), (RET_HEADS * RET_DV, BF16)]
    return _row_call(_ret_proj_kernel, "retention_projections", b, t // TOKEN_TILE, inputs,
                     [gain.reshape(1, d), w_in], widths)


def _ret_scan_kernel(l1g_ref, q_ref, k_ref, v_ref, sg_ref, gain_ref, y_ref, cross_scr, st_scr, *, n_ctx, n_all):
    c = SCAN_CHUNK
    log_g = jnp.log1p(-jnp.exp(l1g_ref[...]))
    lgf = log_g[0:1, 0:1]
    lgb = log_g[1:2, 0:1]
    ii = lax.broadcasted_iota(jnp.int32, (c, c), 0)
    jj = lax.broadcasted_iota(jnp.int32, (c, c), 1)
    diff = (ii - jj).astype(F32)
    mask = jnp.where(diff >= 0.0, jnp.exp(lgf * jnp.maximum(diff, 0.0)), jnp.exp(lgb * jnp.maximum(-diff, 0.0)))
    ic = lax.broadcasted_iota(jnp.int32, (c, 1), 0).astype(F32)
    qdec_f = jnp.exp(lgf * (ic + 1.0))
    qdec_b = jnp.exp(lgb * (c - ic))
    kdec_f = jnp.exp(lgf * (c - 1.0 - ic))
    kdec_b = jnp.exp(lgb * ic)
    cdec_f = jnp.exp(lgf * c)
    cdec_b = jnp.exp(lgb * c)

    def kv_outer(kc, dec, vc):
        return _dot((kc.astype(F32) * dec).T.astype(BF16), vc)

    sf_scr, sb_scr = st_scr.at[0], st_scr.at[1]
    sf_scr[...] = jnp.zeros_like(sf_scr)
    sb_scr[...] = jnp.zeros_like(sb_scr)
    order_b = list(range(n_ctx - 1, -1, -1)) + list(range(n_all - 1, n_ctx - 1, -1))
    seen = set()
    for nf, nb in zip(range(n_all), order_b):
        for n, s_ref, qdec, kdec, cdec in ((nf, sf_scr, qdec_f, kdec_f, cdec_f), (nb, sb_scr, qdec_b, kdec_b, cdec_b)):
            rows = slice(n * c, (n + 1) * c)
            state = s_ref[...]
            part = _dot((q_ref[rows, :].astype(F32) * qdec).astype(BF16), state.astype(BF16))
            if n in seen:
                cross_scr[rows, :] += part
            else:
                cross_scr[rows, :] = part
                seen.add(n)
            s_ref[...] = state * cdec + kv_outer(k_ref[rows, :], kdec, v_ref[rows, :])

    gain = gain_ref[...]
    for n in range(n_all):
        rows = slice(n * c, (n + 1) * c)
        scores = _dot_nt(q_ref[rows, :], k_ref[rows, :]) * mask
        o = _dot(scores.astype(BF16), v_ref[rows, :]) + cross_scr[rows, :]
        mu = jnp.mean(o, axis=-1, keepdims=True)
        dev = o - mu
        on = dev * lax.rsqrt(jnp.mean(dev * dev, axis=-1, keepdims=True) + EPS)
        y_ref[rows, :] = (on * gain * sg_ref[rows, :].astype(F32)).astype(y_ref.dtype)


def _ret_scan(q, k, v, sg, log1m_gamma, gn_gain, n_ctx):
    b, t, _ = q.shape
    l1g = jnp.broadcast_to(log1m_gamma.T[:, :, None], (RET_HEADS, 2, LANES))
    l1g = jnp.concatenate([l1g, jnp.zeros((RET_HEADS, 6, LANES), F32) - 1.0], axis=1)
    kern = functools.partial(_ret_scan_kernel, n_ctx=n_ctx // SCAN_CHUNK, n_all=t // SCAN_CHUNK)
    return pl.pallas_call(
        kern,
        grid=(b, RET_HEADS),
        in_specs=[pl.BlockSpec((None, 8, LANES), lambda i, h: (h, 0, 0)),
                  pl.BlockSpec((None, t, RET_DK), lambda i, h: (i, 0, h)),
                  pl.BlockSpec((None, t, RET_DK), lambda i, h: (i, 0, h)),
                  pl.BlockSpec((None, t, RET_DV), lambda i, h: (i, 0, h)),
                  pl.BlockSpec((None, t, RET_DV), lambda i, h: (i, 0, h)),
                  pl.BlockSpec((1, RET_DV), lambda i, h: (0, h))],
        out_specs=pl.BlockSpec((None, t, RET_DV), lambda i, h: (i, 0, h)),
        out_shape=jax.ShapeDtypeStruct((b, t, RET_HEADS * RET_DV), BF16),
        scratch_shapes=[pltpu.VMEM((t, RET_DV), F32), pltpu.VMEM((2, RET_DK, RET_DV), F32)],
        compiler_params=_cparams("arbitrary", "arbitrary"),
        name="retention_scan",
    )(l1g, q, k, v, sg, gn_gain.reshape(1, -1))


CHUNKS_PER_TILE = TOKEN_TILE // SCAN_CHUNK


def _chunk_major_spec(g, tile0=0):
    return pl.BlockSpec((g, CHUNKS_PER_TILE, FFN_BATCH_PER_STEP, S5_ROW_PITCH, SCAN_CHUNK),
                        lambda i, j: (0, j + tile0, i, 0, 0))


def _ffn_s5_in_kernel(r_ref, mod_ref, g_ref, win_ref, wout_ref, g1_ref, o_ref, u_ref):
    c = SCAN_CHUNK
    for e in range(FFN_BATCH_PER_STEP):
        mod = mod_ref.at[e]
        x = _swiglu_step(r_ref[e], mod, g_ref, win_ref, wout_ref, 0)
        o_ref[e] = x
        xn = _pre_norm(x, g1_ref[...], mod, 1)
        for ci in range(CHUNKS_PER_TILE):
            u_ref[:, ci, e, :S5_GROUP, :] = xn[ci * c:(ci + 1) * c, :].T.reshape(u_ref.shape[0], S5_GROUP, c)
            u_ref[:, ci, e, S5_GROUP:, :] = jnp.zeros((u_ref.shape[0], S5_ROW_PITCH - S5_GROUP, c), F32)


def _ffn_s5_in(r, mod, gain, w_in, w_out, mixer_gain, n_ctx_tiles):
    b, t, d = r.shape
    g = d // S5_GROUP
    consts = [gain.reshape(1, d), w_in, w_out, mixer_gain.reshape(1, d)]
    return pl.pallas_call(
        _ffn_s5_in_kernel,
        grid=(b // FFN_BATCH_PER_STEP, t // TOKEN_TILE),
        in_specs=[_rows_spec(d), _mod_spec(d, n_ctx_tiles)] + [_const_spec(c) for c in consts],
        out_specs=[_rows_spec(d), _chunk_major_spec(g)],
        out_shape=[jax.ShapeDtypeStruct((b, t, d), F32),
                   jax.ShapeDtypeStruct((g, t // SCAN_CHUNK, b, S5_ROW_PITCH, SCAN_CHUNK), F32)],
        compiler_params=_cparams("arbitrary", "arbitrary"),
        name="swiglu_s5_in",
    )(r, mod, *[_const_array(c) for c in consts])


def _s5_disc(lam_re, lam_im, log_dt):
    dt = jnp.exp(log_dt)
    a_re = lam_re * dt
    a_im = lam_im * dt
    mag = jnp.exp(a_re)
    e_re = mag * jnp.cos(a_im) - 1.0
    e_im = mag * jnp.sin(a_im)
    den = lam_re * lam_re + lam_im * lam_im
    return a_re, a_im, (e_re * lam_re + e_im * lam_im) / den, (e_im * lam_re - e_re * lam_im) / den


def _cmul(ar, ai, br, bi):
    return ar * br - ai * bi, ar * bi + ai * br


def _s5_kernel(u_ref, lamc_ref, lamr_ref, lamr2_ref, bc_ref, br_ref, ct_ref, d_ref,
               y_ref, t_scr, x_scr, w_scr, *, batch, n_ctx, n_all):
    c, m, p = SCAN_CHUNK, S5_GROUP, S5_STATE
    cm = c * m
    rows = n_all * batch
    for mi in range(m):
        x_scr[:, mi * c:(mi + 1) * c] = u_ref[pl.ds(mi, rows, stride=S5_ROW_PITCH), :].astype(BF16)

    kk = lax.broadcasted_iota(jnp.int32, (1, LANES), 1).astype(F32)

    def powers(a_re, a_im, expo):
        mag = jnp.exp(a_re * expo)
        return mag * jnp.cos(a_im * expo), mag * jnp.sin(a_im * expo)

    def lane_tiles(fn):
        parts = [fn(i) for i in range(m)]
        return (jnp.concatenate([q[0] for q in parts], axis=1), jnp.concatenate([q[1] for q in parts], axis=1))

    disc = [_s5_disc(lamc_ref[d][:, 0:1], lamc_ref[d][:, 1:2], lamc_ref[d][:, 2:3]) for d in range(2)]
    bbar = [_cmul(disc[d][2], disc[d][3], bc_ref[d, 0], bc_ref[d, 1]) for d in range(2)]
    ctr = [(ct_ref[d, 0], ct_ref[d, 1]) for d in range(2)]

    def col(pair, i):
        return pair[0][:, i:i + 1], pair[1][:, i:i + 1]

    f_up = powers(disc[0][0], disc[0][1], kk + 1.0)
    f_down = powers(disc[0][0], disc[0][1], (c - 1.0) - kk)
    b_up = powers(disc[1][0], disc[1][1], kk)
    b_down = powers(disc[1][0], disc[1][1], (c - 1.0) - kk)
    b_down1 = powers(disc[1][0], disc[1][1], c - kk)
    lag0 = jnp.where(kk == c - 1.0, 1.0, 0.0)

    q0f = lane_tiles(lambda mo: (col(ctr[0], mo)[0] * lag0, col(ctr[0], mo)[1] * lag0))
    q1f = lane_tiles(lambda mo: _cmul(*col(ctr[0], mo), *f_up))
    q0b = lane_tiles(lambda mo: _cmul(*col(ctr[1], mo), *b_down))
    qrb = lane_tiles(lambda mo: _cmul(*col(ctr[1], mo), *b_down1))

    hp = lax.Precision.HIGHEST
    btr = []
    for d in range(2):
        lam = lamr_ref[d]
        _, _, cf_re, cf_im = _s5_disc(lam[0:1, :], lam[1:2, :], lam[2:3, :])
        btr.append(_cmul(cf_re, cf_im, br_ref[d, 0], br_ref[d, 1]))

    def gen(bt, q):
        return (jnp.dot(bt[0], q[0], precision=hp, preferred_element_type=F32)
                - jnp.dot(bt[1], q[1], precision=hp, preferred_element_type=F32))

    w0 = gen(btr[0], q0f) + gen(btr[1], q0b)
    w1 = gen(btr[0], q1f)
    for mo in range(m):
        w_scr[:, 2 * mo * c:(2 * mo + 1) * c] = w0[:, mo * c:(mo + 1) * c]
        w_scr[:, (2 * mo + 1) * c:(2 * mo + 2) * c] = w1[:, mo * c:(mo + 1) * c]

    for mi in range(m):
        row = jnp.broadcast_to(w_scr[mi:mi + 1, :], (c, 2 * cm))
        win = pltpu.roll(row, 2 * cm - (c - 1), 1, stride=1, stride_axis=0)
        blk = jnp.concatenate([win[:, 2 * mo * c:(2 * mo + 1) * c] for mo in range(m)], axis=1)
        t_scr[mi * c:(mi + 1) * c, :] = blk.astype(BF16)

    for d, q in ((0, q1f), (1, qrb)):
        base = cm + 2 * p * d
        t_scr[base:base + p, :] = q[0].astype(BF16)
        t_scr[base + p:base + 2 * p, :] = (-q[1]).astype(BF16)

    wb_f = lane_tiles(lambda mi: _cmul(*col(bbar[0], mi), *f_down))
    wb_b = lane_tiles(lambda mi: _cmul(*col(bbar[1], mi), *b_up))
    wb = jnp.concatenate([wb_f[0], wb_f[1], wb_b[0], wb_b[1]], axis=0).astype(BF16)
    hloc = _dot_nt(x_scr[:, :cm], wb)

    sign = jnp.where(lax.broadcasted_iota(jnp.int32, (1, 2 * p), 1) < p, -1.0, 1.0)
    order_f = list(range(n_all))
    order_b = list(range(n_ctx - 1, -1, -1)) + list(range(n_all - 1, n_ctx - 1, -1))
    for d, order in ((0, order_f), (1, order_b)):
        lam = lamr2_ref[d]
        a_re, a_im, _, _ = _s5_disc(lam[0:1, :], lam[1:2, :], lam[2:3, :])
        mag = jnp.exp(a_re * c)
        ac_r = mag * jnp.cos(a_im * c)
        ac_i = mag * jnp.sin(a_im * c) * sign
        state = jnp.zeros((batch, 2 * p), F32)
        colx = cm + 2 * p * d
        for n in order:
            x_scr[n * batch:(n + 1) * batch, colx:colx + 2 * p] = state.astype(BF16)
            local = hloc[n * batch:(n + 1) * batch, 2 * p * d:2 * p * (d + 1)]
            state = state * ac_r + pltpu.roll(state, p, 1) * ac_i + local

    y = _dot(x_scr[...], t_scr[...])
    for mo in range(m):
        skip = d_ref[mo:mo + 1, :] * u_ref[pl.ds(mo, rows, stride=S5_ROW_PITCH), :]
        y_ref[pl.ds(mo, rows, stride=S5_ROW_PITCH), :] = y[:, mo * c:(mo + 1) * c] + skip
    for pad_row in range(m, S5_ROW_PITCH):
        y_ref[pl.ds(pad_row, rows, stride=S5_ROW_PITCH), :] = jnp.zeros((rows, c), F32)


def _s5_operands(u, lam_re, lam_im, log_dt, b_re, b_im, c_re, c_im, d_skip):
    g, n_all, batch, pitch, c = u.shape
    m = S5_GROUP
    dtb = jnp.broadcast_to(log_dt[:, :, None], lam_re.shape)
    lam3 = jnp.stack([lam_re, lam_im, dtb], axis=-1)
    lamc = jnp.pad(lam3, ((0, 0), (0, 0), (0, 0), (0, LANES - 3))).transpose(1, 0, 2, 3)
    lam3r = jnp.stack([lam_re, lam_im, dtb], axis=2)
    lamr = jnp.pad(lam3r, ((0, 0), (0, 0), (0, 5), (0, 0))).transpose(1, 0, 2, 3)
    lamr2 = jnp.concatenate([lamr, lamr], axis=-1)
    bcol = jnp.stack([b_re, b_im], axis=1)
    bc = jnp.pad(bcol, ((0, 0),) * 4 + ((0, LANES - m),)).transpose(2, 0, 1, 3, 4)
    brow = jnp.swapaxes(bcol, -1, -2).transpose(2, 0, 1, 3, 4)
    ctc = jnp.swapaxes(jnp.stack([c_re, c_im], axis=1), -1, -2)
    ctc = jnp.pad(ctc, ((0, 0),) * 4 + ((0, LANES - m),)).transpose(2, 0, 1, 3, 4)
    dvec = jnp.broadcast_to(d_skip.reshape(g, m, 1), (g, m, LANES))
    return (u.reshape(g, n_all * batch * pitch, c), lamc, lamr, lamr2, bc, brow, ctc, dvec)


def _s5_mixer(u, lam_re, lam_im, log_dt, b_re, b_im, c_re, c_im, d_skip, n_ctx):
    g, n_all, b, pitch, c = u.shape
    m, p = S5_GROUP, S5_STATE
    rows = n_all * b
    cm = c * m
    kern = functools.partial(_s5_kernel, batch=b, n_ctx=n_ctx // c, n_all=n_all)
    per_g = lambda *tail: pl.BlockSpec((None,) + tail, lambda i: (i,) + (0,) * len(tail))
    y = pl.pallas_call(
        kern,
        grid=(g,),
        in_specs=[per_g(rows * pitch, c), per_g(2, p, LANES), per_g(2, 8, p), per_g(2, 8, 2 * p),
                  per_g(2, 2, p, LANES), per_g(2, 2, m, p), per_g(2, 2, p, LANES), per_g(m, LANES)],
        out_specs=per_g(rows * pitch, c),
        out_shape=jax.ShapeDtypeStruct((g, rows * pitch, c), F32),
        scratch_shapes=[pltpu.VMEM((cm + 4 * p, cm), BF16), pltpu.VMEM((rows, cm + 4 * p), BF16),
                        pltpu.VMEM((m, 2 * cm), F32)],
        compiler_params=_cparams("arbitrary"),
        name="s5_scan",
    )(*_s5_operands(u, lam_re, lam_im, log_dt, b_re, b_im, c_re, c_im, d_skip))
    return y.reshape(u.shape)


def kernel(x, c, ctx, c_ctx, ada_w, ada_b, norm_g, ffn_w_in, ffn_w_out, mla_w_in, mla_q_norm, mla_kv_norm,
           mla_w_uq, mla_w_ukv, mla_q_gain, mla_k_gain, mla_w_o, ret_w_in, ret_log1m_gamma, ret_gn_gain, ret_w_o,
           s5_lam_re, s5_lam_im, s5_log_dt, s5_b_re, s5_b_im, s5_c_re, s5_c_im, s5_d, s5_glu_w, s5_glu_b):
    b, s, d = x.shape
    n_ctx = ctx.shape[1]
    depth = ada_w.shape[0]
    assert n_ctx % TOKEN_TILE == 0 and s % TOKEN_TILE == 0 and s % GRID_W == 0 and b % FFN_BATCH_PER_STEP == 0
    n_ctx_tiles = n_ctx // TOKEN_TILE

    pad = (-(b + 1)) % 8
    cvec = jnp.concatenate([c, c_ctx[None, :], jnp.zeros((pad, d), F32)], axis=0)
    mod_all = _modulation(cvec, ada_w, ada_b)
    mod_lat = mod_all[:, :b].reshape(depth, b, 1, N_MOD, d)
    mod_ctx = jnp.broadcast_to(mod_all[:, b].reshape(depth, 1, 1, N_MOD, d), (depth, b, 1, N_MOD, d))
    mods = jnp.concatenate([mod_ctx, mod_lat], axis=2)

    mla_tabs = _mla_tables(n_ctx, s)
    ret_tabs = _ret_tables(n_ctx, s)

    ffn_w_in_bf = ffn_w_in.astype(BF16)
    ffn_w_out_bf = ffn_w_out.astype(BF16)
    r = None
    for i in range(depth):
        kind, j = i % N_MIXERS, i // N_MIXERS
        mod = mods[i]
        ffn0 = (norm_g[i, 0], (ffn_w_in_bf, (i, 0)), (ffn_w_out_bf, (i, 0)))
        ffn1 = (norm_g[i, 2], (ffn_w_in_bf, (i, 1)), (ffn_w_out_bf, (i, 1)))
        if i == 0:
            assert kind != 2
            r = _ffn_first(ctx, x, mod, *ffn0, n_ctx_tiles)
        elif kind == 2:
            r, u = _ffn_s5_in(r, mod, *ffn0, norm_g[i, 1], n_ctx_tiles)
        else:
            r = _ffn(r, mod, *ffn0, n_ctx_tiles)
        tile0 = n_ctx_tiles if i == depth - 1 else 0
        if kind == 0:
            weights = _mla_weights(mla_w_in[j], mla_q_norm[j], mla_kv_norm[j], mla_w_uq[j], mla_w_ukv[j],
                                   mla_q_gain[j], mla_k_gain[j])
            q, k, v = _mla_proj(r, *mla_tabs, mod, norm_g[i, 1], weights, n_ctx_tiles)
            y = _attention(q, k, v, _attn_score_bound(mla_q_gain[j], mla_k_gain[j]), n_ctx)
            mix = (y, _rows_spec(y.shape[-1], tile0), [mla_w_o[j].astype(BF16)], _ffn_proj_kernel, "mla_out_swiglu")
        elif kind == 1:
            q, k, v, sg = _ret_proj(r, *ret_tabs, mod, norm_g[i, 1], ret_w_in[j].astype(BF16), n_ctx_tiles)
            y = _ret_scan(q, k, v, sg, ret_log1m_gamma[j], ret_gn_gain[j], n_ctx)
            mix = (y, _rows_spec(y.shape[-1], tile0), [ret_w_o[j].astype(BF16)], _ffn_proj_kernel, "ret_out_swiglu")
        else:
            y = _s5_mixer(u, s5_lam_re[j], s5_lam_im[j], s5_log_dt[j], s5_b_re[j], s5_b_im[j],
                          s5_c_re[j], s5_c_im[j], s5_d[j], n_ctx)
            mix = (y, _chunk_major_spec(y.shape[0], tile0), [s5_glu_w[j].astype(BF16), s5_glu_b[j].reshape(1, -1)],
                   _ffn_glu_kernel, "s5_out_swiglu")
        r = _ffn_after_mixer(r, *mix, mod, *ffn1, n_ctx_tiles, tile0)
    return r
```

```python
import functools
import math

import jax
import jax.numpy as jnp
import numpy as np
from jax import lax
from jax.experimental import pallas as pl
from jax.experimental.pallas import tpu as pltpu

F32 = jnp.float32
BF16 = jnp.bfloat16

EPS = 1e-6
ROPE_BASE = 10000.0
GRID_W = 64
N_MIXERS = 3
N_MOD = 9

MLA_HEADS = 8
MLA_NOPE = 128
MLA_ROPE = 64
MLA_V = 128
MLA_Q_RANK = 384
MLA_KV_RANK = 256
MLA_HEAD_PAD = 256

RET_HEADS = 4
RET_DK = 256
RET_DV = 512
SCAN_CHUNK = 128

S5_GROUP = 16
S5_ROW_PITCH = 24
S5_STATE = 64

LANES = 128
MXU_WIDTH = 256
TOKEN_TILE = 256
ATTN_HEADS_PER_STEP = 4
FFN_BATCH_PER_STEP = 4
VMEM_LIMIT_BYTES = 56 * 1024 * 1024


def _cparams(*sem):
    return pltpu.CompilerParams(dimension_semantics=sem, vmem_limit_bytes=VMEM_LIMIT_BYTES)


def _resident(shape, single_buffer=False):
    zeros = (0,) * len(shape)
    if single_buffer:
        return pl.BlockSpec(shape, lambda *_: zeros, pipeline_mode=pl.Buffered(1))
    return pl.BlockSpec(shape, lambda *_: zeros)


def _silu(x):
    return x * jax.nn.sigmoid(x)


def _rms(x, n):
    return x * lax.rsqrt(jnp.sum(x * x, axis=-1, keepdims=True) * (1.0 / n) + EPS)


def _pre_norm(x, gain, mod_ref, k):
    y = _rms(x, x.shape[-1]) * gain
    return y * (1.0 + mod_ref[3 * k + 1:3 * k + 2, :]) + mod_ref[3 * k:3 * k + 1, :]


def _dot(a, b):
    return jnp.dot(a, b, preferred_element_type=F32)


def _dot_nt(a, b):
    return lax.dot_general(a, b, (((1,), (1,)), ((), ())), preferred_element_type=F32)


def _mod_kernel(cv_ref, w_ref, b_ref, o_ref):
    s = _silu(cv_ref[...]).astype(BF16)
    o_ref[...] = _dot(s, w_ref[...].astype(BF16)) + b_ref[...]


def _modulation(cvec, ada_w, ada_b):
    depth, d, n = ada_w.shape
    rows = cvec.shape[0]
    tn = n // 4
    return pl.pallas_call(
        _mod_kernel,
        grid=(depth, n // tn),
        in_specs=[
            _resident((rows, d)),
            pl.BlockSpec((None, d, tn), lambda i, j: (i, 0, j)),
            pl.BlockSpec((None, 1, tn), lambda i, j: (i, 0, j)),
        ],
        out_specs=pl.BlockSpec((None, rows, tn), lambda i, j: (i, 0, j)),
        out_shape=jax.ShapeDtypeStruct((depth, rows, n), F32),
        compiler_params=_cparams("arbitrary", "arbitrary"),
        name="adaln_modulation",
    )(cvec, ada_w, ada_b.reshape(depth, 1, n))


def _rows_spec(width, tile0=0):
    return pl.BlockSpec((FFN_BATCH_PER_STEP, TOKEN_TILE, width), lambda i, j: (i, j + tile0, 0))


def _mod_spec(d, n_ctx_tiles, tile0=0):
    return pl.BlockSpec((FFN_BATCH_PER_STEP, None, N_MOD, d),
                        lambda i, j: (i, (j + tile0 >= n_ctx_tiles).astype(jnp.int32), 0, 0))


def _const_spec(c):
    if not isinstance(c, tuple):
        return _resident(c.shape, single_buffer=True)
    stacked, lead = c
    tail = stacked.shape[len(lead):]
    return pl.BlockSpec((None,) * len(lead) + tail, lambda *_: lead + (0,) * len(tail), pipeline_mode=pl.Buffered(1))


def _const_array(c):
    return c[0] if isinstance(c, tuple) else c


def _row_call(kernel, name, b, n_tiles, inputs, consts, outs):
    in_specs = [spec for _, spec in inputs] + [_const_spec(c) for c in consts]
    return pl.pallas_call(
        kernel,
        grid=(b // FFN_BATCH_PER_STEP, n_tiles),
        in_specs=in_specs,
        out_specs=[_rows_spec(w) for w, _ in outs],
        out_shape=[jax.ShapeDtypeStruct((b, n_tiles * TOKEN_TILE, w), dt) for w, dt in outs],
        compiler_params=_cparams("arbitrary", "arbitrary"),
        name=name,
    )(*[a for a, _ in inputs], *[_const_array(c) for c in consts])


def _gelu_tanh(x):
    cdf = 0.5 * (1.0 + jnp.tanh(math.sqrt(2.0 / math.pi) * (x + 0.044715 * (x * x * x))))
    return x * cdf


def _swiglu_step(x, mod_ref, g_ref, win_ref, wout_ref, k):
    xn = _pre_norm(x, g_ref[...], mod_ref, k).astype(BF16)
    f = wout_ref.shape[0]
    half = (f // (2 * MXU_WIDTH) + 1) * MXU_WIDTH
    y = None
    for c0, c1 in ((0, half), (half, f)):
        a = _dot(xn, win_ref[:, c0:c1])
        b = _dot(xn, win_ref[:, f + c0:f + c1])
        part = _dot((_silu(a) * b).astype(BF16), wout_ref[c0:c1, :])
        y = part if y is None else y + part
    return x + (0.5 * mod_ref[3 * k + 2:3 * k + 3, :]) * y


def _ffn_first_kernel(ctx_ref, x_ref, mod_ref, g_ref, win_ref, wout_ref, o_ref, *, n_ctx_tiles):
    for e in range(FFN_BATCH_PER_STEP):
        x = jnp.where(pl.program_id(1) < n_ctx_tiles, ctx_ref[e], x_ref[e])
        o_ref[e] = _swiglu_step(x, mod_ref.at[e], g_ref, win_ref, wout_ref, 0)


def _ffn_kernel(r_ref, mod_ref, g_ref, win_ref, wout_ref, o_ref):
    for e in range(FFN_BATCH_PER_STEP):
        o_ref[e] = _swiglu_step(r_ref[e], mod_ref.at[e], g_ref, win_ref, wout_ref, 0)


def _ffn_proj_kernel(r_ref, y_ref, mod_ref, pw_ref, g_ref, win_ref, wout_ref, o_ref):
    for e in range(FFN_BATCH_PER_STEP):
        mod = mod_ref.at[e]
        x = r_ref[e] + mod[5:6, :] * _dot(y_ref[e], pw_ref[...])
        o_ref[e] = _swiglu_step(x, mod, g_ref, win_ref, wout_ref, 2)


def _ffn_glu_kernel(r_ref, y_ref, mod_ref, pw_ref, pb_ref, g_ref, win_ref, wout_ref, o_ref):
    d = r_ref.shape[-1]
    c = SCAN_CHUNK
    for e in range(FFN_BATCH_PER_STEP):
        mod = mod_ref.at[e]
        y = jnp.concatenate([y_ref[:, ci, e, :S5_GROUP, :].reshape(d, c).T for ci in range(TOKEN_TILE // c)], axis=0)
        ag = _dot(_gelu_tanh(y).astype(BF16), pw_ref[...]) + pb_ref[...]
        x = r_ref[e] + mod[5:6, :] * (ag[:, :d] * jax.nn.sigmoid(ag[:, d:]))
        o_ref[e] = _swiglu_step(x, mod, g_ref, win_ref, wout_ref, 2)


def _ffn_first(ctx, x, mod, gain, w_in, w_out, n_ctx_tiles):
    b, s, d = x.shape
    last_ctx = n_ctx_tiles - 1
    eb = FFN_BATCH_PER_STEP
    inputs = [(ctx, pl.BlockSpec((eb, TOKEN_TILE, d), lambda i, j: (i, jnp.minimum(j, last_ctx), 0))),
              (x, pl.BlockSpec((eb, TOKEN_TILE, d), lambda i, j: (i, jnp.maximum(j - n_ctx_tiles, 0), 0))),
              (mod, _mod_spec(d, n_ctx_tiles))]
    (out,) = _row_call(functools.partial(_ffn_first_kernel, n_ctx_tiles=n_ctx_tiles), "swiglu_first",
                       b, n_ctx_tiles + s // TOKEN_TILE, inputs, [gain.reshape(1, d), w_in, w_out], [(d, F32)])
    return out


def _ffn(r, mod, gain, w_in, w_out, n_ctx_tiles):
    b, t, d = r.shape
    inputs = [(r, _rows_spec(d)), (mod, _mod_spec(d, n_ctx_tiles))]
    (out,) = _row_call(_ffn_kernel, "swiglu_half_step", b, t // TOKEN_TILE, inputs,
                       [gain.reshape(1, d), w_in, w_out], [(d, F32)])
    return out


def _ffn_after_mixer(r, y, y_spec, mix_consts, kern, name, mod, gain, w_in, w_out, n_ctx_tiles, tile0):
    b, t, d = r.shape
    inputs = [(r, _rows_spec(d, tile0)), (y, y_spec), (mod, _mod_spec(d, n_ctx_tiles, tile0))]
    (out,) = _row_call(kern, name, b, t // TOKEN_TILE - tile0, inputs,
                       mix_consts + [gain.reshape(1, d), w_in, w_out], [(d, F32)])
    return out


def _rope_tile_perm():
    src = np.full((LANES,), -1, np.int64)
    src[0:16] = np.arange(0, 16)
    src[16:32] = np.arange(32, 48)
    src[64:80] = np.arange(16, 32)
    src[80:96] = np.arange(48, 64)
    return src


def _pad_rope_cols(w):
    src = _rope_tile_perm()
    cols = jnp.take(w, jnp.asarray(np.maximum(src, 0)), axis=-1)
    return jnp.where(jnp.asarray(src >= 0), cols, 0.0)


def _mla_tables(n_ctx, n_lat):
    half = MLA_ROPE // 2
    inv = ROPE_BASE ** (-jnp.arange(0, half, 2, dtype=F32) / half)
    pos = jnp.arange(n_lat)
    ang_r = (pos // GRID_W).astype(F32)[:, None] * inv[None, :]
    ang_c = (pos % GRID_W).astype(F32)[:, None] * inv[None, :]
    z = jnp.zeros((n_lat, 32), F32)
    cos = jnp.concatenate([jnp.cos(ang_r), jnp.cos(ang_c), z, jnp.cos(ang_r), jnp.cos(ang_c), z], axis=-1)
    sin = jnp.concatenate([-jnp.sin(ang_r), -jnp.sin(ang_c), z, jnp.sin(ang_r), jnp.sin(ang_c), z], axis=-1)
    cos = jnp.concatenate([jnp.ones((n_ctx, LANES), F32), cos], axis=0)
    sin = jnp.concatenate([jnp.zeros((n_ctx, LANES), F32), sin], axis=0)
    return cos, sin


def _mla_proj_kernel(r_ref, cos_ref, sin_ref, mod_ref, *rest):
    q_ref, k_ref, v_ref = rest[-3:]
    for e in range(FFN_BATCH_PER_STEP):
        _mla_project(r_ref[e], cos_ref, sin_ref, mod_ref.at[e], *rest[:-3], q_ref.at[e], k_ref.at[e], v_ref.at[e])


def _mla_project(x, cos_ref, sin_ref, mod_ref, g_ref, win_ref, qn_ref, kvn_ref, wuq_ref, wukv_ref,
                 qg_ref, kg_ref, q_ref, k_ref, v_ref):
    xn = _pre_norm(x, g_ref[...], mod_ref, 1).astype(BF16)
    p = _dot(xn, win_ref[...])
    cos = cos_ref[...]
    sin = sin_ref[...]

    def rotate(x):
        return x * cos + pltpu.roll(x, LANES // 2, 1) * sin

    cq = (_rms(p[:, :MLA_Q_RANK], MLA_Q_RANK) * qn_ref[...]).astype(BF16)
    ckv = (_rms(p[:, MLA_Q_RANK:MLA_Q_RANK + MLA_KV_RANK], MLA_KV_RANK) * kvn_ref[...]).astype(BF16)
    kr = rotate(_rms(p[:, MLA_Q_RANK + MLA_KV_RANK:], MLA_ROPE) * kg_ref[:, LANES:]).astype(BF16)
    q = _dot(cq, wuq_ref[...])
    kv = _dot(ckv, wukv_ref[...])
    scale = (MLA_NOPE + MLA_ROPE) ** -0.5 * math.log2(math.e)
    ii = lax.broadcasted_iota(jnp.int32, (2 * LANES, 2 * LANES), 0)
    jj = lax.broadcasted_iota(jnp.int32, (2 * LANES, 2 * LANES), 1)
    same = (ii // LANES) == (jj // LANES)
    mean_q = jnp.where(same, jnp.where(ii < LANES, 1.0 / MLA_NOPE, 1.0 / MLA_ROPE), 0.0).astype(BF16)
    mean_k = jnp.where(same, 1.0 / MLA_NOPE, 0.0).astype(BF16)
    q_gain = qg_ref[...] * scale
    for h in range(MLA_HEADS):
        a0 = h * MLA_HEAD_PAD
        qh = q[:, a0:a0 + 2 * LANES]
        qn = qh * lax.rsqrt(_dot((qh * qh).astype(BF16), mean_q) + EPS) * q_gain
        q_ref[:, a0:a0 + LANES] = qn[:, :LANES].astype(BF16)
        q_ref[:, a0 + LANES:a0 + 2 * LANES] = rotate(qn[:, LANES:]).astype(BF16)
        k_ref[:, a0 + LANES:a0 + 2 * LANES] = kr
    for hh in range(MLA_HEADS // 2):
        kh = kv[:, 2 * hh * LANES:(2 * hh + 2) * LANES]
        kn = kh * lax.rsqrt(_dot((kh * kh).astype(BF16), mean_k) + EPS)
        for e in range(2):
            a0 = (2 * hh + e) * MLA_HEAD_PAD
            k_ref[:, a0:a0 + LANES] = (kn[:, e * LANES:(e + 1) * LANES] * kg_ref[:, :LANES]).astype(BF16)
    v_ref[...] = kv[:, MLA_HEADS * LANES:].astype(BF16)


def _mla_weights(w_in, q_norm, kv_norm, w_uq, w_ukv, q_gain, k_gain):
    lat = MLA_Q_RANK + MLA_KV_RANK
    w_in_p = jnp.concatenate([w_in[:, :lat], _pad_rope_cols(w_in[:, lat:])], axis=-1).astype(BF16)
    uq = w_uq.reshape(MLA_Q_RANK, MLA_HEADS, MLA_NOPE + MLA_ROPE)
    uq = jnp.concatenate([uq[..., :MLA_NOPE], _pad_rope_cols(uq[..., MLA_NOPE:])], axis=-1)
    uq = uq.reshape(MLA_Q_RANK, MLA_HEADS * MLA_HEAD_PAD).astype(BF16)
    ukv = w_ukv.reshape(MLA_KV_RANK, MLA_HEADS, MLA_NOPE + MLA_V)
    ukv = jnp.concatenate([ukv[..., :MLA_NOPE].reshape(MLA_KV_RANK, -1),
                           ukv[..., MLA_NOPE:].reshape(MLA_KV_RANK, -1)], axis=-1).astype(BF16)
    qg = jnp.concatenate([q_gain[:MLA_NOPE], _pad_rope_cols(q_gain[MLA_NOPE:])]).reshape(1, -1)
    kg = jnp.concatenate([k_gain[:MLA_NOPE], _pad_rope_cols(k_gain[MLA_NOPE:])]).reshape(1, -1)
    return [w_in_p, q_norm.reshape(1, -1), kv_norm.reshape(1, -1), uq, ukv, qg, kg]


def _mla_proj(r, cos, sin, mod, gain, weights, n_ctx_tiles):
    b, t, d = r.shape
    hw = MLA_HEADS * MLA_HEAD_PAD
    table = pl.BlockSpec((TOKEN_TILE, LANES), lambda i, j: (j, 0))
    inputs = [(r, _rows_spec(d)), (cos, table), (sin, table), (mod, _mod_spec(d, n_ctx_tiles))]
    return _row_call(_mla_proj_kernel, "mla_projections", b, t // TOKEN_TILE, inputs,
                     [gain.reshape(1, d)] + weights, [(hw, BF16), (hw, BF16), (MLA_HEADS * MLA_V, BF16)])


ATTN_MIN_ROW_SUM = 2.0 ** -100


def _attn_kernel(bound_ref, q_ref, k_ref, v_ref, o_ref, *, n_ctx_tiles, n_ctx, tile0):
    def attend(n_keys, use_bound):
        low = None
        for h in range(ATTN_HEADS_PER_STEP):
            qk = slice(h * MLA_HEAD_PAD, (h + 1) * MLA_HEAD_PAD)
            vo = slice(h * MLA_V, (h + 1) * MLA_V)
            s = _dot_nt(q_ref[:, qk], k_ref[:n_keys, qk])
            shift = bound_ref[0:1, 0:1] if use_bound else jnp.max(s, axis=-1, keepdims=True)
            p = jnp.exp2(s - shift)
            l = jnp.sum(p, axis=-1, keepdims=True)
            o = _dot(p.astype(BF16), v_ref[:n_keys, vo])
            o_ref[:, vo] = (o / l).astype(o_ref.dtype)
            low = l if low is None else jnp.minimum(low, l)
        return low

    def attend_keys(n_keys):
        low = attend(n_keys, True)
        safe = jnp.min(low) >= ATTN_MIN_ROW_SUM

        @pl.when(jnp.logical_not(safe))
        def _():
            attend(n_keys, False)

    is_ctx = pl.program_id(2) + tile0 < n_ctx_tiles
    pl.when(is_ctx)(lambda: attend_keys(n_ctx))
    pl.when(jnp.logical_not(is_ctx))(lambda: attend_keys(k_ref.shape[0]))


def _attn_score_bound(q_gain, k_gain):
    scale = (MLA_NOPE + MLA_ROPE) ** -0.5 * math.log2(math.e)
    gq, gk = jnp.abs(q_gain), jnp.abs(k_gain)
    dot_max = (MLA_NOPE * jnp.max(gq[:MLA_NOPE]) * jnp.max(gk[:MLA_NOPE])
               + MLA_ROPE * jnp.max(gq[MLA_NOPE:]) * jnp.max(gk[MLA_NOPE:]))
    return jnp.full((8, LANES), 1.01 * scale, F32) * dot_max


def _attention(q, k, v, bound, n_ctx, tile0=0):
    b, t, _ = q.shape
    tq = TOKEN_TILE
    hp = ATTN_HEADS_PER_STEP
    kern = functools.partial(_attn_kernel, n_ctx_tiles=n_ctx // tq, n_ctx=n_ctx, tile0=tile0)
    return pl.pallas_call(
        kern,
        grid=(b, MLA_HEADS // hp, t // tq - tile0),
        in_specs=[_resident(bound.shape),
                  pl.BlockSpec((None, tq, hp * MLA_HEAD_PAD), lambda i, h, j: (i, j + tile0, h)),
                  pl.BlockSpec((None, t, hp * MLA_HEAD_PAD), lambda i, h, j: (i, 0, h)),
                  pl.BlockSpec((None, t, hp * MLA_V), lambda i, h, j: (i, 0, h))],
        out_specs=pl.BlockSpec((None, tq, hp * MLA_V), lambda i, h, j: (i, j + tile0, h)),
        out_shape=jax.ShapeDtypeStruct((b, t, MLA_HEADS * MLA_V), BF16),
        compiler_params=_cparams("arbitrary", "arbitrary", "arbitrary"),
        name="mla_attention",
    )(bound, q, k, v)


def _ret_tables(n_ctx, n_lat):
    inv = ROPE_BASE ** (-jnp.arange(0, RET_DK, 2, dtype=F32) / RET_DK)
    ang = jnp.arange(n_lat, dtype=F32)[:, None] * inv[None, :]
    cos = jnp.concatenate([jnp.ones((n_ctx, LANES), F32), jnp.cos(ang)], axis=0)
    sin = jnp.concatenate([jnp.zeros((n_ctx, LANES), F32), jnp.sin(ang)], axis=0)
    return cos, sin


def _ret_proj_kernel(r_ref, cos_ref, sin_ref, mod_ref, g_ref, win_ref, q_ref, k_ref, v_ref, sg_ref):
    for e in range(FFN_BATCH_PER_STEP):
        _ret_project(r_ref[e], cos_ref, sin_ref, mod_ref.at[e], g_ref, win_ref,
                     q_ref.at[e], k_ref.at[e], v_ref.at[e], sg_ref.at[e])


def _ret_project(x, cos_ref, sin_ref, mod_ref, g_ref, win_ref, q_ref, k_ref, v_ref, sg_ref):
    xn = _pre_norm(x, g_ref[...], mod_ref, 1).astype(BF16)
    cos = cos_ref[...]
    sin = sin_ref[...]
    hdk = RET_HEADS * RET_DK
    hdv = RET_HEADS * RET_DV
    scale = RET_DK ** -0.5
    for out_ref, base, sc in ((q_ref, 0, None), (k_ref, hdk, scale)):
        qk = _dot(xn, win_ref[:, base:base + hdk])
        for h in range(RET_HEADS):
            c0 = h * RET_DK
            x1 = qk[:, c0:c0 + LANES]
            x2 = qk[:, c0 + LANES:c0 + 2 * LANES]
            o1 = x1 * cos - x2 * sin
            o2 = x2 * cos + x1 * sin
            if sc is not None:
                o1 = o1 * sc
                o2 = o2 * sc
            out_ref[:, c0:c0 + LANES] = o1.astype(BF16)
            out_ref[:, c0 + LANES:c0 + 2 * LANES] = o2.astype(BF16)
    v_ref[...] = _dot(xn, win_ref[:, 2 * hdk:2 * hdk + hdv]).astype(BF16)
    sg_ref[...] = _silu(_dot(xn, win_ref[:, 2 * hdk + hdv:])).astype(BF16)


def _ret_proj(r, cos, sin, mod, gain, w_in, n_ctx_tiles):
    b, t, d = r.shape
    table = pl.BlockSpec((TOKEN_TILE, LANES), lambda i, j: (j, 0))
    inputs = [(r, _rows_spec(d)), (cos, table), (sin, table), (mod, _mod_spec(d, n_ctx_tiles))]
    widths = [(RET_HEADS * RET_DK, BF16), (RET_HEADS * RET_DK, BF16),
              (RET_HEADS * RET_DV, BF16), (RET_HEADS * RET_DV, BF16)]
    return _row_call(_ret_proj_kernel, "retention_projections", b, t // TOKEN_TILE, inputs,
                     [gain.reshape(1, d), w_in], widths)


def _ret_scan_kernel(l1g_ref, q_ref, k_ref, v_ref, sg_ref, gain_ref, y_ref, cross_scr, st_scr, *, n_ctx, n_all):
    c = SCAN_CHUNK
    log_g = jnp.log1p(-jnp.exp(l1g_ref[...]))
    lgf = log_g[0:1, 0:1]
    lgb = log_g[1:2, 0:1]
    ii = lax.broadcasted_iota(jnp.int32, (c, c), 0)
    jj = lax.broadcasted_iota(jnp.int32, (c, c), 1)
    diff = (ii - jj).astype(F32)
    mask = jnp.where(diff >= 0.0, jnp.exp(lgf * jnp.maximum(diff, 0.0)), jnp.exp(lgb * jnp.maximum(-diff, 0.0)))
    ic = lax.broadcasted_iota(jnp.int32, (c, 1), 0).astype(F32)
    qdec_f = jnp.exp(lgf * (ic + 1.0))
    qdec_b = jnp.exp(lgb * (c - ic))
    kdec_f = jnp.exp(lgf * (c - 1.0 - ic))
    kdec_b = jnp.exp(lgb * ic)
    cdec_f = jnp.exp(lgf * c)
    cdec_b = jnp.exp(lgb * c)

    def kv_outer(kc, dec, vc):
        return _dot((kc.astype(F32) * dec).T.astype(BF16), vc)

    sf_scr, sb_scr = st_scr.at[0], st_scr.at[1]
    sf_scr[...] = jnp.zeros_like(sf_scr)
    sb_scr[...] = jnp.zeros_like(sb_scr)
    order_b = list(range(n_ctx - 1, -1, -1)) + list(range(n_all - 1, n_ctx - 1, -1))
    seen = set()
    for nf, nb in zip(range(n_all), order_b):
        for n, s_ref, qdec, kdec, cdec in ((nf, sf_scr, qdec_f, kdec_f, cdec_f), (nb, sb_scr, qdec_b, kdec_b, cdec_b)):
            rows = slice(n * c, (n + 1) * c)
            state = s_ref[...]
            part = _dot((q_ref[rows, :].astype(F32) * qdec).astype(BF16), state.astype(BF16))
            if n in seen:
                cross_scr[rows, :] += part
            else:
                cross_scr[rows, :] = part
                seen.add(n)
            s_ref[...] = state * cdec + kv_outer(k_ref[rows, :], kdec, v_ref[rows, :])

    gain = gain_ref[...]
    for n in range(n_all):
        rows = slice(n * c, (n + 1) * c)
        scores = _dot_nt(q_ref[rows, :], k_ref[rows, :]) * mask
        o = _dot(scores.astype(BF16), v_ref[rows, :]) + cross_scr[rows, :]
        mu = jnp.mean(o, axis=-1, keepdims=True)
        dev = o - mu
        on = dev * lax.rsqrt(jnp.mean(dev * dev, axis=-1, keepdims=True) + EPS)
        y_ref[rows, :] = (on * gain * sg_ref[rows, :].astype(F32)).astype(y_ref.dtype)


def _ret_scan(q, k, v, sg, log1m_gamma, gn_gain, n_ctx):
    b, t, _ = q.shape
    l1g = jnp.broadcast_to(log1m_gamma.T[:, :, None], (RET_HEADS, 2, LANES))
    l1g = jnp.concatenate([l1g, jnp.zeros((RET_HEADS, 6, LANES), F32) - 1.0], axis=1)
    kern = functools.partial(_ret_scan_kernel, n_ctx=n_ctx // SCAN_CHUNK, n_all=t // SCAN_CHUNK)
    return pl.pallas_call(
        kern,
        grid=(b, RET_HEADS),
        in_specs=[pl.BlockSpec((None, 8, LANES), lambda i, h: (h, 0, 0)),
                  pl.BlockSpec((None, t, RET_DK), lambda i, h: (i, 0, h)),
                  pl.BlockSpec((None, t, RET_DK), lambda i, h: (i, 0, h)),
                  pl.BlockSpec((None, t, RET_DV), lambda i, h: (i, 0, h)),
                  pl.BlockSpec((None, t, RET_DV), lambda i, h: (i, 0, h)),
                  pl.BlockSpec((1, RET_DV), lambda i, h: (0, h))],
        out_specs=pl.BlockSpec((None, t, RET_DV), lambda i, h: (i, 0, h)),
        out_shape=jax.ShapeDtypeStruct((b, t, RET_HEADS * RET_DV), BF16),
        scratch_shapes=[pltpu.VMEM((t, RET_DV), F32), pltpu.VMEM((2, RET_DK, RET_DV), F32)],
        compiler_params=_cparams("arbitrary", "arbitrary"),
        name="retention_scan",
    )(l1g, q, k, v, sg, gn_gain.reshape(1, -1))


CHUNKS_PER_TILE = TOKEN_TILE // SCAN_CHUNK


def _chunk_major_spec(g, tile0=0):
    return pl.BlockSpec((g, CHUNKS_PER_TILE, FFN_BATCH_PER_STEP, S5_ROW_PITCH, SCAN_CHUNK),
                        lambda i, j: (0, j + tile0, i, 0, 0))


def _ffn_s5_in_kernel(r_ref, mod_ref, g_ref, win_ref, wout_ref, g1_ref, o_ref, u_ref):
    c = SCAN_CHUNK
    for e in range(FFN_BATCH_PER_STEP):
        mod = mod_ref.at[e]
        x = _swiglu_step(r_ref[e], mod, g_ref, win_ref, wout_ref, 0)
        o_ref[e] = x
        xn = _pre_norm(x, g1_ref[...], mod, 1)
        for ci in range(CHUNKS_PER_TILE):
            u_ref[:, ci, e, :S5_GROUP, :] = xn[ci * c:(ci + 1) * c, :].T.reshape(u_ref.shape[0], S5_GROUP, c)
            u_ref[:, ci, e, S5_GROUP:, :] = jnp.zeros((u_ref.shape[0], S5_ROW_PITCH - S5_GROUP, c), F32)


def _ffn_s5_in(r, mod, gain, w_in, w_out, mixer_gain, n_ctx_tiles):
    b, t, d = r.shape
    g = d // S5_GROUP
    consts = [gain.reshape(1, d), w_in, w_out, mixer_gain.reshape(1, d)]
    return pl.pallas_call(
        _ffn_s5_in_kernel,
        grid=(b // FFN_BATCH_PER_STEP, t // TOKEN_TILE),
        in_specs=[_rows_spec(d), _mod_spec(d, n_ctx_tiles)] + [_const_spec(c) for c in consts],
        out_specs=[_rows_spec(d), _chunk_major_spec(g)],
        out_shape=[jax.ShapeDtypeStruct((b, t, d), F32),
                   jax.ShapeDtypeStruct((g, t // SCAN_CHUNK, b, S5_ROW_PITCH, SCAN_CHUNK), F32)],
        compiler_params=_cparams("arbitrary", "arbitrary"),
        name="swiglu_s5_in",
    )(r, mod, *[_const_array(c) for c in consts])


def _s5_disc(lam_re, lam_im, log_dt):
    dt = jnp.exp(log_dt)
    a_re = lam_re * dt
    a_im = lam_im * dt
    mag = jnp.exp(a_re)
    e_re = mag * jnp.cos(a_im) - 1.0
    e_im = mag * jnp.sin(a_im)
    den = lam_re * lam_re + lam_im * lam_im
    return a_re, a_im, (e_re * lam_re + e_im * lam_im) / den, (e_im * lam_re - e_re * lam_im) / den


def _cmul(ar, ai, br, bi):
    return ar * br - ai * bi, ar * bi + ai * br


def _s5_kernel(u_ref, lamc_ref, lamr_ref, lamr2_ref, bc_ref, br_ref, ct_ref, d_ref,
               y_ref, t_scr, x_scr, w_scr, *, batch, n_ctx, n_all):
    c, m, p = SCAN_CHUNK, S5_GROUP, S5_STATE
    cm = c * m
    rows = n_all * batch
    for mi in range(m):
        x_scr[:, mi * c:(mi + 1) * c] = u_ref[pl.ds(mi, rows, stride=S5_ROW_PITCH), :].astype(BF16)

    kk = lax.broadcasted_iota(jnp.int32, (1, LANES), 1).astype(F32)

    def powers(a_re, a_im, expo):
        mag = jnp.exp(a_re * expo)
        return mag * jnp.cos(a_im * expo), mag * jnp.sin(a_im * expo)

    def lane_tiles(fn):
        parts = [fn(i) for i in range(m)]
        return (jnp.concatenate([q[0] for q in parts], axis=1), jnp.concatenate([q[1] for q in parts], axis=1))

    disc = [_s5_disc(lamc_ref[d][:, 0:1], lamc_ref[d][:, 1:2], lamc_ref[d][:, 2:3]) for d in range(2)]
    bbar = [_cmul(disc[d][2], disc[d][3], bc_ref[d, 0], bc_ref[d, 1]) for d in range(2)]
    ctr = [(ct_ref[d, 0], ct_ref[d, 1]) for d in range(2)]

    def col(pair, i):
        return pair[0][:, i:i + 1], pair[1][:, i:i + 1]

    f_up = powers(disc[0][0], disc[0][1], kk + 1.0)
    f_down = powers(disc[0][0], disc[0][1], (c - 1.0) - kk)
    b_up = powers(disc[1][0], disc[1][1], kk)
    b_down = powers(disc[1][0], disc[1][1], (c - 1.0) - kk)
    b_down1 = powers(disc[1][0], disc[1][1], c - kk)
    lag0 = jnp.where(kk == c - 1.0, 1.0, 0.0)

    q0f = lane_tiles(lambda mo: (col(ctr[0], mo)[0] * lag0, col(ctr[0], mo)[1] * lag0))
    q1f = lane_tiles(lambda mo: _cmul(*col(ctr[0], mo), *f_up))
    q0b = lane_tiles(lambda mo: _cmul(*col(ctr[1], mo), *b_down))
    qrb = lane_tiles(lambda mo: _cmul(*col(ctr[1], mo), *b_down1))

    hp = lax.Precision.HIGHEST
    btr = []
    for d in range(2):
        lam = lamr_ref[d]
        _, _, cf_re, cf_im = _s5_disc(lam[0:1, :], lam[1:2, :], lam[2:3, :])
        btr.append(_cmul(cf_re, cf_im, br_ref[d, 0], br_ref[d, 1]))

    def gen(bt, q):
        return (jnp.dot(bt[0], q[0], precision=hp, preferred_element_type=F32)
                - jnp.dot(bt[1], q[1], precision=hp, preferred_element_type=F32))

    w0 = gen(btr[0], q0f) + gen(btr[1], q0b)
    w1 = gen(btr[0], q1f)
    for mo in range(m):
        w_scr[:, 2 * mo * c:(2 * mo + 1) * c] = w0[:, mo * c:(mo + 1) * c]
        w_scr[:, (2 * mo + 1) * c:(2 * mo + 2) * c] = w1[:, mo * c:(mo + 1) * c]

    for mi in range(m):
        row = jnp.broadcast_to(w_scr[mi:mi + 1, :], (c, 2 * cm))
        win = pltpu.roll(row, 2 * cm - (c - 1), 1, stride=1, stride_axis=0)
        blk = jnp.concatenate([win[:, 2 * mo * c:(2 * mo + 1) * c] for mo in range(m)], axis=1)
        t_scr[mi * c:(mi + 1) * c, :] = blk.astype(BF16)

    for d, q in ((0, q1f), (1, qrb)):
        base = cm + 2 * p * d
        t_scr[base:base + p, :] = q[0].astype(BF16)
        t_scr[base + p:base + 2 * p, :] = (-q[1]).astype(BF16)

    wb_f = lane_tiles(lambda mi: _cmul(*col(bbar[0], mi), *f_down))
    wb_b = lane_tiles(lambda mi: _cmul(*col(bbar[1], mi), *b_up))
    wb = jnp.concatenate([wb_f[0], wb_f[1], wb_b[0], wb_b[1]], axis=0).astype(BF16)
    hloc = _dot_nt(x_scr[:, :cm], wb)

    sign = jnp.where(lax.broadcasted_iota(jnp.int32, (1, 2 * p), 1) < p, -1.0, 1.0)
    order_f = list(range(n_all))
    order_b = list(range(n_ctx - 1, -1, -1)) + list(range(n_all - 1, n_ctx - 1, -1))
    for d, order in ((0, order_f), (1, order_b)):
        lam = lamr2_ref[d]
        a_re, a_im, _, _ = _s5_disc(lam[0:1, :], lam[1:2, :], lam[2:3, :])
        mag = jnp.exp(a_re * c)
        ac_r = mag * jnp.cos(a_im * c)
        ac_i = mag * jnp.sin(a_im * c) * sign
        state = jnp.zeros((batch, 2 * p), F32)
        colx = cm + 2 * p * d
        for n in order:
            x_scr[n * batch:(n + 1) * batch, colx:colx + 2 * p] = state.astype(BF16)
            local = hloc[n * batch:(n + 1) * batch, 2 * p * d:2 * p * (d + 1)]
            state = state * ac_r + pltpu.roll(state, p, 1) * ac_i + local

    y = _dot(x_scr[...], t_scr[...])
    for mo in range(m):
        skip = d_ref[mo:mo + 1, :] * u_ref[pl.ds(mo, rows, stride=S5_ROW_PITCH), :]
        y_ref[pl.ds(mo, rows, stride=S5_ROW_PITCH), :] = y[:, mo * c:(mo + 1) * c] + skip
    for pad_row in range(m, S5_ROW_PITCH):
        y_ref[pl.ds(pad_row, rows, stride=S5_ROW_PITCH), :] = jnp.zeros((rows, c), F32)


def _s5_operands(u, lam_re, lam_im, log_dt, b_re, b_im, c_re, c_im, d_skip):
    g, n_all, batch, pitch, c = u.shape
    m = S5_GROUP
    dtb = jnp.broadcast_to(log_dt[:, :, None], lam_re.shape)
    lam3 = jnp.stack([lam_re, lam_im, dtb], axis=-1)
    lamc = jnp.pad(lam3, ((0, 0), (0, 0), (0, 0), (0, LANES - 3))).transpose(1, 0, 2, 3)
    lam3r = jnp.stack([lam_re, lam_im, dtb], axis=2)
    lamr = jnp.pad(lam3r, ((0, 0), (0, 0), (0, 5), (0, 0))).transpose(1, 0, 2, 3)
    lamr2 = jnp.concatenate([lamr, lamr], axis=-1)
    bcol = jnp.stack([b_re, b_im], axis=1)
    bc = jnp.pad(bcol, ((0, 0),) * 4 + ((0, LANES - m),)).transpose(2, 0, 1, 3, 4)
    brow = jnp.swapaxes(bcol, -1, -2).transpose(2, 0, 1, 3, 4)
    ctc = jnp.swapaxes(jnp.stack([c_re, c_im], axis=1), -1, -2)
    ctc = jnp.pad(ctc, ((0, 0),) * 4 + ((0, LANES - m),)).transpose(2, 0, 1, 3, 4)
    dvec = jnp.broadcast_to(d_skip.reshape(g, m, 1), (g, m, LANES))
    return (u.reshape(g, n_all * batch * pitch, c), lamc, lamr, lamr2, bc, brow, ctc, dvec)


def _s5_mixer(u, lam_re, lam_im, log_dt, b_re, b_im, c_re, c_im, d_skip, n_ctx):
    g, n_all, b, pitch, c = u.shape
    m, p = S5_GROUP, S5_STATE
    rows = n_all * b
    cm = c * m
    kern = functools.partial(_s5_kernel, batch=b, n_ctx=n_ctx // c, n_all=n_all)
    per_g = lambda *tail: pl.BlockSpec((None,) + tail, lambda i: (i,) + (0,) * len(tail))
    y = pl.pallas_call(
        kern,
        grid=(g,),
        in_specs=[per_g(rows * pitch, c), per_g(2, p, LANES), per_g(2, 8, p), per_g(2, 8, 2 * p),
                  per_g(2, 2, p, LANES), per_g(2, 2, m, p), per_g(2, 2, p, LANES), per_g(m, LANES)],
        out_specs=per_g(rows * pitch, c),
        out_shape=jax.ShapeDtypeStruct((g, rows * pitch, c), F32),
        scratch_shapes=[pltpu.VMEM((cm + 4 * p, cm), BF16), pltpu.VMEM((rows, cm + 4 * p), BF16),
                        pltpu.VMEM((m, 2 * cm), F32)],
        compiler_params=_cparams("arbitrary"),
        name="s5_scan",
    )(*_s5_operands(u, lam_re, lam_im, log_dt, b_re, b_im, c_re, c_im, d_skip))
    return y.reshape(u.shape)


def kernel(x, c, ctx, c_ctx, ada_w, ada_b, norm_g, ffn_w_in, ffn_w_out, mla_w_in, mla_q_norm, mla_kv_norm,
           mla_w_uq, mla_w_ukv, mla_q_gain, mla_k_gain, mla_w_o, ret_w_in, ret_log1m_gamma, ret_gn_gain, ret_w_o,
           s5_lam_re, s5_lam_im, s5_log_dt, s5_b_re, s5_b_im, s5_c_re, s5_c_im, s5_d, s5_glu_w, s5_glu_b):
    b, s, d = x.shape
    n_ctx = ctx.shape[1]
    depth = ada_w.shape[0]
    assert n_ctx % TOKEN_TILE == 0 and s % TOKEN_TILE == 0 and s % GRID_W == 0 and b % FFN_BATCH_PER_STEP == 0
    n_ctx_tiles = n_ctx // TOKEN_TILE

    pad = (-(b + 1)) % 8
    cvec = jnp.concatenate([c, c_ctx[None, :], jnp.zeros((pad, d), F32)], axis=0)
    mod_all = _modulation(cvec, ada_w, ada_b)
    mod_lat = mod_all[:, :b].reshape(depth, b, 1, N_MOD, d)
    mod_ctx = jnp.broadcast_to(mod_all[:, b].reshape(depth, 1, 1, N_MOD, d), (depth, b, 1, N_MOD, d))
    mods = jnp.concatenate([mod_ctx, mod_lat], axis=2)

    mla_tabs = _mla_tables(n_ctx, s)
    ret_tabs = _ret_tables(n_ctx, s)

    ffn_w_in_bf = ffn_w_in.astype(BF16)
    ffn_w_out_bf = ffn_w_out.astype(BF16)
    r = None
    for i in range(depth):
        kind, j = i % N_MIXERS, i // N_MIXERS
        mod = mods[i]
        ffn0 = (norm_g[i, 0], (ffn_w_in_bf, (i, 0)), (ffn_w_out_bf, (i, 0)))
        ffn1 = (norm_g[i, 2], (ffn_w_in_bf, (i, 1)), (ffn_w_out_bf, (i, 1)))
        if i == 0:
            assert kind != 2
            r = _ffn_first(ctx, x, mod, *ffn0, n_ctx_tiles)
        elif kind == 2:
            r, u = _ffn_s5_in(r, mod, *ffn0, norm_g[i, 1], n_ctx_tiles)
        else:
            r = _ffn(r, mod, *ffn0, n_ctx_tiles)
        tile0 = n_ctx_tiles if i == depth - 1 else 0
        if kind == 0:
            weights = _mla_weights(mla_w_in[j], mla_q_norm[j], mla_kv_norm[j], mla_w_uq[j], mla_w_ukv[j],
                                   mla_q_gain[j], mla_k_gain[j])
            q, k, v = _mla_proj(r, *mla_tabs, mod, norm_g[i, 1], weights, n_ctx_tiles)
            y = _attention(q, k, v, _attn_score_bound(mla_q_gain[j], mla_k_gain[j]), n_ctx, tile0)
            mix = (y, _rows_spec(y.shape[-1], tile0), [mla_w_o[j].astype(BF16)], _ffn_proj_kernel, "mla_out_swiglu")
        elif kind == 1:
            q, k, v, sg = _ret_proj(r, *ret_tabs, mod, norm_g[i, 1], ret_w_in[j].astype(BF16), n_ctx_tiles)
            y = _ret_scan(q, k, v, sg, ret_log1m_gamma[j], ret_gn_gain[j], n_ctx)
            mix = (y, _rows_spec(y.shape[-1], tile0), [ret_w_o[j].astype(BF16)], _ffn_proj_kernel, "ret_out_swiglu")
        else:
            y = _s5_mixer(u, s5_lam_re[j], s5_lam_im[j], s5_log_dt[j], s5_b_re[j], s5_b_im[j],
                          s5_c_re[j], s5_c_im[j], s5_d[j], n_ctx)
            mix = (y, _chunk_major_spec(y.shape[0], tile0), [s5_glu_w[j].astype(BF16), s5_glu_b[j].reshape(1, -1)],
                   _ffn_glu_kernel, "s5_out_swiglu")
        r = _ffn_after_mixer(r, *mix, mod, *ffn1, n_ctx_tiles, tile0)
    return r
```
